```python
import math
import jax
import jax.numpy as jnp
from jax import lax
import numpy as np

D_MODEL = 1024
BATCH = 8
SEQ = 2048
DEPTH = 2
DEC_BATCH = 32
DEC_SEQ = 4
PAST_LEN = 8192
PAGE_SIZE = 128

D_CONV = D_MODEL // 2
CONV_GROUPS = 8
CONV_WIDTH = 3
HEAD_DIM = 64
D_ATT = D_MODEL - D_CONV
N_HEADS = D_ATT // HEAD_DIM
N_KV = 2
GROUP = N_HEADS // N_KV
D_KV = N_KV * HEAD_DIM
CMP_LEN = 32
CMP_STRIDE = 16
CMP_HID = 2 * HEAD_DIM
SEL_BLOCK = 64
SEL_TOPK = 16
WINDOW = 512
NUM_BUCKETS = 32
MAX_DISTANCE = 128
N_GATES = 3 * N_HEADS
SPLIT_SIZES = (D_CONV, D_CONV, D_CONV, D_ATT, D_KV, D_KV, D_KV, D_KV, D_KV, D_KV, N_GATES)
D_IN = 3 * D_CONV + D_ATT + 6 * D_KV + N_GATES
D_FF = ((8 * D_MODEL + 3 * 256 - 1) // (3 * 256)) * 256
Q_BLOCK = 64
EPS = 1e-6
SCALE = HEAD_DIM ** -0.5
F32 = jnp.float32
NEG = -1e30

kernel_name = 'hybrid_conv_nsa_decoder_step'


def rmsnorm(x, g):
    xf = x.astype(F32)
    y = xf * lax.rsqrt(jnp.mean(xf * xf, axis=-1, keepdims=True) + EPS)
    return (y * g.astype(F32)).astype(x.dtype)


def t5_bucket(dist):
    n = jnp.maximum(dist, 0)
    max_exact = NUM_BUCKETS // 2
    nf = jnp.maximum(n, 1).astype(F32)
    large = max_exact + (jnp.log(nf / max_exact) / math.log(MAX_DISTANCE / max_exact)
                         * (NUM_BUCKETS - max_exact)).astype(jnp.int32)
    large = jnp.minimum(large, NUM_BUCKETS - 1)
    return jnp.where(n < max_exact, n, large)


def bias_shared(dist, tbl):
    b = tbl[t5_bucket(dist)]
    return jnp.moveaxis(b, -1, 0).reshape(N_KV, GROUP, *dist.shape)


def bias_per_group(dist, tbl):
    tg = tbl.reshape(NUM_BUCKETS, N_KV, GROUP)
    b = jax.vmap(lambda t, d: t[t5_bucket(d)], in_axes=(1, 1), out_axes=1)(tg, dist)
    return jnp.moveaxis(b, -1, 2)


def masked_softmax(logits, mask, axes):
    l = jnp.where(mask, logits.astype(F32), NEG)
    m = jnp.max(l, axis=axes, keepdims=True)
    e = jnp.where(mask, jnp.exp(l - m), 0.0)
    s = jnp.sum(e, axis=axes, keepdims=True)
    return e / jnp.where(s > 0, s, 1.0)


def compress(rows, pe, w1, w2):
    t_all = rows.shape[1]
    n_cmp = (t_all - CMP_LEN) // CMP_STRIDE + 1
    idx = jnp.arange(n_cmp)[:, None] * CMP_STRIDE + jnp.arange(CMP_LEN)[None, :]
    blocks = rows[:, idx] + pe[None, None, :, None, :]
    hid = jax.nn.gelu(jnp.einsum('bnlgd,lde->bnge', blocks, w1))
    return jnp.einsum('bnge,ed->bngd', hid, w2)


def sel_attend(q, q_pos, kb, vb, idx, tbl):
    take = jax.vmap(jax.vmap(lambda blocks, i: blocks[i]))
    k_sel = take(kb, idx)
    v_sel = take(vb, idx)
    k_pos = idx[..., None] * SEL_BLOCK + jnp.arange(SEL_BLOCK, dtype=jnp.int32)
    dist = q_pos[None, None, :, None, None] - k_pos
    logits = jnp.einsum('bqgrd,bgqksd->bgrqks', q, k_sel, preferred_element_type=F32) * SCALE
    logits = logits + bias_per_group(dist, tbl)
    p = masked_softmax(logits, (dist >= 0)[:, :, None], (-2, -1))
    return jnp.einsum('bgrqks,bgqksd->bqgrd', p, v_sel)


def win_attend(q, q_pos, k, v, k_pos, tbl):
    dist = q_pos[:, None] - k_pos[None, :]
    mask = (dist >= 0) & (dist < WINDOW) & (k_pos >= 0)[None, :]
    logits = jnp.einsum('bqgrd,bkgd->bgrqk', q, k, preferred_element_type=F32) * SCALE
    logits = logits + bias_shared(dist, tbl)
    p = masked_softmax(logits, mask, (-1,))
    return jnp.einsum('bgrqk,bkgd->bqgrd', p, v)


def nsa(q, pos0, k_cmp, v_cmp, ks, vs, kw, vw, kw_start, g, tbl, blocked):
    B, T = q.shape[:2]
    q_pos = pos0 + jnp.arange(T, dtype=jnp.int32)
    n_cmp = k_cmp.shape[1]
    blk_end = jnp.arange(n_cmp, dtype=jnp.int32) * CMP_STRIDE + (CMP_LEN - 1)
    dist = q_pos[:, None] - blk_end[None, :]
    logits = jnp.einsum('bqgrd,bngd->bgrqn', q, k_cmp, preferred_element_type=F32) * SCALE
    logits = logits + bias_shared(dist, tbl)
    p_cmp = masked_softmax(logits, dist >= 0, (-1,))
    o_cmp = jnp.einsum('bgrqn,bngd->bqgrd', p_cmp, v_cmp)
    t_all = ks.shape[1]
    n_sel = -(-t_all // SEL_BLOCK)
    c0 = jnp.arange(n_cmp)[:, None] * CMP_STRIDE
    s0 = jnp.arange(n_sel)[None, :] * SEL_BLOCK
    cover = jnp.clip(jnp.minimum(c0 + CMP_LEN, s0 + SEL_BLOCK) - jnp.maximum(c0, s0), 0).astype(F32) / CMP_STRIDE
    score = jnp.einsum('bgrqn,nj->bgqj', p_cmp, cover)
    j = jnp.arange(n_sel, dtype=jnp.int32)[None, :]
    cur = (q_pos // SEL_BLOCK)[:, None]
    forced = (j == 0) | (j == cur) | (j == cur - 1)
    future = j * SEL_BLOCK > q_pos[:, None]
    score = jnp.where(forced, 1e6, jnp.where(future, -1e6, score))
    idx = lax.top_k(score, min(SEL_TOPK, n_sel))[1]
    pad = n_sel * SEL_BLOCK - t_all

    def to_blocks(a):
        a = jnp.pad(a, ((0, 0), (0, pad), (0, 0), (0, 0)))
        return a.reshape(B, n_sel, SEL_BLOCK, N_KV, HEAD_DIM).transpose(0, 3, 1, 2, 4)

    kb, vb = to_blocks(ks), to_blocks(vs)
    if blocked:
        nq = T // Q_BLOCK
        qc = jnp.moveaxis(q.reshape(B, nq, Q_BLOCK, N_KV, GROUP, HEAD_DIM), 1, 0)
        ic = jnp.moveaxis(idx.reshape(B, N_KV, nq, Q_BLOCK, idx.shape[-1]), 2, 0)
        wpad = ((0, 0), (WINDOW, 0), (0, 0), (0, 0))
        kwp, vwp = jnp.pad(kw, wpad), jnp.pad(vw, wpad)

        def body(args):
            c, qb, ib = args
            s = pos0 + c * Q_BLOCK
            qp = s + jnp.arange(Q_BLOCK, dtype=jnp.int32)
            o_s = sel_attend(qb, qp, kb, vb, ib, tbl)
            off = s - kw_start
            kk = lax.dynamic_slice_in_dim(kwp, off, WINDOW + Q_BLOCK, axis=1)
            vv = lax.dynamic_slice_in_dim(vwp, off, WINDOW + Q_BLOCK, axis=1)
            kp = s - WINDOW + jnp.arange(WINDOW + Q_BLOCK, dtype=jnp.int32)
            o_w = win_attend(qb, qp, kk, vv, kp, tbl)
            return o_s, o_w

        o_s, o_w = lax.map(body, (jnp.arange(nq, dtype=jnp.int32), qc, ic))
        o_sel = jnp.moveaxis(o_s, 0, 1).reshape(B, T, N_KV, GROUP, HEAD_DIM)
        o_win = jnp.moveaxis(o_w, 0, 1).reshape(B, T, N_KV, GROUP, HEAD_DIM)
    else:
        o_sel = sel_attend(q, q_pos, kb, vb, idx, tbl)
        kp = kw_start + jnp.arange(kw.shape[1], dtype=jnp.int32)
        o_win = win_attend(q, q_pos, kw, vw, kp, tbl)
    o = g[:, :, 0] * o_cmp + g[:, :, 1] * o_sel + g[:, :, 2] * o_win
    return o.reshape(B, T, D_ATT)


def layer(x, pos0, lw, tbl, past):
    B, T, _ = x.shape
    h = rmsnorm(x, lw['norm_mix'])
    proj = h @ lw['w_in']
    cuts = [int(c) for c in np.cumsum(SPLIT_SIZES)[:-1]]
    gb, gc, hv, q, kc, vc, ks, vs, kw, vw, gl = jnp.split(proj, cuts, axis=-1)
    u = gc * hv
    buf = jnp.zeros((B, CONV_WIDTH - 1, D_CONV), u.dtype) if past is None else past['conv'].astype(u.dtype)
    up = jnp.concatenate([buf, u], axis=1)
    cw = lw['conv_w']
    conv = sum(up[:, k:k + T] * cw[k] for k in range(CONV_WIDTH))
    y_conv = gb * conv
    new_conv = up[:, -(CONV_WIDTH - 1):]
    q = rmsnorm(q.reshape(B, T, N_HEADS, HEAD_DIM), lw['q_norm']).reshape(B, T, N_KV, GROUP, HEAD_DIM)
    kv_shape = (B, T, N_KV, HEAD_DIM)
    kc = kc.reshape(kv_shape)
    vc = vc.reshape(kv_shape)
    ks = rmsnorm(ks.reshape(kv_shape), lw['k_norm'][1])
    vs = vs.reshape(kv_shape)
    kw = rmsnorm(kw.reshape(kv_shape), lw['k_norm'][2])
    vw = vw.reshape(kv_shape)
    if past is None:
        kc_all, vc_all, ks_all, vs_all, kw_all, vw_all = kc, vc, ks, vs, kw, vw
    else:
        def cat(a, b):
            return jnp.concatenate([a.astype(b.dtype), b], axis=1)
        kc_all, vc_all = cat(past['kc'], kc), cat(past['vc'], vc)
        ks_all, vs_all = cat(past['ks'], ks), cat(past['vs'], vs)
        kw_all, vw_all = cat(past['kw'], kw), cat(past['vw'], vw)
    kw_start = pos0 + T - kw_all.shape[1]
    k_cmp = rmsnorm(compress(kc_all, lw['cmp_pe'][0], lw['cmp_w1'][0], lw['cmp_w2'][0]), lw['k_norm'][0])
    v_cmp = compress(vc_all, lw['cmp_pe'][1], lw['cmp_w1'][1], lw['cmp_w2'][1])
    g = jax.nn.sigmoid(gl.astype(F32)).reshape(B, T, 3, N_KV, GROUP, 1)
    o_att = nsa(q, pos0, k_cmp, v_cmp, ks_all, vs_all, kw_all, vw_all, kw_start, g, tbl, past is None)
    mix = jnp.concatenate([rmsnorm(y_conv, lw['out_norm_conv']),
                           rmsnorm(o_att.astype(x.dtype), lw['out_norm_att'])], axis=-1) @ lw['w_out']
    x = x + mix
    h2 = rmsnorm(x, lw['norm_ffn'])
    x = x + (jax.nn.silu(h2 @ lw['w_gate']) * (h2 @ lw['w_up'])) @ lw['w_down']
    n_keep = min(WINDOW, kw_all.shape[1])
    state = (new_conv, kc, vc, ks, vs, kw_all[:, -n_keep:], vw_all[:, -n_keep:])
    return x, state


def setup_inputs(seed: int = 0) -> dict:
    key = jax.random.key(seed)
    k = jax.random.split(key, 32)
    n_pages = PAST_LEN // PAGE_SIZE
    n_pool = (5 * DEC_BATCH * n_pages + 3) // 4
    n_win = min(WINDOW, PAST_LEN)

    def nrm(kk, shape, s):
        return s * jax.random.normal(kk, shape, F32)

    def gain(kk, shape):
        return 1.0 + 0.02 * jax.random.normal(kk, shape, F32)

    page_table = jax.random.permutation(k[0], n_pool)[:DEC_BATCH * n_pages].reshape(DEC_BATCH, n_pages).astype(jnp.int32)
    pool = (DEPTH, n_pool, PAGE_SIZE, N_KV, HEAD_DIM)
    win = (DEPTH, DEC_BATCH, n_win, N_KV, HEAD_DIM)
    return {
        'x_prompt': nrm(k[1], (BATCH, SEQ, D_MODEL), 1.0),
        'x_sample': nrm(k[2], (DEC_BATCH, DEC_SEQ, D_MODEL), 1.0),
        'state_conv': nrm(k[3], (DEPTH, DEC_BATCH, CONV_WIDTH - 1, D_CONV), 1.0),
        'cache_cmp_k': nrm(k[4], pool, 1.0),
        'cache_cmp_v': nrm(k[5], pool, 1.0),
        'cache_slc_k': nrm(k[6], pool, 1.0),
        'cache_slc_v': nrm(k[7], pool, 1.0),
        'cache_win_k': nrm(k[8], win, 1.0),
        'cache_win_v': nrm(k[9], win, 1.0),
        'page_table': page_table,
        'rel_bias': nrm(k[10], (NUM_BUCKETS, N_HEADS), 0.5),
        'norm_mix': gain(k[11], (DEPTH, D_MODEL)),
        'w_in': nrm(k[12], (DEPTH, D_MODEL, D_IN), D_MODEL ** -0.5),
        'conv_w': nrm(k[13], (DEPTH, CONV_WIDTH, D_CONV), CONV_WIDTH ** -0.5),
        'q_norm': gain(k[14], (DEPTH, HEAD_DIM)),
        'k_norm': gain(k[15], (DEPTH, 3, HEAD_DIM)),
        'cmp_pe': nrm(k[16], (DEPTH, 2, CMP_LEN, HEAD_DIM), 0.1),
        'cmp_w1': nrm(k[17], (DEPTH, 2, CMP_LEN, HEAD_DIM, CMP_HID), (CMP_LEN * HEAD_DIM) ** -0.5),
        'cmp_w2': nrm(k[18], (DEPTH, 2, CMP_HID, HEAD_DIM), CMP_HID ** -0.5),
        'out_norm_conv': gain(k[19], (DEPTH, D_CONV)),
        'out_norm_att': gain(k[20], (DEPTH, D_ATT)),
        'w_out': nrm(k[21], (DEPTH, D_MODEL, D_MODEL), D_MODEL ** -0.5),
        'norm_ffn': gain(k[22], (DEPTH, D_MODEL)),
        'w_gate': nrm(k[23], (DEPTH, D_MODEL, D_FF), D_MODEL ** -0.5),
        'w_up': nrm(k[24], (DEPTH, D_MODEL, D_FF), D_MODEL ** -0.5),
        'w_down': nrm(k[25], (DEPTH, D_FF, D_MODEL), D_FF ** -0.5),
    }


def reference(x_prompt, x_sample, state_conv, cache_cmp_k, cache_cmp_v, cache_slc_k, cache_slc_v,
              cache_win_k, cache_win_v, page_table, rel_bias, norm_mix, w_in, conv_w, q_norm, k_norm,
              cmp_pe, cmp_w1, cmp_w2, out_norm_conv, out_norm_att, w_out, norm_ffn, w_gate, w_up, w_down):
    n_seq, n_pages = page_table.shape
    past_len = n_pages * PAGE_SIZE

    def paged(c):
        return c[page_table].reshape(n_seq, past_len, N_KV, HEAD_DIM)

    yp, ys = x_prompt, x_sample
    sp, ss = [], []
    for l in range(DEPTH):
        lw = dict(norm_mix=norm_mix[l], w_in=w_in[l], conv_w=conv_w[l], q_norm=q_norm[l], k_norm=k_norm[l],
                  cmp_pe=cmp_pe[l], cmp_w1=cmp_w1[l], cmp_w2=cmp_w2[l], out_norm_conv=out_norm_conv[l],
                  out_norm_att=out_norm_att[l], w_out=w_out[l], norm_ffn=norm_ffn[l], w_gate=w_gate[l],
                  w_up=w_up[l], w_down=w_down[l])
        yp, st_p = layer(yp, 0, lw, rel_bias, None)
        sp.append(st_p)
        past = dict(conv=state_conv[l], kc=paged(cache_cmp_k[l]), vc=paged(cache_cmp_v[l]),
                    ks=paged(cache_slc_k[l]), vs=paged(cache_slc_v[l]), kw=cache_win_k[l], vw=cache_win_v[l])
        ys, st_s = layer(ys, past_len, lw, rel_bias, past)
        ss.append(st_s)
    conv_p, cmp_k_p, cmp_v_p, slc_k_p, slc_v_p, win_k_p, win_v_p = [jnp.stack(a) for a in zip(*sp)]
    conv_s, cmp_k_s, cmp_v_s, slc_k_s, slc_v_s, win_k_s, win_v_s = [jnp.stack(a) for a in zip(*ss)]
    return (yp, ys, conv_p, cmp_k_p, cmp_v_p, slc_k_p, slc_v_p, win_k_p, win_v_p,
            conv_s, cmp_k_s, cmp_v_s, slc_k_s, slc_v_s, win_k_s, win_v_s)
```

```python
import functools
import math

import jax
import jax.numpy as jnp
import numpy as np
from jax import lax
from jax.experimental import pallas as pl
from jax.experimental.pallas import tpu as pltpu

F32 = jnp.float32
BF16 = jnp.bfloat16

HEAD_DIM = 64
N_KV = 2
GROUP = 4
N_HEADS = N_KV * GROUP
D_KV = N_KV * HEAD_DIM
CMP_LEN = 32
CMP_STRIDE = 16
CMP_HID = 2 * HEAD_DIM
SEL_BLOCK = 64
SEL_TOPK = 16
WINDOW = 512
NUM_BUCKETS = 32
MAX_DISTANCE = 128
PAGE_SIZE = 128
CONV_WIDTH = 3
EPS = 1e-6
SCALE = HEAD_DIM ** -0.5
NEG = -1e30

LANES = 128
TQ = 128
PAGES_PER_STEP = 16
VMEM_LIMIT = 56 * 1024 * 1024

_NT = (((1,), (1,)), ((), ()))


def _dot(a, b):
    return jnp.dot(a, b, preferred_element_type=F32)


def _dot_nt(a, b):
    return lax.dot_general(a, b, _NT, preferred_element_type=F32)


def _params(n_axes):
    return pltpu.CompilerParams(dimension_semantics=("arbitrary",) * n_axes,
                                vmem_limit_bytes=VMEM_LIMIT)


def _resident(shape):
    nd = len(shape)
    return pl.BlockSpec(shape, lambda *_: (0,) * nd, pipeline_mode=pl.Buffered(1))


def _rms(x, gain):
    return x * lax.rsqrt(jnp.mean(x * x, axis=-1, keepdims=True) + EPS) * gain


def _pair_rms(y, gain, lo):
    y2 = y * y
    s_lo = jnp.sum(jnp.where(lo, y2, 0.0), axis=-1, keepdims=True)
    s_hi = jnp.sum(jnp.where(lo, 0.0, y2), axis=-1, keepdims=True)
    inv = jnp.where(lo, lax.rsqrt(s_lo / HEAD_DIM + EPS), lax.rsqrt(s_hi / HEAD_DIM + EPS))
    return y * inv * gain


def _in_proj_kernel(x_ref, nm_ref, w_ref, qg_ref, kg_ref,
                    gb_ref, u_ref, q_ref, kc_ref, vc_ref, ks_ref, vs_ref, kw_ref, vw_ref, gt_ref):
    x = x_ref[...]
    hb = _rms(x, nm_ref[...]).astype(BF16)
    d = gb_ref.shape[1]

    def seg(a, b):
        return _dot(hb, w_ref[:, a:b])

    gb_ref[...] = seg(0, d)
    u_ref[...] = seg(d, 2 * d) * seg(2 * d, 3 * d)
    lo = lax.broadcasted_iota(jnp.int32, (x.shape[0], LANES), 1) < HEAD_DIM
    c = 3 * d
    for j in range(q_ref.shape[1] // LANES):
        q_ref[:, j * LANES:(j + 1) * LANES] = _pair_rms(
            seg(c + j * LANES, c + (j + 1) * LANES), qg_ref[...], lo).astype(BF16)
    c += q_ref.shape[1]
    kc_ref[...] = seg(c, c + LANES)
    vc_ref[...] = seg(c + LANES, c + 2 * LANES)
    ks_ref[...] = _pair_rms(seg(c + 2 * LANES, c + 3 * LANES), kg_ref[0:1, :], lo)
    vs_ref[...] = seg(c + 3 * LANES, c + 4 * LANES)
    kw_ref[...] = _pair_rms(seg(c + 4 * LANES, c + 5 * LANES), kg_ref[1:2, :], lo)
    vw_ref[...] = seg(c + 5 * LANES, c + 6 * LANES)
    gt_ref[...] = jax.nn.sigmoid(seg(c + 6 * LANES, c + 7 * LANES))


def _in_proj(x, nm, w, qg, kg, d_conv, d_att, tm):
    n, dm = x.shape
    row = lambda w_: pl.BlockSpec((tm, w_), lambda i: (i, 0))
    widths = [d_conv, d_conv, d_att] + [LANES] * 7
    dtypes = [F32, F32, BF16] + [F32] * 7
    return pl.pallas_call(
        _in_proj_kernel,
        grid=(n // tm,),
        in_specs=[row(dm), _resident(nm.shape), _resident(w.shape), _resident(qg.shape), _resident(kg.shape)],
        out_specs=[row(w_) for w_ in widths],
        out_shape=[jax.ShapeDtypeStruct((n, w_), dt) for w_, dt in zip(widths, dtypes)],
        compiler_params=_params(1),
        name="in_proj",
    )(x, nm, w, qg, kg)


def _cmp_ab_kernel(*refs, n_in, n_prefetch=0):
    refs = refs[n_prefetch:]
    x_refs = refs[:n_in]
    w_ref, a_ref, b_ref = refs[n_in:]
    n_chunks = x_refs[0].shape[1] // CMP_STRIDE
    cols = []
    for l in range(CMP_STRIDE):
        parts = [xr[0, pl.ds(l, n_chunks, stride=CMP_STRIDE), :] for xr in x_refs]
        cols.append(parts[0] if n_in == 1 else jnp.concatenate(parts, axis=0))
    xr = jnp.concatenate(cols, axis=1).astype(BF16)
    ab = _dot(xr, w_ref[...])
    half = ab.shape[1] // 2
    a_ref[0] = ab[:, :half]
    b_ref[0] = ab[:, half:]


def _cmp_fin_kernel(a_ref, b_ref, pe_ref, w1_ref, w2_ref, g_ref, o_ref, *, n_valid, normalize):
    a = a_ref[0]
    bm = b_ref[0]
    n_chunks = a.shape[0]
    b_next = pltpu.roll(bm, n_chunks - 1, 0)
    cpe = jnp.dot(pe_ref[...], w1_ref[...], preferred_element_type=F32,
                  precision=lax.Precision.HIGHEST)[0:1]
    x = a + b_next + jnp.concatenate([cpe, cpe], axis=1)
    hid = x * (0.5 * (1.0 + jnp.tanh(math.sqrt(2.0 / math.pi) * (x + 0.044715 * (x * x * x)))))
    out = _dot(hid.astype(BF16), w2_ref[...])
    if normalize:
        lo = lax.broadcasted_iota(jnp.int32, out.shape, 1) < HEAD_DIM
        out = _pair_rms(out, g_ref[...], lo)
    rows = lax.broadcasted_iota(jnp.int32, out.shape, 0)
    o_ref[0] = jnp.where(rows < n_valid, out, 0.0)


def _cmp_fin(a, b, cw, n_valid, normalize):
    nb, n_chunks, wid = a.shape
    blk = pl.BlockSpec((1, n_chunks, wid), lambda i: (i, 0, 0))
    return pl.pallas_call(
        functools.partial(_cmp_fin_kernel, n_valid=n_valid, normalize=normalize),
        grid=(nb,),
        in_specs=[blk, blk, _resident(cw["pe"].shape), _resident(cw["w1f"].shape),
                  _resident(cw["w2bd"].shape), _resident(cw["gain"].shape)],
        out_specs=pl.BlockSpec((1, n_chunks, LANES), lambda i: (i, 0, 0)),
        out_shape=jax.ShapeDtypeStruct((nb, n_chunks, LANES), F32),
        compiler_params=_params(1),
        name="cmp_fin",
    )(a, b, cw["pe"], cw["w1f"], cw["w2bd"], cw["gain"])


def _compress_rows(rows, cw, normalize):
    nb, t, _ = rows.shape
    n_chunks = t // CMP_STRIDE
    wid = cw["wab"].shape[1] // 2
    out_blk = pl.BlockSpec((1, n_chunks, wid), lambda i: (i, 0, 0))
    a, b = pl.pallas_call(
        functools.partial(_cmp_ab_kernel, n_in=1),
        grid=(nb,),
        in_specs=[pl.BlockSpec((1, t, LANES), lambda i: (i, 0, 0)), _resident(cw["wab"].shape)],
        out_specs=[out_blk, out_blk],
        out_shape=[jax.ShapeDtypeStruct((nb, n_chunks, wid), F32)] * 2,
        compiler_params=_params(1),
        name="cmp_ab_rows",
    )(rows, cw["wab"])
    return _cmp_fin(a, b, cw, (t - CMP_LEN) // CMP_STRIDE + 1, normalize)


def _page_spec(i, base, pages_per_step):
    return pl.BlockSpec((1, PAGE_SIZE, LANES),
                        lambda b, s, pt: (base + pt[b, s * pages_per_step + i], 0, 0))


def _compress_paged(cache, page_table, base, cw, normalize):
    nb, n_pages = page_table.shape
    pps = PAGES_PER_STEP
    ch = pps * PAGE_SIZE // CMP_STRIDE
    wid = cw["wab"].shape[1] // 2
    n_chunks = n_pages * PAGE_SIZE // CMP_STRIDE
    out_blk = pl.BlockSpec((1, ch, wid), lambda b, s, pt: (b, s, 0))
    a, b = pl.pallas_call(
        functools.partial(_cmp_ab_kernel, n_in=pps, n_prefetch=1),
        grid_spec=pltpu.PrefetchScalarGridSpec(
            num_scalar_prefetch=1,
            grid=(nb, n_pages // pps),
            in_specs=[_page_spec(i, base, pps) for i in range(pps)]
            + [pl.BlockSpec(cw["wab"].shape, lambda b, s, pt: (0, 0), pipeline_mode=pl.Buffered(1))],
            out_specs=[out_blk, out_blk]),
        out_shape=[jax.ShapeDtypeStruct((nb, n_chunks, wid), F32)] * 2,
        compiler_params=_params(2),
        name="cmp_ab_paged",
    )(page_table, *([cache] * pps), cw["wab"])
    return _cmp_fin(a, b, cw, n_chunks - 1, normalize)


def _softmax_rows(s, mask):
    s = jnp.where(mask, s, NEG)
    m = jnp.max(s, axis=-1, keepdims=True)
    e = jnp.where(mask, jnp.exp(s - m), 0.0)
    l = jnp.sum(e, axis=-1, keepdims=True)
    return e / jnp.where(l > 0, l, 1.0)


def _online_update(state, s, mask, v):
    m, l, acc = state
    s = jnp.where(mask, s, NEG)
    m_new = jnp.maximum(m, jnp.max(s, axis=-1, keepdims=True))
    alpha = jnp.exp(m - m_new)
    e = jnp.where(mask, jnp.exp(s - m_new), 0.0)
    l = alpha * l + jnp.sum(e, axis=-1, keepdims=True)
    acc = alpha * acc + _dot(e.astype(BF16), v)
    return m_new, l, acc


def _finish(state):
    _, l, acc = state
    return acc / jnp.where(l > 0, l, 1.0)


def _split_dot_nt(a_bf16, x):
    hi = x.astype(BF16)
    lo = (x - hi.astype(F32)).astype(BF16)
    return _dot_nt(a_bf16, hi) + _dot_nt(a_bf16, lo)


def _pattn_kernel(q_ref, gt_ref, kcmp_ref, vcmp_ref, ks_ref, vs_ref, kw_ref, vw_ref,
                  bcmp_ref, toep_ref, far_ref, covt_ref, o_ref, *, n_sel, topk):
    qi = pl.program_id(1)
    q0 = qi * TQ
    rows4 = GROUP * TQ
    lane = lax.broadcasted_iota(jnp.int32, (TQ, LANES), 1)
    sub = lax.broadcasted_iota(jnp.int32, (TQ, LANES), 0)
    lo = lane < HEAD_DIM
    lane4 = lax.broadcasted_iota(jnp.int32, (rows4, LANES), 1)
    tpos4 = q0 + (lax.broadcasted_iota(jnp.int32, (rows4, LANES), 0) & (TQ - 1))
    gt = gt_ref[...]
    kcmp = kcmp_ref[0].astype(BF16)
    vcmp = vcmp_ref[0].astype(BF16)
    covt = covt_ref[...]
    n_slabs = q_ref.shape[2] // LANES
    outs = []
    for g in range(N_KV):
        half = lo if g == 0 else jnp.logical_not(lo)
        qm = jnp.concatenate(
            [jnp.where(half, q_ref[0, :, j * LANES:(j + 1) * LANES], jnp.zeros((), BF16))
             for j in range(n_slabs)], axis=0)

        s = _dot_nt(qm, kcmp) * SCALE + bcmp_ref[g, 0]
        p = _softmax_rows(s, lane4 * CMP_STRIDE + (CMP_LEN - 1) <= tpos4)
        o_cmp = _dot(p.astype(BF16), vcmp)
        p_sum = p[0:TQ]
        for j in range(1, n_slabs):
            p_sum = p_sum + p[j * TQ:(j + 1) * TQ]

        sc = _split_dot_nt(covt, p_sum)
        tq_pos = q0 + lane
        cur = tq_pos >> 6
        forced = (sub == 0) | (sub == cur) | (sub == cur - 1)
        sc = jnp.where(forced, 1e6, jnp.where(sub * SEL_BLOCK > tq_pos, -1e6, sc))
        sc = sc[0:n_sel]
        blk = sub[0:n_sel]
        rank = jnp.zeros((n_sel, LANES), F32)
        for i in range(n_sel):
            si = sc[i:i + 1, :]
            beats = (si > sc) | ((si == sc) & (blk > i))
            rank = rank + jnp.where(beats, 1.0, 0.0)
        sel_t = jnp.where(rank < topk, 1.0, 0.0)
        if n_sel < LANES:
            sel_t = jnp.concatenate([sel_t, jnp.zeros((LANES - n_sel, LANES), F32)], axis=0)
        sel = sel_t.T.astype(BF16)

        def sel_mask(kt):
            expand = jnp.where(sub == 2 * kt + (lane >> 6), 1.0, 0.0).astype(BF16)
            hit = _dot(sel, expand)
            return jnp.concatenate([hit] * n_slabs, axis=0) > 0.5

        def tile(k_ref, v_ref, kt):
            start = pl.multiple_of(kt * TQ, TQ)
            k = k_ref[0, pl.ds(start, TQ), :].astype(BF16)
            v = v_ref[0, pl.ds(start, TQ), :].astype(BF16)
            return _dot_nt(qm, k) * SCALE, v

        init = (jnp.full((rows4, 1), NEG, F32), jnp.zeros((rows4, 1), F32), jnp.zeros((rows4, LANES), F32))
        far = far_ref[g]
        k_near = jnp.maximum(qi - 1, 0)
        near = lane4 < jnp.where(qi >= 1, LANES, 0)
        causal = lane4 <= (tpos4 - q0)

        def sel_far(kt, st):
            s_, v_ = tile(ks_ref, vs_ref, kt)
            return _online_update(st, s_ + far, sel_mask(kt), v_)

        st = lax.fori_loop(0, k_near, sel_far, init)
        s_, v_ = tile(ks_ref, vs_ref, k_near)
        st = _online_update(st, s_ + toep_ref[1, g], sel_mask(k_near) & near, v_)
        s_, v_ = tile(ks_ref, vs_ref, qi)
        st = _online_update(st, s_ + toep_ref[0, g], sel_mask(qi) & causal, v_)
        o_sel = _finish(st)

        def win_far(kt, st):
            s_, v_ = tile(kw_ref, vw_ref, kt)
            dist = tpos4 - (kt * TQ + lane4)
            return _online_update(st, s_ + far, dist < WINDOW, v_)

        st = lax.fori_loop(jnp.maximum(qi - WINDOW // TQ, 0), k_near, win_far, init)
        s_, v_ = tile(kw_ref, vw_ref, k_near)
        st = _online_update(st, s_ + toep_ref[1, g], near, v_)
        s_, v_ = tile(kw_ref, vw_ref, qi)
        st = _online_update(st, s_ + toep_ref[0, g], causal, v_)
        o_win = _finish(st)

        per_head = []
        for j in range(n_slabs):
            r = slice(j * TQ, (j + 1) * TQ)
            c = g * GROUP + j
            per_head.append(gt[:, c:c + 1] * o_cmp[r]
                            + gt[:, N_HEADS + c:N_HEADS + c + 1] * o_sel[r]
                            + gt[:, 2 * N_HEADS + c:2 * N_HEADS + c + 1] * o_win[r])
        outs.append(per_head)
    for j in range(n_slabs):
        o_ref[0, :, j * LANES:(j + 1) * LANES] = jnp.where(lo, outs[0][j], outs[1][j])


def _pattn(q, gates, kcmp, vcmp, ks, vs, kw, vw, tabs):
    nb, t, d_att = q.shape
    n_q = t // TQ
    n_sel = -(-t // SEL_BLOCK)
    seq = pl.BlockSpec((1, t, LANES), lambda b, i: (b, 0, 0))
    cmpb = pl.BlockSpec((1, kcmp.shape[1], LANES), lambda b, i: (b, 0, 0))
    return pl.pallas_call(
        functools.partial(_pattn_kernel, n_sel=n_sel, topk=min(SEL_TOPK, n_sel)),
        grid=(nb, n_q),
        in_specs=[pl.BlockSpec((1, TQ, d_att), lambda b, i: (b, i, 0)),
                  pl.BlockSpec((TQ, LANES), lambda b, i: (b * n_q + i, 0)),
                  cmpb, cmpb, seq, seq, seq, seq,
                  pl.BlockSpec((N_KV, 1, GROUP * TQ, LANES), lambda b, i: (0, i, 0, 0)),
                  _resident(tabs["toep"].shape), _resident(tabs["far"].shape), _resident(tabs["covt"].shape)],
        out_specs=pl.BlockSpec((1, TQ, d_att), lambda b, i: (b, i, 0)),
        out_shape=jax.ShapeDtypeStruct((nb, t, d_att), F32),
        compiler_params=_params(2),
        name="pattn",
    )(q, gates, kcmp, vcmp, ks, vs, kw, vw, tabs["bcmp"], tabs["toep"], tabs["far"], tabs["covt"])


def _sattn1_kernel(qm_ref, kcmp_ref, vcmp_ref, kwc_ref, vwc_ref, kwn_ref, vwn_ref,
                   bcmp_ref, bwc_ref, bwn_ref, cov_ref, ocmp_ref, owin_ref, sel_ref,
                   *, n_cmp, n_sel, topk, t_new, pos0, win_start):
    qm = qm_ref[0]
    n_rows = qm.shape[0]
    n_gt = N_KV * t_new
    kcmp = kcmp_ref[0].astype(BF16)
    s = _dot_nt(qm, kcmp) * SCALE + bcmp_ref[...]
    ncol = lax.broadcasted_iota(jnp.int32, s.shape, 1)
    trow = pos0 + (lax.broadcasted_iota(jnp.int32, s.shape, 0) & (t_new - 1))
    p = _softmax_rows(s, (ncol < n_cmp) & (ncol * CMP_STRIDE + (CMP_LEN - 1) <= trow))
    ocmp_ref[0] = _dot(p.astype(BF16), vcmp_ref[0].astype(BF16))
    p_sum = p[0:n_gt]
    for j in range(1, n_rows // n_gt):
        p_sum = p_sum + p[j * n_gt:(j + 1) * n_gt]
    hi = p_sum.astype(BF16)
    lo = (p_sum - hi.astype(F32)).astype(BF16)
    sc = _dot(hi, cov_ref[...]) + _dot(lo, cov_ref[...])
    blk = lax.broadcasted_iota(jnp.int32, sc.shape, 1)
    qpos = pos0 + (lax.broadcasted_iota(jnp.int32, sc.shape, 0) & (t_new - 1))
    cur = qpos >> 6
    forced = (blk == 0) | (blk == cur) | (blk == cur - 1)
    sc = jnp.where(forced, 1e6, jnp.where(blk * SEL_BLOCK > qpos, -1e6, sc))
    sc = jnp.where(blk < n_sel, sc, -3e38)
    rank = jnp.zeros(sc.shape, F32)
    for i in range(n_sel):
        si = sc[:, i:i + 1]
        beats = (si > sc) | ((si == sc) & (blk > i))
        rank = rank + jnp.where(beats, 1.0, 0.0)
    sel_ref[0] = jnp.where((rank < topk) & (blk < n_sel), 1.0, 0.0)

    init = (jnp.full((n_rows, 1), NEG, F32), jnp.zeros((n_rows, 1), F32), jnp.zeros((n_rows, LANES), F32))
    s = _dot_nt(qm, kwc_ref[0].astype(BF16)) * SCALE + bwc_ref[...]
    trow = pos0 + (lax.broadcasted_iota(jnp.int32, s.shape, 0) & (t_new - 1))
    dist = trow - (win_start + lax.broadcasted_iota(jnp.int32, s.shape, 1))
    st = _online_update(init, s, (dist >= 0) & (dist < WINDOW), vwc_ref[0].astype(BF16))
    s = _dot_nt(qm, kwn_ref[0].astype(BF16)) * SCALE + bwn_ref[...]
    trow = pos0 + (lax.broadcasted_iota(jnp.int32, s.shape, 0) & (t_new - 1))
    col = lax.broadcasted_iota(jnp.int32, s.shape, 1)
    dist = trow - (pos0 + col)
    st = _online_update(st, s, (dist >= 0) & (dist < WINDOW) & (col < t_new), vwn_ref[0].astype(BF16))
    owin_ref[0] = _finish(st)


def _sattn2_kernel(pt_ref, *refs, pps, t_new, pos0):
    k_refs = refs[:pps]
    v_refs = refs[pps:2 * pps]
    (qm_ref, selc_ref, seln_ref, ksn_ref, vsn_ref, bsel_ref, bseln_ref, exp_ref,
     ocmp_ref, owin_ref, gate_ref, o_ref, m_sc, l_sc, acc_sc) = refs[2 * pps:]
    step = pl.program_id(1)
    qm = qm_ref[0]
    n_rows = qm.shape[0]
    reps = n_rows // selc_ref.shape[2]

    @pl.when(step == 0)
    def _():
        s = _dot_nt(qm, ksn_ref[0].astype(BF16)) * SCALE + bseln_ref[...]
        trow = lax.broadcasted_iota(jnp.int32, s.shape, 0) & (t_new - 1)
        col = lax.broadcasted_iota(jnp.int32, s.shape, 1)
        hit = jnp.concatenate([seln_ref[0]] * reps, axis=0) > 0.5
        init = (jnp.full((n_rows, 1), NEG, F32), jnp.zeros((n_rows, 1), F32),
                jnp.zeros((n_rows, LANES), F32))
        m, l, acc = _online_update(init, s, hit & (col <= trow) & (col < t_new), vsn_ref[0].astype(BF16))
        m_sc[...] = m
        l_sc[...] = l
        acc_sc[...] = acc

    k = jnp.concatenate([r[0] for r in k_refs], axis=0).astype(BF16)
    v = jnp.concatenate([r[0] for r in v_refs], axis=0).astype(BF16)
    s = _dot_nt(qm, k) * SCALE + bsel_ref[...]
    hit = _dot(selc_ref[0, 0].astype(BF16), exp_ref[...])
    mask = jnp.concatenate([hit] * reps, axis=0) > 0.5
    m, l, acc = _online_update((m_sc[...], l_sc[...], acc_sc[...]), s, mask, v)
    m_sc[...] = m
    l_sc[...] = l
    acc_sc[...] = acc

    @pl.when(step == pl.num_programs(1) - 1)
    def _():
        o_sel = _finish((m, l, acc))
        o_ref[0] = gate_ref[0, 0] * ocmp_ref[0] + gate_ref[0, 1] * o_sel + gate_ref[0, 2] * owin_ref[0]


def _sattn(qm, kcmp, vcmp, kwc, vwc, kwn, vwn, ksn, vsn, cache_k, cache_v, page_table, base, gates, tabs,
           t_new, pos0):
    nb, n_rows, _ = qm.shape
    n_pages = page_table.shape[1]
    n_chunks = kcmp.shape[1]
    n_gt = N_KV * t_new
    n_sel = -(-(pos0 + t_new) // SEL_BLOCK)
    selw = tabs["cov"].shape[1]
    per_b = lambda shape: pl.BlockSpec((1,) + shape, lambda b: (b,) + (0,) * len(shape))
    ocmp, owin, sel = pl.pallas_call(
        functools.partial(_sattn1_kernel, n_cmp=n_chunks - 1, n_sel=n_sel, topk=min(SEL_TOPK, n_sel),
                          t_new=t_new, pos0=pos0, win_start=pos0 - kwc.shape[1]),
        grid=(nb,),
        in_specs=[per_b((n_rows, LANES)), per_b((n_chunks, LANES)), per_b((n_chunks, LANES)),
                  per_b(kwc.shape[1:]), per_b(vwc.shape[1:]), per_b(kwn.shape[1:]), per_b(vwn.shape[1:]),
                  _resident(tabs["bcmp"].shape), _resident(tabs["bwc"].shape), _resident(tabs["bwn"].shape),
                  _resident(tabs["cov"].shape)],
        out_specs=[per_b((n_rows, LANES)), per_b((n_rows, LANES)), per_b((n_gt, selw))],
        out_shape=[jax.ShapeDtypeStruct((nb, n_rows, LANES), F32), jax.ShapeDtypeStruct((nb, n_rows, LANES), F32),
                   jax.ShapeDtypeStruct((nb, n_gt, selw), F32)],
        compiler_params=_params(1),
        name="sattn1",
    )(qm, kcmp, vcmp, kwc, vwc, kwn, vwn, tabs["bcmp"], tabs["bwc"], tabs["bwn"], tabs["cov"])

    pps = PAGES_PER_STEP
    n_steps = n_pages // pps
    bps = pps * PAGE_SIZE // SEL_BLOCK
    n_cblk = n_pages * PAGE_SIZE // SEL_BLOCK
    selc = sel[:, :, :n_cblk].reshape(nb, n_gt, n_steps, bps).transpose(0, 2, 1, 3)
    selc = jnp.pad(selc, ((0, 0), (0, 0), (0, 0), (0, LANES - bps)))
    seln = jnp.broadcast_to(sel[:, :, n_cblk:n_cblk + 1], (nb, n_gt, LANES))

    keys = pps * PAGE_SIZE
    cst = lambda shape: pl.BlockSpec(shape, lambda b, s, pt: (0,) * len(shape), pipeline_mode=pl.Buffered(1))
    pb = lambda shape: pl.BlockSpec((1,) + shape, lambda b, s, pt: (b,) + (0,) * len(shape))
    return pl.pallas_call(
        functools.partial(_sattn2_kernel, pps=pps, t_new=t_new, pos0=pos0),
        grid_spec=pltpu.PrefetchScalarGridSpec(
            num_scalar_prefetch=1,
            grid=(nb, n_steps),
            in_specs=[_page_spec(i, base, pps) for i in range(pps)] * 2
            + [pb((n_rows, LANES)),
               pl.BlockSpec((1, 1, n_gt, LANES), lambda b, s, pt: (b, s, 0, 0)),
               pb((n_gt, LANES)), pb(ksn.shape[1:]), pb(vsn.shape[1:]),
               pl.BlockSpec((n_rows, keys), lambda b, s, pt: (0, s)),
               cst(tabs["bseln"].shape), cst(tabs["expand"].shape),
               pb((n_rows, LANES)), pb((n_rows, LANES)), pb((3, n_rows, LANES))],
            out_specs=pb((n_rows, LANES)),
            scratch_shapes=[pltpu.VMEM((n_rows, 1), F32), pltpu.VMEM((n_rows, 1), F32),
                            pltpu.VMEM((n_rows, LANES), F32)]),
        out_shape=jax.ShapeDtypeStruct((nb, n_rows, LANES), F32),
        compiler_params=_params(2),
        name="sattn2",
    )(page_table, *([cache_k] * pps), *([cache_v] * pps), qm, selc, seln, ksn, vsn,
      tabs["bsel"], tabs["bseln"], tabs["expand"], ocmp, owin, gates)


def _mix_ffn_kernel(x_ref, gb_ref, u_ref, u1_ref, u2_ref, oa_ref, cw_ref, nc_ref, na_ref, wo_ref,
                    nf_ref, wg_ref, wu_ref, wd_ref, y_ref, acc_ref):
    d = gb_ref.shape[1]
    conv = u2_ref[...] * cw_ref[0:1, :] + u1_ref[...] * cw_ref[1:2, :] + u_ref[...] * cw_ref[2:3, :]
    yc = _rms(gb_ref[...] * conv, nc_ref[...]).astype(BF16)
    oa = _rms(oa_ref[...], na_ref[...]).astype(BF16)
    x1 = x_ref[...] + (_dot(yc, wo_ref[0:d, :]) + _dot(oa, wo_ref[d:, :]))
    h2 = _rms(x1, nf_ref[...]).astype(BF16)
    acc_ref[...] = jnp.zeros_like(acc_ref)
    for c in range(wg_ref.shape[0]):
        gate = _dot(h2, wg_ref[c])
        act = (gate * jax.nn.sigmoid(gate)) * _dot(h2, wu_ref[c])
        acc_ref[...] += _dot(act.astype(BF16), wd_ref[c])
    y_ref[...] = x1 + acc_ref[...]


def _mix_ffn(x, gb, u, u1, u2, oa, lw, tm):
    n, dm = x.shape
    d = gb.shape[1]
    row = lambda w_: pl.BlockSpec((tm, w_), lambda i: (i, 0))
    consts = [lw["conv_w"], lw["norm_conv"], lw["norm_att"], lw["w_out"], lw["norm_ffn"],
              lw["w_gate"], lw["w_up"], lw["w_down"]]
    return pl.pallas_call(
        _mix_ffn_kernel,
        grid=(n // tm,),
        in_specs=[row(dm)] + [row(d)] * 5 + [_resident(c.shape) for c in consts],
        out_specs=row(dm),
        out_shape=jax.ShapeDtypeStruct((n, dm), F32),
        scratch_shapes=[pltpu.VMEM((tm, dm), F32)],
        compiler_params=_params(1),
        name="mix_ffn",
    )(x, gb, u, u1, u2, oa, *consts)


def _t5_bucket(dist):
    n = jnp.maximum(dist, 0)
    max_exact = NUM_BUCKETS // 2
    nf = jnp.maximum(n, 1).astype(F32)
    large = max_exact + (jnp.log(nf / max_exact) / math.log(MAX_DISTANCE / max_exact)
                         * (NUM_BUCKETS - max_exact)).astype(jnp.int32)
    return jnp.where(n < max_exact, n, jnp.minimum(large, NUM_BUCKETS - 1))


def _head_rows_bias(dist, tbl, t_inner):
    b = jnp.moveaxis(tbl[_t5_bucket(dist)], -1, 0)
    return b.reshape(N_KV, GROUP, t_inner, dist.shape[1]).transpose(1, 0, 2, 3)


def _cover(n_cmp, n_sel):
    c0 = np.arange(n_cmp)[:, None] * CMP_STRIDE
    s0 = np.arange(n_sel)[None, :] * SEL_BLOCK
    return np.clip(np.minimum(c0 + CMP_LEN, s0 + SEL_BLOCK) - np.maximum(c0, s0), 0, None) / CMP_STRIDE


def _prompt_tables(tbl, t):
    n_q = t // TQ
    n_chunks = t // CMP_STRIDE
    ar = jnp.arange(TQ, dtype=jnp.int32)
    toep = []
    for c in range(2):
        dist = c * TQ + ar[:, None] - ar[None, :]
        b = _head_rows_bias(dist, tbl, TQ)
        toep.append(b.transpose(1, 0, 2, 3).reshape(N_KV, GROUP * TQ, TQ))
    far = tbl[_t5_bucket(jnp.int32(MAX_DISTANCE + 1))]
    far = jnp.broadcast_to(far.reshape(N_KV, GROUP, 1, 1), (N_KV, GROUP, TQ, LANES)).reshape(N_KV, GROUP * TQ, LANES)
    tpos = jnp.arange(t, dtype=jnp.int32)
    blk_end = jnp.arange(n_chunks, dtype=jnp.int32) * CMP_STRIDE + (CMP_LEN - 1)
    b = _head_rows_bias(tpos[:, None] - blk_end[None, :], tbl, t)
    b = b.reshape(GROUP, N_KV, n_q, TQ, n_chunks).transpose(1, 2, 0, 3, 4).reshape(N_KV, n_q, GROUP * TQ, n_chunks)
    n_cmp = (t - CMP_LEN) // CMP_STRIDE + 1
    n_sel = -(-t // SEL_BLOCK)
    covt = np.zeros((LANES, n_chunks), np.float32)
    covt[:n_sel, :n_cmp] = _cover(n_cmp, n_sel).T
    return dict(toep=jnp.stack(toep), far=far, bcmp=b, covt=jnp.asarray(covt, BF16))


def _sample_tables(tbl, t_new, pos0, n_win, n_pad):
    n_chunks = pos0 // CMP_STRIDE
    n_cmp = n_chunks - 1
    n_sel = -(-(pos0 + t_new) // SEL_BLOCK)
    selw = -(-n_sel // LANES) * LANES
    qpos = pos0 + jnp.arange(t_new, dtype=jnp.int32)
    rows = lambda dist: _head_rows_bias(dist, tbl, t_new).reshape(N_HEADS * t_new, dist.shape[1])
    blk_end = jnp.arange(n_chunks, dtype=jnp.int32) * CMP_STRIDE + (CMP_LEN - 1)
    new_pos = pos0 + jnp.arange(n_pad, dtype=jnp.int32)
    cov = np.zeros((n_chunks, selw), np.float32)
    cov[:n_cmp, :n_sel] = _cover(n_cmp, n_sel)
    keys = PAGES_PER_STEP * PAGE_SIZE
    expand = (np.arange(LANES)[:, None] == np.arange(keys)[None, :] // SEL_BLOCK).astype(np.float32)
    return dict(
        bcmp=rows(qpos[:, None] - blk_end[None, :]),
        bwc=rows(qpos[:, None] - (pos0 - n_win + jnp.arange(n_win, dtype=jnp.int32))[None, :]),
        bwn=rows(qpos[:, None] - new_pos[None, :]),
        bsel=rows(qpos[:, None] - jnp.arange(pos0, dtype=jnp.int32)[None, :]),
        bseln=rows(qpos[:, None] - new_pos[None, :]),
        cov=jnp.asarray(cov, BF16), expand=jnp.asarray(expand, BF16))


def _layer_weights(l, norm_mix, w_in, conv_w, q_norm, k_norm, cmp_pe, cmp_w1, cmp_w2, out_norm_conv,
                   out_norm_att, w_out, norm_ffn, w_gate, w_up, w_down, d_conv, d_att):
    perm = np.arange(d_att).reshape(N_KV, GROUP, HEAD_DIM).transpose(1, 0, 2).reshape(-1)
    q0 = 3 * d_conv
    w = w_in[l]
    n_gates = w.shape[1] - (q0 + d_att + 6 * D_KV)
    w = jnp.concatenate([w[:, :q0], w[:, q0:q0 + d_att][:, perm], w[:, q0 + d_att:],
                         jnp.zeros((w.shape[0], LANES - n_gates), w.dtype)], axis=1).astype(BF16)
    wo = w_out[l]
    wo = jnp.concatenate([wo[:d_conv], wo[d_conv:][perm]], axis=0).astype(BF16)
    ck = 2 * LANES
    n_ck = w_gate.shape[2] // ck
    pair = lambda v: jnp.tile(v, 2)[None, :]

    def cmp_weights(i, gain):
        w1 = cmp_w1[l, i]
        z = jnp.zeros_like(w1[0])
        halves = []
        for h in range(2):
            rows_ = [jnp.concatenate([jnp.concatenate([w1[h * CMP_STRIDE + r], z], axis=1),
                                      jnp.concatenate([z, w1[h * CMP_STRIDE + r]], axis=1)], axis=0)
                     for r in range(CMP_STRIDE)]
            halves.append(jnp.concatenate(rows_, axis=0))
        w2 = cmp_w2[l, i]
        z2 = jnp.zeros_like(w2)
        return dict(
            wab=jnp.concatenate(halves, axis=1).astype(BF16),
            pe=jnp.broadcast_to(cmp_pe[l, i].reshape(1, -1), (8, CMP_LEN * HEAD_DIM)),
            w1f=w1.reshape(CMP_LEN * HEAD_DIM, CMP_HID),
            w2bd=jnp.concatenate([jnp.concatenate([w2, z2], axis=1),
                                  jnp.concatenate([z2, w2], axis=1)], axis=0).astype(BF16),
            gain=gain)

    return dict(
        norm_mix=norm_mix[l][None, :], w_in=w, q_gain=pair(q_norm[l]),
        k_gain=jnp.stack([jnp.tile(k_norm[l, 1], 2), jnp.tile(k_norm[l, 2], 2)]),
        cmp_k=cmp_weights(0, pair(k_norm[l, 0])), cmp_v=cmp_weights(1, jnp.ones((1, LANES), F32)),
        conv_w=jnp.pad(conv_w[l], ((0, 8 - CONV_WIDTH), (0, 0))),
        norm_conv=out_norm_conv[l][None, :], norm_att=out_norm_att[l][perm][None, :], w_out=wo,
        norm_ffn=norm_ffn[l][None, :],
        w_gate=w_gate[l].reshape(-1, n_ck, ck).transpose(1, 0, 2).astype(BF16),
        w_up=w_up[l].reshape(-1, n_ck, ck).transpose(1, 0, 2).astype(BF16),
        w_down=w_down[l].reshape(n_ck, ck, -1).astype(BF16))


def _shifted(u, hist):
    up = jnp.concatenate([hist, u], axis=1)
    t = u.shape[1]
    return up[:, 1:1 + t], up[:, 0:t]


def kernel(x_prompt, x_sample, state_conv, cache_cmp_k, cache_cmp_v, cache_slc_k, cache_slc_v,
           cache_win_k, cache_win_v, page_table, rel_bias, norm_mix, w_in, conv_w, q_norm, k_norm,
           cmp_pe, cmp_w1, cmp_w2, out_norm_conv, out_norm_att, w_out, norm_ffn, w_gate, w_up, w_down):
    nb, t, dm = x_prompt.shape
    ns, t_new, _ = x_sample.shape
    depth = w_in.shape[0]
    d_conv = conv_w.shape[2]
    d_att = dm - d_conv
    n_pool = cache_cmp_k.shape[1]
    n_pages = page_table.shape[1]
    pos0 = n_pages * PAGE_SIZE
    n_win = cache_win_k.shape[2]
    assert t % TQ == 0 and d_att == N_HEADS * HEAD_DIM and n_pages % PAGES_PER_STEP == 0
    assert n_win == WINDOW and pos0 >= WINDOW and t_new <= LANES and t_new & (t_new - 1) == 0
    assert t // CMP_STRIDE == LANES and SEL_BLOCK == 64

    tm_p = 512
    kv = lambda a, b_, t_: a.reshape(b_, t_, N_KV, HEAD_DIM)
    ptab = _prompt_tables(rel_bias, t)
    stab = _sample_tables(rel_bias, t_new, pos0, n_win, LANES)
    pool = lambda c: c.reshape(depth * n_pool, PAGE_SIZE, LANES)
    pools = [pool(c) for c in (cache_cmp_k, cache_cmp_v, cache_slc_k, cache_slc_v)]
    pad_new = lambda a: jnp.pad(a.reshape(ns, t_new, LANES), ((0, 0), (0, LANES - t_new), (0, 0)))
    eye = jnp.eye(N_KV, dtype=BF16)[None, None, :, None, :, None]

    xp = x_prompt.reshape(nb * t, dm)
    xs = x_sample.reshape(ns * t_new, dm)
    st_p, st_s = [], []
    for l in range(depth):
        lw = _layer_weights(l, norm_mix, w_in, conv_w, q_norm, k_norm, cmp_pe, cmp_w1, cmp_w2, out_norm_conv,
                            out_norm_att, w_out, norm_ffn, w_gate, w_up, w_down, d_conv, d_att)
        proj = lambda x, tm: _in_proj(x, lw["norm_mix"], lw["w_in"], lw["q_gain"], lw["k_gain"], d_conv, d_att, tm)

        gb, u, q, kc, vc, ks, vs, kw, vw, gates = proj(xp, tm_p)
        seq = lambda a: a.reshape(nb, t, LANES)
        kcmp = _compress_rows(seq(kc), lw["cmp_k"], True)
        vcmp = _compress_rows(seq(vc), lw["cmp_v"], False)
        o_att = _pattn(q.reshape(nb, t, d_att), gates, kcmp, vcmp, seq(ks), seq(vs), seq(kw), seq(vw), ptab)
        u3 = u.reshape(nb, t, d_conv)
        u1, u2 = _shifted(u3, jnp.zeros((nb, CONV_WIDTH - 1, d_conv), F32))
        xp = _mix_ffn(xp, gb, u, u1.reshape(nb * t, d_conv), u2.reshape(nb * t, d_conv),
                      o_att.reshape(nb * t, d_att), lw, tm_p)
        n_keep = min(WINDOW, t)
        st_p.append((u3[:, -(CONV_WIDTH - 1):], kv(kc, nb, t), kv(vc, nb, t), kv(ks, nb, t), kv(vs, nb, t),
                     kv(kw, nb, t)[:, -n_keep:], kv(vw, nb, t)[:, -n_keep:]))

        gb, u, q, kc, vc, ks, vs, kw, vw, gates = proj(xs, ns * t_new)
        base = l * n_pool
        kcmp = _compress_paged(pools[0], page_table, base, lw["cmp_k"], True)
        vcmp = _compress_paged(pools[1], page_table, base, lw["cmp_v"], False)
        qr = q.reshape(ns, t_new, GROUP, N_KV, HEAD_DIM).transpose(0, 2, 3, 1, 4)
        qm = (qr[:, :, :, :, None, :] * eye).reshape(ns, N_HEADS * t_new, LANES)
        gr = gates[:, :3 * N_HEADS].reshape(ns, t_new, 3, N_KV, GROUP).transpose(0, 2, 4, 3, 1)
        gr = jnp.broadcast_to(gr.reshape(ns, 3, N_HEADS * t_new, 1), (ns, 3, N_HEADS * t_new, LANES))
        kwc = cache_win_k[l].reshape(ns, n_win, LANES)
        vwc = cache_win_v[l].reshape(ns, n_win, LANES)
        o = _sattn(qm, kcmp, vcmp, kwc, vwc, pad_new(kw), pad_new(vw), pad_new(ks), pad_new(vs),
                   pools[2], pools[3], page_table, base, gr, stab, t_new, pos0)
        o6 = o.reshape(ns, GROUP, N_KV, t_new, N_KV, HEAD_DIM)
        o_att = jnp.stack([o6[:, :, g, :, g] for g in range(N_KV)], axis=2)
        o_att = o_att.transpose(0, 3, 1, 2, 4).reshape(ns * t_new, d_att)
        u3 = u.reshape(ns, t_new, d_conv)
        u1, u2 = _shifted(u3, state_conv[l])
        xs = _mix_ffn(xs, gb, u, u1.reshape(-1, d_conv), u2.reshape(-1, d_conv), o_att, lw, ns * t_new)
        new_conv = jnp.concatenate([state_conv[l], u3], axis=1)[:, -(CONV_WIDTH - 1):]
        kw_all = jnp.concatenate([cache_win_k[l], kv(kw, ns, t_new)], axis=1)
        vw_all = jnp.concatenate([cache_win_v[l], kv(vw, ns, t_new)], axis=1)
        n_keep = min(WINDOW, kw_all.shape[1])
        st_s.append((new_conv, kv(kc, ns, t_new), kv(vc, ns, t_new), kv(ks, ns, t_new), kv(vs, ns, t_new),
                     kw_all[:, -n_keep:], vw_all[:, -n_keep:]))

    outs_p = [jnp.stack(a) for a in zip(*st_p)]
    outs_s = [jnp.stack(a) for a in zip(*st_s)]
    return (xp.reshape(nb, t, dm), xs.reshape(ns, t_new, dm), *outs_p, *outs_s)
```

```python
import functools
import math

import jax
import jax.numpy as jnp
import numpy as np
from jax import lax
from jax.experimental import pallas as pl
from jax.experimental.pallas import tpu as pltpu

F32 = jnp.float32
BF16 = jnp.bfloat16

HEAD_DIM = 64
N_KV = 2
GROUP = 4
N_HEADS = N_KV * GROUP
D_KV = N_KV * HEAD_DIM
CMP_LEN = 32
CMP_STRIDE = 16
CMP_HID = 2 * HEAD_DIM
SEL_BLOCK = 64
SEL_TOPK = 16
WINDOW = 512
NUM_BUCKETS = 32
MAX_DISTANCE = 128
PAGE_SIZE = 128
CONV_WIDTH = 3
EPS = 1e-6
SCALE = HEAD_DIM ** -0.5
LOG2E = math.log2(math.e)
NEG = -1e30

LANES = 128
TQ = 128
GATE_ROWS = 32
PAGES_PER_STEP = 16
VMEM_LIMIT = 56 * 1024 * 1024

_NT = (((1,), (1,)), ((), ()))


def _dot(a, b):
    return jnp.dot(a, b, preferred_element_type=F32)


def _dot_nt(a, b):
    return lax.dot_general(a, b, _NT, preferred_element_type=F32)


def _params(n_axes):
    return pltpu.CompilerParams(dimension_semantics=("arbitrary",) * n_axes,
                                vmem_limit_bytes=VMEM_LIMIT)


def _resident(shape):
    nd = len(shape)
    return pl.BlockSpec(shape, lambda *_: (0,) * nd, pipeline_mode=pl.Buffered(1))


def _rms(x, gain):
    return x * lax.rsqrt(jnp.mean(x * x, axis=-1, keepdims=True) + EPS) * gain


def _pair_rms(y, gain, lo):
    y2 = y * y
    s_lo = jnp.sum(jnp.where(lo, y2, 0.0), axis=-1, keepdims=True)
    s_hi = jnp.sum(jnp.where(lo, 0.0, y2), axis=-1, keepdims=True)
    inv = jnp.where(lo, lax.rsqrt(s_lo / HEAD_DIM + EPS), lax.rsqrt(s_hi / HEAD_DIM + EPS))
    return y * inv * gain


def _in_proj_kernel(x_ref, nm_ref, w_ref, qg_ref, kg_ref,
                    gb_ref, u_ref, q_ref, kc_ref, vc_ref, ks_ref, vs_ref, kw_ref, vw_ref, gt_ref):
    x = x_ref[...]
    hb = _rms(x, nm_ref[...]).astype(BF16)
    d = gb_ref.shape[1]

    def seg(a, b):
        return _dot(hb, w_ref[:, a:b])

    gb_ref[...] = seg(0, d)
    u_ref[...] = seg(d, 2 * d) * seg(2 * d, 3 * d)
    lo = lax.broadcasted_iota(jnp.int32, (x.shape[0], LANES), 1) < HEAD_DIM
    c = 3 * d
    for j in range(q_ref.shape[1] // LANES):
        q_ref[:, j * LANES:(j + 1) * LANES] = _pair_rms(
            seg(c + j * LANES, c + (j + 1) * LANES), qg_ref[...], lo).astype(BF16)
    c += q_ref.shape[1]
    kc_ref[...] = seg(c, c + LANES)
    vc_ref[...] = seg(c + LANES, c + 2 * LANES)
    ks_ref[...] = _pair_rms(seg(c + 2 * LANES, c + 3 * LANES), kg_ref[0:1, :], lo)
    vs_ref[...] = seg(c + 3 * LANES, c + 4 * LANES)
    kw_ref[...] = _pair_rms(seg(c + 4 * LANES, c + 5 * LANES), kg_ref[1:2, :], lo)
    vw_ref[...] = seg(c + 5 * LANES, c + 6 * LANES)
    gt_ref[...] = jax.nn.sigmoid(seg(c + 6 * LANES, c + 7 * LANES))


def _in_proj(x, nm, w, qg, kg, d_conv, d_att, tm):
    n, dm = x.shape
    row = lambda w_: pl.BlockSpec((tm, w_), lambda i: (i, 0))
    widths = [d_conv, d_conv, d_att] + [LANES] * 7
    dtypes = [F32, F32, BF16] + [F32] * 7
    return pl.pallas_call(
        _in_proj_kernel,
        grid=(n // tm,),
        in_specs=[row(dm), _resident(nm.shape), _resident(w.shape), _resident(qg.shape), _resident(kg.shape)],
        out_specs=[row(w_) for w_ in widths],
        out_shape=[jax.ShapeDtypeStruct((n, w_), dt) for w_, dt in zip(widths, dtypes)],
        compiler_params=_params(1),
        name="in_proj",
    )(x, nm, w, qg, kg)


def _in_proj_t_kernel(x_ref, nm_ref, w_ref, wt_ref, qgt_ref, kg_ref,
                      gb_ref, u_ref, kc_ref, vc_ref, ks_ref, kw_ref, qt_ref, vst_ref, vwt_ref, gtt_ref):
    x = x_ref[...]
    hb = _rms(x, nm_ref[...]).astype(BF16)
    d = gb_ref.shape[1]

    def seg(a, b):
        return _dot(hb, w_ref[:, a:b])

    gb_ref[...] = seg(0, d)
    u_ref[...] = seg(d, 2 * d) * seg(2 * d, 3 * d)
    lo = lax.broadcasted_iota(jnp.int32, (x.shape[0], LANES), 1) < HEAD_DIM
    c = 3 * d
    kc_ref[...] = seg(c, c + LANES)
    vc_ref[...] = seg(c + LANES, c + 2 * LANES)
    ks_ref[...] = _pair_rms(seg(c + 2 * LANES, c + 3 * LANES), kg_ref[0:1, :], lo)
    kw_ref[...] = _pair_rms(seg(c + 3 * LANES, c + 4 * LANES), kg_ref[1:2, :], lo)

    def seg_t(a, b):
        return _dot_nt(wt_ref[a:b, :], hb)

    d_att = qt_ref.shape[1]
    for h in range(d_att // HEAD_DIM):
        y = seg_t(h * HEAD_DIM, (h + 1) * HEAD_DIM)
        inv = lax.rsqrt(jnp.mean(y * y, axis=0, keepdims=True) + EPS)
        qt_ref[0, h * HEAD_DIM:(h + 1) * HEAD_DIM, :] = (y * inv * qgt_ref[...]).astype(BF16)
    vst_ref[0] = seg_t(d_att, d_att + LANES)
    vwt_ref[0] = seg_t(d_att + LANES, d_att + 2 * LANES)
    gtt_ref[0] = jax.nn.sigmoid(seg_t(d_att + 2 * LANES, d_att + 2 * LANES + GATE_ROWS))


def _in_proj_t(x, nm, w, wt, qgt, kg, nb, d_conv, d_att, tm):
    n, dm = x.shape
    t = n // nb
    per_b = t // tm
    row = lambda w_: pl.BlockSpec((tm, w_), lambda i: (i, 0))
    col = lambda r: pl.BlockSpec((1, r, tm), lambda i: (i // per_b, 0, i % per_b))
    widths = [d_conv, d_conv] + [LANES] * 4
    rows_t = [d_att, LANES, LANES, GATE_ROWS]
    dt_t = [BF16, F32, F32, F32]
    return pl.pallas_call(
        _in_proj_t_kernel,
        grid=(n // tm,),
        in_specs=[row(dm), _resident(nm.shape), _resident(w.shape), _resident(wt.shape),
                  _resident(qgt.shape), _resident(kg.shape)],
        out_specs=[row(w_) for w_ in widths] + [col(r) for r in rows_t],
        out_shape=[jax.ShapeDtypeStruct((n, w_), F32) for w_ in widths]
        + [jax.ShapeDtypeStruct((nb, r, t), dt) for r, dt in zip(rows_t, dt_t)],
        compiler_params=_params(1),
        name="in_proj_t",
    )(x, nm, w, wt, qgt, kg)


def _cmp_ab_kernel(*refs, n_in, n_prefetch=0):
    refs = refs[n_prefetch:]
    x_refs = refs[:n_in]
    w_ref, a_ref, b_ref = refs[n_in:]
    n_chunks = x_refs[0].shape[1] // CMP_STRIDE
    cols = []
    for l in range(CMP_STRIDE):
        parts = [xr[0, pl.ds(l, n_chunks, stride=CMP_STRIDE), :] for xr in x_refs]
        cols.append(parts[0] if n_in == 1 else jnp.concatenate(parts, axis=0))
    xr = jnp.concatenate(cols, axis=1).astype(BF16)
    ab = _dot(xr, w_ref[...])
    half = ab.shape[1] // 2
    a_ref[0] = ab[:, :half]
    b_ref[0] = ab[:, half:]


def _cmp_fin_kernel(a_ref, b_ref, pe_ref, w1_ref, w2_ref, g_ref, o_ref, *, n_valid, normalize, transpose_out):
    a = a_ref[0]
    bm = b_ref[0]
    n_chunks = a.shape[0]
    b_next = pltpu.roll(bm, n_chunks - 1, 0)
    cpe = jnp.dot(pe_ref[...], w1_ref[...], preferred_element_type=F32,
                  precision=lax.Precision.HIGHEST)[0:1]
    x = a + b_next + jnp.concatenate([cpe, cpe], axis=1)
    hid = x * (0.5 * (1.0 + jnp.tanh(math.sqrt(2.0 / math.pi) * (x + 0.044715 * (x * x * x)))))
    out = _dot(hid.astype(BF16), w2_ref[...])
    if normalize:
        lo = lax.broadcasted_iota(jnp.int32, out.shape, 1) < HEAD_DIM
        out = _pair_rms(out, g_ref[...], lo)
    rows = lax.broadcasted_iota(jnp.int32, out.shape, 0)
    out = jnp.where(rows < n_valid, out, 0.0)
    o_ref[0] = out.T if transpose_out else out


def _cmp_fin(a, b, cw, n_valid, normalize, transpose_out=False):
    nb, n_chunks, wid = a.shape
    blk = pl.BlockSpec((1, n_chunks, wid), lambda i: (i, 0, 0))
    oshape = (LANES, n_chunks) if transpose_out else (n_chunks, LANES)
    return pl.pallas_call(
        functools.partial(_cmp_fin_kernel, n_valid=n_valid, normalize=normalize, transpose_out=transpose_out),
        grid=(nb,),
        in_specs=[blk, blk, _resident(cw["pe"].shape), _resident(cw["w1f"].shape),
                  _resident(cw["w2bd"].shape), _resident(cw["gain"].shape)],
        out_specs=pl.BlockSpec((1,) + oshape, lambda i: (i, 0, 0)),
        out_shape=jax.ShapeDtypeStruct((nb,) + oshape, F32),
        compiler_params=_params(1),
        name="cmp_fin",
    )(a, b, cw["pe"], cw["w1f"], cw["w2bd"], cw["gain"])


def _compress_rows(rows, cw, normalize, transpose_out):
    nb, t, _ = rows.shape
    n_chunks = t // CMP_STRIDE
    wid = cw["wab"].shape[1] // 2
    out_blk = pl.BlockSpec((1, n_chunks, wid), lambda i: (i, 0, 0))
    a, b = pl.pallas_call(
        functools.partial(_cmp_ab_kernel, n_in=1),
        grid=(nb,),
        in_specs=[pl.BlockSpec((1, t, LANES), lambda i: (i, 0, 0)), _resident(cw["wab"].shape)],
        out_specs=[out_blk, out_blk],
        out_shape=[jax.ShapeDtypeStruct((nb, n_chunks, wid), F32)] * 2,
        compiler_params=_params(1),
        name="cmp_ab_rows",
    )(rows, cw["wab"])
    return _cmp_fin(a, b, cw, (t - CMP_LEN) // CMP_STRIDE + 1, normalize, transpose_out)


def _page_spec(i, base, pages_per_step):
    return pl.BlockSpec((1, PAGE_SIZE, LANES),
                        lambda b, s, pt: (base + pt[b, s * pages_per_step + i], 0, 0))


def _compress_paged(cache, page_table, base, cw, normalize):
    nb, n_pages = page_table.shape
    pps = PAGES_PER_STEP
    ch = pps * PAGE_SIZE // CMP_STRIDE
    wid = cw["wab"].shape[1] // 2
    n_chunks = n_pages * PAGE_SIZE // CMP_STRIDE
    out_blk = pl.BlockSpec((1, ch, wid), lambda b, s, pt: (b, s, 0))
    a, b = pl.pallas_call(
        functools.partial(_cmp_ab_kernel, n_in=pps, n_prefetch=1),
        grid_spec=pltpu.PrefetchScalarGridSpec(
            num_scalar_prefetch=1,
            grid=(nb, n_pages // pps),
            in_specs=[_page_spec(i, base, pps) for i in range(pps)]
            + [pl.BlockSpec(cw["wab"].shape, lambda b, s, pt: (0, 0), pipeline_mode=pl.Buffered(1))],
            out_specs=[out_blk, out_blk]),
        out_shape=[jax.ShapeDtypeStruct((nb, n_chunks, wid), F32)] * 2,
        compiler_params=_params(2),
        name="cmp_ab_paged",
    )(page_table, *([cache] * pps), cw["wab"])
    return _cmp_fin(a, b, cw, n_chunks - 1, normalize)


def _split_bf16(x):
    hi = x.astype(BF16)
    return hi, (x - hi.astype(F32)).astype(BF16)


def _online_update(state, s, mask, v_t):
    m, l, acc = state
    s = jnp.where(mask, s, NEG)
    m_new = jnp.maximum(m, jnp.max(s, axis=-1, keepdims=True))
    alpha = jnp.exp2(m - m_new)
    e = jnp.where(mask, jnp.exp2(s - m_new), 0.0)
    l = alpha * l + jnp.sum(e, axis=-1, keepdims=True)
    acc = alpha * acc + _dot_nt(e.astype(BF16), v_t)
    return m_new, l, acc


def _finish(state):
    _, l, acc = state
    return acc / jnp.where(l > 0, l, 1.0)


def _reset_t(m_ref, acc_ref):
    m_ref[...] = jnp.full(m_ref.shape, NEG, F32)
    acc_ref[...] = jnp.zeros(acc_ref.shape, F32)


def _update_t(m_ref, acc_ref, s_t, v_t):
    m = m_ref[...]
    m_new = jnp.maximum(m, jnp.max(s_t, axis=0, keepdims=True))
    alpha = jnp.exp2(m - m_new)
    e = jnp.exp2(s_t - m_new).astype(BF16)
    acc_ref[...] = alpha * acc_ref[...] + _dot(v_t, e)
    m_ref[...] = m_new


def _pattn_kernel(qt_ref, gt_ref, kcmp_ref, vcmpt_ref, ksa_ref, vst_ref, kwa_ref, vwt_ref,
                  bcmp_ref, near_ref, covt_ref, o_ref, m_ref, acc_ref, qa_ref, out_ref, *, n_sel, topk):
    qi = pl.program_id(1)
    q0 = qi * TQ
    n_slabs = qt_ref.shape[1] // LANES
    wide = n_slabs * TQ
    sub = lax.broadcasted_iota(jnp.int32, (LANES, TQ), 0)
    lane = lax.broadcasted_iota(jnp.int32, (LANES, TQ), 1)
    sub_w = lax.broadcasted_iota(jnp.int32, (LANES, wide), 0)
    tpos_w = q0 + (lax.broadcasted_iota(jnp.int32, (LANES, wide), 1) & (TQ - 1))
    kcmp = kcmp_ref[0].astype(BF16)
    vcmp_t = vcmpt_ref[0].astype(BF16)
    covt = covt_ref[...]
    gt = gt_ref[0]
    zero_rows = jnp.zeros((LANES - n_sel, TQ), F32)
    def gate(br, g):
        return jnp.concatenate([gt[br * N_HEADS + g * GROUP + j:br * N_HEADS + g * GROUP + j + 1, :]
                                for j in range(n_slabs)], axis=1)

    for g in range(N_KV):
        mine = (sub < HEAD_DIM) if g == 0 else (sub >= HEAD_DIM)
        q_t = jnp.concatenate(
            [jnp.where(mine, qt_ref[0, j * LANES:(j + 1) * LANES, :], jnp.zeros((), BF16))
             for j in range(n_slabs)], axis=1)

        s = _dot(kcmp, q_t) + bcmp_ref[g, 0]
        mask = sub_w * CMP_STRIDE + (CMP_LEN - 1) <= tpos_w
        s = jnp.where(mask, s, NEG)
        e = jnp.where(mask, jnp.exp2(s - jnp.max(s, axis=0, keepdims=True)), 0.0)
        l = jnp.sum(e, axis=0, keepdims=True)
        p = e * (1.0 / jnp.where(l > 0, l, 1.0))
        out_ref[g] = gate(0, g) * _dot(vcmp_t, p.astype(BF16))
        p_sum = p[:, 0:TQ]
        for j in range(1, n_slabs):
            p_sum = p_sum + p[:, j * TQ:(j + 1) * TQ]

        hi, lo_ = _split_bf16(p_sum)
        sc = _dot(covt, hi) + _dot(covt, lo_)
        tq_pos = q0 + lane
        cur = tq_pos >> 6
        forced = (sub == 0) | (sub == cur) | (sub == cur - 1)
        sc = jnp.where(forced, 1e6, jnp.where(sub * SEL_BLOCK > tq_pos, -1e6, sc))
        sc = sc[0:n_sel]
        blk = sub[0:n_sel]
        rank = jnp.zeros((n_sel, TQ), F32)
        for i in range(n_sel):
            si = sc[i:i + 1, :]
            beats = (si > sc) | ((si == sc) & (blk > i))
            rank = rank + jnp.where(beats, 1.0, 0.0)
        pen = jnp.concatenate([jnp.where(rank < topk, 0.0, NEG), zero_rows], axis=0)
        pen = jnp.where(sub == LANES - 1, NEG, pen).astype(BF16)
        qa_ref[g, 0:LANES, :] = q_t
        qa_ref[g, LANES:2 * LANES, :] = jnp.concatenate([pen] * n_slabs, axis=1)

    def tile(k_ref, v_ref, q_aug, kt, g):
        start = pl.multiple_of(kt * TQ, TQ)
        return _dot(k_ref[0, pl.ds(start, TQ), :], q_aug), v_ref[0, g, :, pl.ds(start, TQ)]

    def add_branch(br):
        for g in range(N_KV):
            acc = acc_ref[g]
            den = acc[HEAD_DIM:HEAD_DIM + 1] if g == 0 else acc[0:1]
            out_ref[g] += (gate(br, g) * (1.0 / den)) * acc

    _reset_t(m_ref, acc_ref)

    @pl.loop(1, qi)
    def _(kt):
        for g in range(N_KV):
            s_, v_ = tile(ksa_ref, vst_ref, qa_ref[g], kt, g)
            _update_t(m_ref.at[g], acc_ref.at[g], s_, v_)

    for g in range(N_KV):
        s_, v_ = tile(ksa_ref, vst_ref, qa_ref[g], qi, g)
        _update_t(m_ref.at[g], acc_ref.at[g], s_ + near_ref[1, g], v_)
        s_, v_ = tile(ksa_ref, vst_ref, qa_ref[g], qi + 1, g)
        _update_t(m_ref.at[g], acc_ref.at[g], s_ + near_ref[0, g], v_)
    add_branch(1)

    _reset_t(m_ref, acc_ref)
    pad_pen = jnp.where(sub_w == 0, NEG, 0.0).astype(BF16)
    n_w = WINDOW // TQ
    for g in range(N_KV):
        q_win = jnp.concatenate([qa_ref[g, 0:LANES, :], pad_pen], axis=0)
        for w in range(n_w + 1):
            s_, v_ = tile(kwa_ref, vwt_ref, q_win, qi + w, g)
            if w == 0:
                s_ = s_ + near_ref[2, g]
            elif w == n_w - 1:
                s_ = s_ + near_ref[1, g]
            elif w == n_w:
                s_ = s_ + near_ref[0, g]
            _update_t(m_ref.at[g], acc_ref.at[g], s_, v_)
    add_branch(2)

    out = jnp.where(sub_w < HEAD_DIM, out_ref[0], out_ref[1])
    for j in range(n_slabs):
        o_ref[0, j] = out[:, j * TQ:(j + 1) * TQ]


def _pattn(q_t, gates_t, kcmp, vcmp_t, ks_aug, vs_t, kw_aug, vw_t, tabs):
    nb, d_att, t = q_t.shape
    n_q = t // TQ
    n_sel = -(-t // SEL_BLOCK)
    n_slabs = d_att // LANES
    whole = lambda a: pl.BlockSpec((1,) + a.shape[1:], lambda b, i: (b,) + (0,) * (a.ndim - 1))
    return pl.pallas_call(
        functools.partial(_pattn_kernel, n_sel=n_sel, topk=min(SEL_TOPK, n_sel)),
        grid=(nb, n_q),
        in_specs=[pl.BlockSpec((1, d_att, TQ), lambda b, i: (b, 0, i)),
                  pl.BlockSpec((1, GATE_ROWS, TQ), lambda b, i: (b, 0, i)),
                  whole(kcmp), whole(vcmp_t), whole(ks_aug), whole(vs_t), whole(kw_aug), whole(vw_t),
                  pl.BlockSpec((N_KV, 1, LANES, n_slabs * TQ), lambda b, i: (0, i, 0, 0)),
                  _resident(tabs["near"].shape), _resident(tabs["covt"].shape)],
        out_specs=pl.BlockSpec((1, n_slabs, LANES, TQ), lambda b, i: (b, 0, 0, i)),
        out_shape=jax.ShapeDtypeStruct((nb, n_slabs, LANES, t), F32),
        scratch_shapes=[pltpu.VMEM((N_KV, 1, n_slabs * TQ), F32), pltpu.VMEM((N_KV, LANES, n_slabs * TQ), F32),
                        pltpu.VMEM((N_KV, 2 * LANES, n_slabs * TQ), BF16), pltpu.VMEM((N_KV, LANES, n_slabs * TQ), F32)],
        compiler_params=_params(2),
        name="pattn",
    )(q_t, gates_t, kcmp, vcmp_t, ks_aug, vs_t, kw_aug, vw_t, tabs["bcmp"], tabs["near"], tabs["covt"])


def _sattn1_kernel(qm_ref, kcmp_ref, vcmp_ref, kwc_ref, vwc_ref, kwn_ref, vwn_ref,
                   bcmp_ref, bwc_ref, bwn_ref, cov_ref, ocmp_ref, owin_ref, sel_ref,
                   *, n_cmp, n_sel, topk, t_new, pos0, win_start):
    qm = qm_ref[0]
    n_rows = qm.shape[0]
    n_gt = N_KV * t_new
    s = _dot_nt(qm, kcmp_ref[0].astype(BF16)) + bcmp_ref[...]
    ncol = lax.broadcasted_iota(jnp.int32, s.shape, 1)
    trow = pos0 + (lax.broadcasted_iota(jnp.int32, s.shape, 0) & (t_new - 1))
    mask = (ncol < n_cmp) & (ncol * CMP_STRIDE + (CMP_LEN - 1) <= trow)
    s = jnp.where(mask, s, NEG)
    e = jnp.where(mask, jnp.exp2(s - jnp.max(s, axis=-1, keepdims=True)), 0.0)
    l = jnp.sum(e, axis=-1, keepdims=True)
    p = e / jnp.where(l > 0, l, 1.0)
    ocmp_ref[0] = _dot(p.astype(BF16), vcmp_ref[0].astype(BF16))
    p_sum = p[0:n_gt]
    for j in range(1, n_rows // n_gt):
        p_sum = p_sum + p[j * n_gt:(j + 1) * n_gt]
    hi, lo = _split_bf16(p_sum)
    sc = _dot(hi, cov_ref[...]) + _dot(lo, cov_ref[...])
    blk = lax.broadcasted_iota(jnp.int32, sc.shape, 1)
    qpos = pos0 + (lax.broadcasted_iota(jnp.int32, sc.shape, 0) & (t_new - 1))
    cur = qpos >> 6
    forced = (blk == 0) | (blk == cur) | (blk == cur - 1)
    sc = jnp.where(forced, 1e6, jnp.where(blk * SEL_BLOCK > qpos, -1e6, sc))
    sc = jnp.where(blk < n_sel, sc, -3e38)
    rank = jnp.zeros(sc.shape, F32)
    for i in range(n_sel):
        si = sc[:, i:i + 1]
        beats = (si > sc) | ((si == sc) & (blk > i))
        rank = rank + jnp.where(beats, 1.0, 0.0)
    sel_ref[0] = jnp.where((rank < topk) & (blk < n_sel), 1.0, 0.0)

    init = (jnp.full((n_rows, 1), NEG, F32), jnp.zeros((n_rows, 1), F32), jnp.zeros((n_rows, LANES), F32))
    s = _dot(qm, kwc_ref[0].astype(BF16)) + bwc_ref[...]
    trow = pos0 + (lax.broadcasted_iota(jnp.int32, s.shape, 0) & (t_new - 1))
    dist = trow - (win_start + lax.broadcasted_iota(jnp.int32, s.shape, 1))
    st = _online_update(init, s, (dist >= 0) & (dist < WINDOW), vwc_ref[0].astype(BF16))
    s = _dot(qm, kwn_ref[0].astype(BF16)) + bwn_ref[...]
    trow = pos0 + (lax.broadcasted_iota(jnp.int32, s.shape, 0) & (t_new - 1))
    col = lax.broadcasted_iota(jnp.int32, s.shape, 1)
    dist = trow - (pos0 + col)
    st = _online_update(st, s, (dist >= 0) & (dist < WINDOW) & (col < t_new), vwn_ref[0].astype(BF16))
    owin_ref[0] = _finish(st)


def _sattn2_kernel(pt_ref, *refs, pps, t_new):
    k_refs = refs[:pps]
    v_refs = refs[pps:2 * pps]
    (qm_ref, selc_ref, seln_ref, ksn_ref, vsn_ref, bsel_ref, bseln_ref, exp_ref,
     ocmp_ref, owin_ref, gate_ref, o_ref, m_sc, l_sc, acc_sc) = refs[2 * pps:]
    step = pl.program_id(1)
    qm = qm_ref[0]
    n_rows = qm.shape[0]
    reps = n_rows // selc_ref.shape[2]

    @pl.when(step == 0)
    def _():
        s = _dot(qm, ksn_ref[0].astype(BF16)) + bseln_ref[...]
        trow = lax.broadcasted_iota(jnp.int32, s.shape, 0) & (t_new - 1)
        col = lax.broadcasted_iota(jnp.int32, s.shape, 1)
        hit = jnp.concatenate([seln_ref[0]] * reps, axis=0) > 0.5
        init = (jnp.full((n_rows, 1), NEG, F32), jnp.zeros((n_rows, 1), F32),
                jnp.zeros((n_rows, LANES), F32))
        m, l, acc = _online_update(init, s, hit & (col <= trow) & (col < t_new), vsn_ref[0].astype(BF16))
        m_sc[...] = m
        l_sc[...] = l
        acc_sc[...] = acc

    k_t = jnp.concatenate([r[0] for r in k_refs], axis=1).astype(BF16)
    v_t = jnp.concatenate([r[0] for r in v_refs], axis=1).astype(BF16)
    s = _dot(qm, k_t) + bsel_ref[...]
    hit = _dot(selc_ref[0, 0].astype(BF16), exp_ref[...])
    mask = jnp.concatenate([hit] * reps, axis=0) > 0.5
    m, l, acc = _online_update((m_sc[...], l_sc[...], acc_sc[...]), s, mask, v_t)
    m_sc[...] = m
    l_sc[...] = l
    acc_sc[...] = acc

    @pl.when(step == pl.num_programs(1) - 1)
    def _():
        o_sel = _finish((m, l, acc))
        o_ref[0] = gate_ref[0, 0] * ocmp_ref[0] + gate_ref[0, 1] * o_sel + gate_ref[0, 2] * owin_ref[0]


def _sattn(qm, kcmp, vcmp, kwc_t, vwc_t, kwn_t, vwn_t, ksn_t, vsn_t, cache_k, cache_v, page_table, base,
           gates, tabs, t_new, pos0):
    nb, n_rows, _ = qm.shape
    n_pages = page_table.shape[1]
    n_chunks = kcmp.shape[1]
    n_gt = N_KV * t_new
    n_sel = -(-(pos0 + t_new) // SEL_BLOCK)
    selw = tabs["cov"].shape[1]
    per_b = lambda shape: pl.BlockSpec((1,) + shape, lambda b: (b,) + (0,) * len(shape))
    ocmp, owin, sel = pl.pallas_call(
        functools.partial(_sattn1_kernel, n_cmp=n_chunks - 1, n_sel=n_sel, topk=min(SEL_TOPK, n_sel),
                          t_new=t_new, pos0=pos0, win_start=pos0 - kwc_t.shape[2]),
        grid=(nb,),
        in_specs=[per_b((n_rows, LANES)), per_b((n_chunks, LANES)), per_b((n_chunks, LANES)),
                  per_b(kwc_t.shape[1:]), per_b(vwc_t.shape[1:]), per_b(kwn_t.shape[1:]), per_b(vwn_t.shape[1:]),
                  _resident(tabs["bcmp"].shape), _resident(tabs["bwc"].shape), _resident(tabs["bwn"].shape),
                  _resident(tabs["cov"].shape)],
        out_specs=[per_b((n_rows, LANES)), per_b((n_rows, LANES)), per_b((n_gt, selw))],
        out_shape=[jax.ShapeDtypeStruct((nb, n_rows, LANES), F32), jax.ShapeDtypeStruct((nb, n_rows, LANES), F32),
                   jax.ShapeDtypeStruct((nb, n_gt, selw), F32)],
        compiler_params=_params(1),
        name="sattn1",
    )(qm, kcmp, vcmp, kwc_t, vwc_t, kwn_t, vwn_t, tabs["bcmp"], tabs["bwc"], tabs["bwn"], tabs["cov"])

    pps = PAGES_PER_STEP
    n_steps = n_pages // pps
    bps = pps * PAGE_SIZE // SEL_BLOCK
    n_cblk = n_pages * PAGE_SIZE // SEL_BLOCK
    selc = sel[:, :, :n_cblk].reshape(nb, n_gt, n_steps, bps).transpose(0, 2, 1, 3)
    selc = jnp.pad(selc, ((0, 0), (0, 0), (0, 0), (0, LANES - bps)))
    seln = jnp.broadcast_to(sel[:, :, n_cblk:n_cblk + 1], (nb, n_gt, LANES))

    keys = pps * PAGE_SIZE
    cst = lambda shape: pl.BlockSpec(shape, lambda b, s, pt: (0,) * len(shape), pipeline_mode=pl.Buffered(1))
    pb = lambda shape: pl.BlockSpec((1,) + shape, lambda b, s, pt: (b,) + (0,) * len(shape))
    return pl.pallas_call(
        functools.partial(_sattn2_kernel, pps=pps, t_new=t_new),
        grid_spec=pltpu.PrefetchScalarGridSpec(
            num_scalar_prefetch=1,
            grid=(nb, n_steps),
            in_specs=[_page_spec(i, base, pps) for i in range(pps)] * 2
            + [pb((n_rows, LANES)),
               pl.BlockSpec((1, 1, n_gt, LANES), lambda b, s, pt: (b, s, 0, 0)),
               pb((n_gt, LANES)), pb(ksn_t.shape[1:]), pb(vsn_t.shape[1:]),
               pl.BlockSpec((n_rows, keys), lambda b, s, pt: (0, s)),
               cst(tabs["bseln"].shape), cst(tabs["expand"].shape),
               pb((n_rows, LANES)), pb((n_rows, LANES)), pb((3, n_rows, LANES))],
            out_specs=pb((n_rows, LANES)),
            scratch_shapes=[pltpu.VMEM((n_rows, 1), F32), pltpu.VMEM((n_rows, 1), F32),
                            pltpu.VMEM((n_rows, LANES), F32)]),
        out_shape=jax.ShapeDtypeStruct((nb, n_rows, LANES), F32),
        compiler_params=_params(2),
        name="sattn2",
    )(page_table, *([cache_k] * pps), *([cache_v] * pps), qm, selc, seln, ksn_t, vsn_t,
      tabs["bsel"], tabs["bseln"], tabs["expand"], ocmp, owin, gates)


def _mix_ffn_kernel(x_ref, gb_ref, u_ref, u1_ref, u2_ref, oa_ref, cw_ref, nc_ref, na_ref, wo_ref,
                    nf_ref, wg_ref, wu_ref, wd_ref, y_ref, acc_ref):
    d = gb_ref.shape[1]
    conv = u2_ref[...] * cw_ref[0:1, :] + u1_ref[...] * cw_ref[1:2, :] + u_ref[...] * cw_ref[2:3, :]
    yc = _rms(gb_ref[...] * conv, nc_ref[...]).astype(BF16)
    oa = _rms(oa_ref[...], na_ref[...]).astype(BF16)
    x1 = x_ref[...] + (_dot(yc, wo_ref[0:d, :]) + _dot(oa, wo_ref[d:, :]))
    h2 = _rms(x1, nf_ref[...]).astype(BF16)
    acc_ref[...] = jnp.zeros_like(acc_ref)
    for c in range(wg_ref.shape[0]):
        gate = _dot(h2, wg_ref[c])
        act = (gate * jax.nn.sigmoid(gate)) * _dot(h2, wu_ref[c])
        acc_ref[...] += _dot(act.astype(BF16), wd_ref[c])
    y_ref[...] = x1 + acc_ref[...]


def _mix_ffn(x, gb, u, u1, u2, oa, lw, tm):
    n, dm = x.shape
    d = gb.shape[1]
    row = lambda w_: pl.BlockSpec((tm, w_), lambda i: (i, 0))
    consts = [lw["conv_w"], lw["norm_conv"], lw["norm_att"], lw["w_out"], lw["norm_ffn"],
              lw["w_gate"], lw["w_up"], lw["w_down"]]
    return pl.pallas_call(
        _mix_ffn_kernel,
        grid=(n // tm,),
        in_specs=[row(dm)] + [row(d)] * 5 + [_resident(c.shape) for c in consts],
        out_specs=row(dm),
        out_shape=jax.ShapeDtypeStruct((n, dm), F32),
        scratch_shapes=[pltpu.VMEM((tm, dm), F32)],
        compiler_params=_params(1),
        name="mix_ffn",
    )(x, gb, u, u1, u2, oa, *consts)


def _t5_bucket(dist):
    n = jnp.maximum(dist, 0)
    max_exact = NUM_BUCKETS // 2
    nf = jnp.maximum(n, 1).astype(F32)
    large = max_exact + (jnp.log(nf / max_exact) / math.log(MAX_DISTANCE / max_exact)
                         * (NUM_BUCKETS - max_exact)).astype(jnp.int32)
    return jnp.where(n < max_exact, n, jnp.minimum(large, NUM_BUCKETS - 1))


def _bias(dist, tbl):
    onehot = jax.nn.one_hot(_t5_bucket(dist), NUM_BUCKETS, dtype=F32)
    return jnp.einsum("...k,kh->h...", onehot, tbl, precision=lax.Precision.HIGHEST)


def _cover(n_cmp, n_sel):
    c0 = np.arange(n_cmp)[:, None] * CMP_STRIDE
    s0 = np.arange(n_sel)[None, :] * SEL_BLOCK
    return np.clip(np.minimum(c0 + CMP_LEN, s0 + SEL_BLOCK) - np.maximum(c0, s0), 0, None) / CMP_STRIDE


def _prompt_tables(tbl, t):
    n_q = t // TQ
    n_chunks = t // CMP_STRIDE
    ar = jnp.arange(TQ, dtype=jnp.int32)
    far = tbl[NUM_BUCKETS - 1]
    assert MAX_DISTANCE <= TQ

    def key_major(b):
        k, q = b.shape[1:]
        return b.reshape(N_KV, GROUP, k, q).transpose(0, 2, 1, 3).reshape(N_KV, k, GROUP * q)

    near = []
    for c in range(2):
        dist = c * TQ + ar[None, :] - ar[:, None]
        b = (_bias(dist, tbl) - far[:, None, None]) * LOG2E
        near.append(key_major(jnp.where(dist[None] >= 0, b, NEG)))
    oldest = jnp.where(ar[:, None] > ar[None, :], 0.0, NEG)
    near.append(key_major(jnp.broadcast_to(oldest[None], (N_HEADS, TQ, TQ))))
    tpos = jnp.arange(t, dtype=jnp.int32)
    blk_end = jnp.arange(n_chunks, dtype=jnp.int32) * CMP_STRIDE + (CMP_LEN - 1)
    b = _bias(tpos[None, :] - blk_end[:, None], tbl) * LOG2E
    b = b.reshape(N_KV, GROUP, n_chunks, n_q, TQ).transpose(0, 3, 2, 1, 4).reshape(N_KV, n_q, n_chunks, GROUP * TQ)
    n_cmp = (t - CMP_LEN) // CMP_STRIDE + 1
    n_sel = -(-t // SEL_BLOCK)
    covt = np.zeros((LANES, n_chunks), np.float32)
    covt[:n_sel, :n_cmp] = _cover(n_cmp, n_sel).T
    keys = np.arange(-TQ, t)
    onehot = (keys[:, None] // SEL_BLOCK == np.arange(LANES)[None, :]) & (keys[:, None] >= 0)
    onehot[:, LANES - 1] = keys < 0
    wkeys = np.arange(-WINDOW, t)
    wflag = np.zeros((WINDOW + t, LANES), np.float32)
    wflag[:, 0] = wkeys < 0
    return dict(near=jnp.stack(near), bcmp=b, covt=jnp.asarray(covt, BF16),
                sel_onehot=jnp.asarray(onehot.astype(np.float32), BF16), win_flag=jnp.asarray(wflag, BF16))


def _sample_tables(tbl, t_new, pos0, n_win, n_pad):
    n_chunks = pos0 // CMP_STRIDE
    n_cmp = n_chunks - 1
    n_sel = -(-(pos0 + t_new) // SEL_BLOCK)
    selw = -(-n_sel // LANES) * LANES
    qpos = pos0 + jnp.arange(t_new, dtype=jnp.int32)

    def rows(dist):
        b = _bias(dist, tbl) * LOG2E
        return b.reshape(N_KV, GROUP, t_new, -1).transpose(1, 0, 2, 3).reshape(N_HEADS * t_new, -1)

    blk_end = jnp.arange(n_chunks, dtype=jnp.int32) * CMP_STRIDE + (CMP_LEN - 1)
    new_pos = pos0 + jnp.arange(n_pad, dtype=jnp.int32)
    cov = np.zeros((n_chunks, selw), np.float32)
    cov[:n_cmp, :n_sel] = _cover(n_cmp, n_sel)
    keys = PAGES_PER_STEP * PAGE_SIZE
    expand = (np.arange(LANES)[:, None] == np.arange(keys)[None, :] // SEL_BLOCK).astype(np.float32)
    bnew = rows(qpos[:, None] - new_pos[None, :])
    return dict(
        bcmp=rows(qpos[:, None] - blk_end[None, :]),
        bwc=rows(qpos[:, None] - (pos0 - n_win + jnp.arange(n_win, dtype=jnp.int32))[None, :]),
        bwn=bnew, bseln=bnew,
        bsel=rows(qpos[:, None] - jnp.arange(pos0, dtype=jnp.int32)[None, :]),
        cov=jnp.asarray(cov, BF16), expand=jnp.asarray(expand, BF16))


def _layer_weights(l, norm_mix, w_in, conv_w, q_norm, k_norm, cmp_pe, cmp_w1, cmp_w2, out_norm_conv,
                   out_norm_att, w_out, norm_ffn, w_gate, w_up, w_down, d_conv, d_att, tm_p):
    perm = np.arange(d_att).reshape(N_KV, GROUP, HEAD_DIM).transpose(1, 0, 2).reshape(-1)
    q0 = 3 * d_conv
    w = w_in[l]
    k0 = q0 + d_att
    n_gates = w.shape[1] - (k0 + 6 * D_KV)
    seg = lambda i: w[:, k0 + i * D_KV:k0 + (i + 1) * D_KV]
    gates = jnp.concatenate([w[:, k0 + 6 * D_KV:], jnp.zeros((w.shape[0], LANES - n_gates), w.dtype)], axis=1)
    w_rows = jnp.concatenate([w[:, :q0], w[:, q0:k0][:, perm], w[:, k0:], jnp.zeros_like(gates[:, n_gates:])],
                             axis=1).astype(BF16)
    w_tok = jnp.concatenate([w[:, :q0], seg(0), seg(1), seg(2), seg(4)], axis=1).astype(BF16)
    w_chan = jnp.concatenate([w[:, q0:k0][:, perm], seg(3), seg(5), gates[:, :GATE_ROWS]], axis=1).T.astype(BF16)
    wo = w_out[l]
    wo = jnp.concatenate([wo[:d_conv], wo[d_conv:][perm]], axis=0).astype(BF16)
    ck = 2 * LANES
    n_ck = w_gate.shape[2] // ck
    pair = lambda v: jnp.tile(v, 2)[None, :]
    q_gain = q_norm[l] * (SCALE * LOG2E)

    def cmp_weights(i, gain):
        w1 = cmp_w1[l, i]
        z = jnp.zeros_like(w1[0])
        halves = []
        for h in range(2):
            rows_ = [jnp.concatenate([jnp.concatenate([w1[h * CMP_STRIDE + r], z], axis=1),
                                      jnp.concatenate([z, w1[h * CMP_STRIDE + r]], axis=1)], axis=0)
                     for r in range(CMP_STRIDE)]
            halves.append(jnp.concatenate(rows_, axis=0))
        w2 = cmp_w2[l, i]
        z2 = jnp.zeros_like(w2)
        return dict(
            wab=jnp.concatenate(halves, axis=1).astype(BF16),
            pe=jnp.broadcast_to(cmp_pe[l, i].reshape(1, -1), (8, CMP_LEN * HEAD_DIM)),
            w1f=w1.reshape(CMP_LEN * HEAD_DIM, CMP_HID),
            w2bd=jnp.concatenate([jnp.concatenate([w2, z2], axis=1),
                                  jnp.concatenate([z2, w2], axis=1)], axis=0).astype(BF16),
            gain=gain)

    return dict(
        norm_mix=norm_mix[l][None, :], w_in=w_rows, w_tok=w_tok, w_chan=w_chan,
        q_gain=pair(q_gain), q_gain_t=jnp.broadcast_to(q_gain[:, None], (HEAD_DIM, tm_p)),
        k_gain=jnp.stack([jnp.tile(k_norm[l, 1], 2), jnp.tile(k_norm[l, 2], 2)]),
        cmp_k=cmp_weights(0, pair(k_norm[l, 0])), cmp_v=cmp_weights(1, jnp.ones((1, LANES), F32)),
        conv_w=jnp.pad(conv_w[l], ((0, 8 - CONV_WIDTH), (0, 0))),
        norm_conv=out_norm_conv[l][None, :], norm_att=out_norm_att[l][perm][None, :], w_out=wo,
        norm_ffn=norm_ffn[l][None, :],
        w_gate=w_gate[l].reshape(-1, n_ck, ck).transpose(1, 0, 2).astype(BF16),
        w_up=w_up[l].reshape(-1, n_ck, ck).transpose(1, 0, 2).astype(BF16),
        w_down=w_down[l].reshape(n_ck, ck, -1).astype(BF16))


def _shifted(u, hist):
    up = jnp.concatenate([hist, u], axis=1)
    t = u.shape[1]
    return up[:, 1:1 + t], up[:, 0:t]


def _chan_major(a):
    a = jnp.moveaxis(a, -3, -1)
    return a.reshape(a.shape[:-3] + (D_KV, a.shape[-1]))


def _token_major(a_t):
    a = a_t.reshape(a_t.shape[:-2] + (N_KV, HEAD_DIM, a_t.shape[-1]))
    return jnp.moveaxis(a, -1, -3)


def kernel(x_prompt, x_sample, state_conv, cache_cmp_k, cache_cmp_v, cache_slc_k, cache_slc_v,
           cache_win_k, cache_win_v, page_table, rel_bias, norm_mix, w_in, conv_w, q_norm, k_norm,
           cmp_pe, cmp_w1, cmp_w2, out_norm_conv, out_norm_att, w_out, norm_ffn, w_gate, w_up, w_down):
    nb, t, dm = x_prompt.shape
    ns, t_new, _ = x_sample.shape
    depth = w_in.shape[0]
    d_conv = conv_w.shape[2]
    d_att = dm - d_conv
    n_pool = cache_cmp_k.shape[1]
    n_pages = page_table.shape[1]
    pos0 = n_pages * PAGE_SIZE
    n_win = cache_win_k.shape[2]
    assert t % TQ == 0 and d_att == N_HEADS * HEAD_DIM and n_pages % PAGES_PER_STEP == 0
    assert n_win == WINDOW and pos0 >= WINDOW and t_new <= LANES and t_new & (t_new - 1) == 0
    assert t // CMP_STRIDE == LANES and SEL_BLOCK == 64 and t >= WINDOW and t // SEL_BLOCK < LANES - 1

    tm_p = 512
    kv = lambda a, b_, t_: a.reshape(b_, t_, N_KV, HEAD_DIM)
    ptab = _prompt_tables(rel_bias, t)
    stab = _sample_tables(rel_bias, t_new, pos0, n_win, LANES)
    rows_pool = lambda c: c.reshape(depth * n_pool, PAGE_SIZE, LANES)
    chan_pool = lambda c: _chan_major(c).reshape(depth * n_pool, D_KV, PAGE_SIZE)
    cmp_pools = [rows_pool(cache_cmp_k), rows_pool(cache_cmp_v)]
    slc_pools = [chan_pool(cache_slc_k), chan_pool(cache_slc_v)]
    win_t = [_chan_major(cache_win_k), _chan_major(cache_win_v)]
    new_t = lambda a: jnp.pad(a.reshape(ns, t_new, LANES).transpose(0, 2, 1), ((0, 0), (0, 0), (0, LANES - t_new)))
    eye = jnp.eye(N_KV, dtype=BF16)[None, None, :, None, :, None]
    other = (jnp.arange(LANES)[None, :] // HEAD_DIM != jnp.arange(N_KV)[:, None])[None, :, :, None]

    xp = x_prompt.reshape(nb * t, dm)
    xs = x_sample.reshape(ns * t_new, dm)
    st_p, st_s = [], []
    for l in range(depth):
        lw = _layer_weights(l, norm_mix, w_in, conv_w, q_norm, k_norm, cmp_pe, cmp_w1, cmp_w2, out_norm_conv,
                            out_norm_att, w_out, norm_ffn, w_gate, w_up, w_down, d_conv, d_att, tm_p)

        gb, u, kc, vc, ks, kw, q_t, vs_t, vw_t, gates_t = _in_proj_t(
            xp, lw["norm_mix"], lw["w_tok"], lw["w_chan"], lw["q_gain_t"], lw["k_gain"], nb, d_conv, d_att, tm_p)
        seq = lambda a: a.reshape(nb, t, LANES)
        kcmp = _compress_rows(seq(kc), lw["cmp_k"], True, False)
        vcmp_t = _compress_rows(seq(vc), lw["cmp_v"], False, True)
        ks_aug = jnp.concatenate(
            [jnp.pad(seq(ks).astype(BF16), ((0, 0), (TQ, 0), (0, 0))),
             jnp.broadcast_to(ptab["sel_onehot"][None], (nb, TQ + t, LANES))], axis=2)
        kw_aug = jnp.concatenate(
            [jnp.pad(seq(kw).astype(BF16), ((0, 0), (WINDOW, 0), (0, 0))),
             jnp.broadcast_to(ptab["win_flag"][None], (nb, WINDOW + t, LANES))], axis=2)
        ones_rows = lambda v_t, front: jnp.where(
            other, 1.0, jnp.pad(v_t, ((0, 0), (0, 0), (front, 0)))[:, None]).astype(BF16)
        o_t = _pattn(q_t, gates_t, kcmp, vcmp_t, ks_aug, ones_rows(vs_t, TQ), kw_aug, ones_rows(vw_t, WINDOW), ptab)
        o_att = o_t.transpose(0, 3, 1, 2).reshape(nb * t, d_att)
        u3 = u.reshape(nb, t, d_conv)
        u1, u2 = _shifted(u3, jnp.zeros((nb, CONV_WIDTH - 1, d_conv), F32))
        xp = _mix_ffn(xp, gb, u, u1.reshape(nb * t, d_conv), u2.reshape(nb * t, d_conv), o_att, lw, tm_p)
        n_keep = min(WINDOW, t)
        st_p.append((u3[:, -(CONV_WIDTH - 1):], kv(kc, nb, t), kv(vc, nb, t), kv(ks, nb, t), _token_major(vs_t),
                     kv(kw, nb, t)[:, -n_keep:], _token_major(vw_t[:, :, -n_keep:])))

        gb, u, q, kc, vc, ks, vs, kw, vw, gates = _in_proj(
            xs, lw["norm_mix"], lw["w_in"], lw["q_gain"], lw["k_gain"], d_conv, d_att, ns * t_new)
        base = l * n_pool
        kcmp = _compress_paged(cmp_pools[0], page_table, base, lw["cmp_k"], True)
        vcmp = _compress_paged(cmp_pools[1], page_table, base, lw["cmp_v"], False)
        qr = q.reshape(ns, t_new, GROUP, N_KV, HEAD_DIM).transpose(0, 2, 3, 1, 4)
        qm = (qr[:, :, :, :, None, :] * eye).reshape(ns, N_HEADS * t_new, LANES)
        gr = gates[:, :3 * N_HEADS].reshape(ns, t_new, 3, N_KV, GROUP).transpose(0, 2, 4, 3, 1)
        gr = jnp.broadcast_to(gr.reshape(ns, 3, N_HEADS * t_new, 1), (ns, 3, N_HEADS * t_new, LANES))
        kw_t, vw_t = new_t(kw), new_t(vw)
        o = _sattn(qm, kcmp, vcmp, win_t[0][l], win_t[1][l], kw_t, vw_t, new_t(ks), new_t(vs),
                   slc_pools[0], slc_pools[1], page_table, base, gr, stab, t_new, pos0)
        o6 = o.reshape(ns, GROUP, N_KV, t_new, N_KV, HEAD_DIM)
        o_att = jnp.stack([o6[:, :, g, :, g] for g in range(N_KV)], axis=2)
        o_att = o_att.transpose(0, 3, 1, 2, 4).reshape(ns * t_new, d_att)
        u3 = u.reshape(ns, t_new, d_conv)
        u1, u2 = _shifted(u3, state_conv[l])
        xs = _mix_ffn(xs, gb, u, u1.reshape(-1, d_conv), u2.reshape(-1, d_conv), o_att, lw, ns * t_new)
        new_conv = jnp.concatenate([state_conv[l], u3], axis=1)[:, -(CONV_WIDTH - 1):]
        n_keep = min(WINDOW, n_win + t_new)
        keep = lambda c_t, n_t: _token_major(jnp.concatenate([c_t, n_t[:, :, :t_new]], axis=2)[:, :, -n_keep:])
        st_s.append((new_conv, kv(kc, ns, t_new), kv(vc, ns, t_new), kv(ks, ns, t_new), kv(vs, ns, t_new),
                     keep(win_t[0][l], kw_t), keep(win_t[1][l], vw_t)))

    outs_p = [jnp.stack(a) for a in zip(*st_p)]
    outs_s = [jnp.stack(a) for a in zip(*st_s)]
    return (xp.reshape(nb, t, dm), xs.reshape(ns, t_new, dm), *outs_p, *outs_s)
```

```python
import functools
import math

import jax
import jax.numpy as jnp
import numpy as np
from jax import lax
from jax.experimental import pallas as pl
from jax.experimental.pallas import tpu as pltpu

F32 = jnp.float32
BF16 = jnp.bfloat16

HEAD_DIM = 64
N_KV = 2
GROUP = 4
N_HEADS = N_KV * GROUP
D_KV = N_KV * HEAD_DIM
CMP_LEN = 32
CMP_STRIDE = 16
CMP_HID = 2 * HEAD_DIM
SEL_BLOCK = 64
SEL_TOPK = 16
WINDOW = 512
NUM_BUCKETS = 32
MAX_DISTANCE = 128
PAGE_SIZE = 128
CONV_WIDTH = 3
EPS = 1e-6
SCALE = HEAD_DIM ** -0.5
LOG2E = math.log2(math.e)
NEG = -1e30

LANES = 128
TQ = 128
GATE_ROWS = 32
PAGES_PER_STEP = 16
VMEM_LIMIT = 56 * 1024 * 1024

_NT = (((1,), (1,)), ((), ()))


def _dot(a, b):
    return jnp.dot(a, b, preferred_element_type=F32)


def _dot_nt(a, b):
    return lax.dot_general(a, b, _NT, preferred_element_type=F32)


def _params(n_axes):
    return pltpu.CompilerParams(dimension_semantics=("arbitrary",) * n_axes,
                                vmem_limit_bytes=VMEM_LIMIT)


def _resident(shape):
    nd = len(shape)
    return pl.BlockSpec(shape, lambda *_: (0,) * nd, pipeline_mode=pl.Buffered(1))


def _rms(x, gain):
    return x * lax.rsqrt(jnp.mean(x * x, axis=-1, keepdims=True) + EPS) * gain


def _pair_rms(y, gain, lo):
    y2 = y * y
    s_lo = jnp.sum(jnp.where(lo, y2, 0.0), axis=-1, keepdims=True)
    s_hi = jnp.sum(jnp.where(lo, 0.0, y2), axis=-1, keepdims=True)
    inv = jnp.where(lo, lax.rsqrt(s_lo / HEAD_DIM + EPS), lax.rsqrt(s_hi / HEAD_DIM + EPS))
    return y * inv * gain


def _in_proj_kernel(x_ref, nm_ref, w_ref, qg_ref, kg_ref,
                    gb_ref, u_ref, q_ref, kc_ref, vc_ref, ks_ref, vs_ref, kw_ref, vw_ref, gt_ref):
    x = x_ref[...]
    hb = _rms(x, nm_ref[...]).astype(BF16)
    d = gb_ref.shape[1]

    def seg(a, b):
        return _dot(hb, w_ref[:, a:b])

    gb_ref[...] = seg(0, d)
    u_ref[...] = seg(d, 2 * d) * seg(2 * d, 3 * d)
    lo = lax.broadcasted_iota(jnp.int32, (x.shape[0], LANES), 1) < HEAD_DIM
    c = 3 * d
    for j in range(q_ref.shape[1] // LANES):
        q_ref[:, j * LANES:(j + 1) * LANES] = _pair_rms(
            seg(c + j * LANES, c + (j + 1) * LANES), qg_ref[...], lo).astype(BF16)
    c += q_ref.shape[1]
    kc_ref[...] = seg(c, c + LANES)
    vc_ref[...] = seg(c + LANES, c + 2 * LANES)
    ks_ref[...] = _pair_rms(seg(c + 2 * LANES, c + 3 * LANES), kg_ref[0:1, :], lo)
    vs_ref[...] = seg(c + 3 * LANES, c + 4 * LANES)
    kw_ref[...] = _pair_rms(seg(c + 4 * LANES, c + 5 * LANES), kg_ref[1:2, :], lo)
    vw_ref[...] = seg(c + 5 * LANES, c + 6 * LANES)
    gt_ref[...] = jax.nn.sigmoid(seg(c + 6 * LANES, c + 7 * LANES))


def _in_proj(x, nm, w, qg, kg, d_conv, d_att, tm):
    n, dm = x.shape
    row = lambda w_: pl.BlockSpec((tm, w_), lambda i: (i, 0))
    widths = [d_conv, d_conv, d_att] + [LANES] * 7
    dtypes = [F32, F32, BF16] + [F32] * 7
    return pl.pallas_call(
        _in_proj_kernel,
        grid=(n // tm,),
        in_specs=[row(dm), _resident(nm.shape), _resident(w.shape), _resident(qg.shape), _resident(kg.shape)],
        out_specs=[row(w_) for w_ in widths],
        out_shape=[jax.ShapeDtypeStruct((n, w_), dt) for w_, dt in zip(widths, dtypes)],
        compiler_params=_params(1),
        name="in_proj",
    )(x, nm, w, qg, kg)


def _in_proj_t_kernel(x_ref, nm_ref, w_ref, wt_ref, qgt_ref, kg_ref,
                      gb_ref, u_ref, kc_ref, vc_ref, ks_ref, kw_ref, qt_ref, vst_ref, vwt_ref, gtt_ref):
    x = x_ref[...]
    hb = _rms(x, nm_ref[...]).astype(BF16)
    d = gb_ref.shape[1]

    def seg(a, b):
        return _dot(hb, w_ref[:, a:b])

    gb_ref[...] = seg(0, d)
    u_ref[...] = seg(d, 2 * d) * seg(2 * d, 3 * d)
    lo = lax.broadcasted_iota(jnp.int32, (x.shape[0], LANES), 1) < HEAD_DIM
    c = 3 * d
    kc_ref[...] = seg(c, c + LANES)
    vc_ref[...] = seg(c + LANES, c + 2 * LANES)
    ks_ref[...] = _pair_rms(seg(c + 2 * LANES, c + 3 * LANES), kg_ref[0:1, :], lo)
    kw_ref[...] = _pair_rms(seg(c + 3 * LANES, c + 4 * LANES), kg_ref[1:2, :], lo)

    def seg_t(a, b):
        return _dot_nt(wt_ref[a:b, :], hb)

    d_att = qt_ref.shape[1]
    for h in range(d_att // HEAD_DIM):
        y = seg_t(h * HEAD_DIM, (h + 1) * HEAD_DIM)
        inv = lax.rsqrt(jnp.mean(y * y, axis=0, keepdims=True) + EPS)
        qt_ref[0, h * HEAD_DIM:(h + 1) * HEAD_DIM, :] = (y * inv * qgt_ref[...]).astype(BF16)
    vst_ref[0] = seg_t(d_att, d_att + LANES)
    vwt_ref[0] = seg_t(d_att + LANES, d_att + 2 * LANES)
    gtt_ref[0] = jax.nn.sigmoid(seg_t(d_att + 2 * LANES, d_att + 2 * LANES + GATE_ROWS))


def _in_proj_t(x, nm, w, wt, qgt, kg, nb, d_conv, d_att, tm):
    n, dm = x.shape
    t = n // nb
    per_b = t // tm
    row = lambda w_: pl.BlockSpec((tm, w_), lambda i: (i, 0))
    col = lambda r: pl.BlockSpec((1, r, tm), lambda i: (i // per_b, 0, i % per_b))
    widths = [d_conv, d_conv] + [LANES] * 4
    rows_t = [d_att, LANES, LANES, GATE_ROWS]
    dt_t = [BF16, F32, F32, F32]
    return pl.pallas_call(
        _in_proj_t_kernel,
        grid=(n // tm,),
        in_specs=[row(dm), _resident(nm.shape), _resident(w.shape), _resident(wt.shape),
                  _resident(qgt.shape), _resident(kg.shape)],
        out_specs=[row(w_) for w_ in widths] + [col(r) for r in rows_t],
        out_shape=[jax.ShapeDtypeStruct((n, w_), F32) for w_ in widths]
        + [jax.ShapeDtypeStruct((nb, r, t), dt) for r, dt in zip(rows_t, dt_t)],
        compiler_params=_params(1),
        name="in_proj_t",
    )(x, nm, w, wt, qgt, kg)


def _cmp_ab_kernel(x_ref, w_ref, a_ref, b_ref):
    n_chunks = x_ref.shape[1] // CMP_STRIDE
    cols = [x_ref[0, pl.ds(l, n_chunks, stride=CMP_STRIDE), :] for l in range(CMP_STRIDE)]
    xr = jnp.concatenate(cols, axis=1).astype(BF16)
    ab = _dot(xr, w_ref[...])
    half = ab.shape[1] // 2
    a_ref[0] = ab[:, :half]
    b_ref[0] = ab[:, half:]


def _cmp_ab_paged_kernel(pt_ref, *refs, n_in):
    x_refs = refs[:n_in]
    perm_ref, w_ref, a_ref, b_ref = refs[n_in:]
    per_page = PAGE_SIZE // CMP_STRIDE
    rows = []
    for xr in x_refs:
        xp = _dot_nt(perm_ref[...], xr[0].astype(BF16))
        rows.append(jnp.concatenate([xp[l * per_page:(l + 1) * per_page] for l in range(CMP_STRIDE)], axis=1))
    xr = jnp.concatenate(rows, axis=0).astype(BF16)
    ab = _dot(xr, w_ref[...])
    half = ab.shape[1] // 2
    a_ref[0] = ab[:, :half]
    b_ref[0] = ab[:, half:]


def _cmp_fin_kernel(a_ref, b_ref, pe_ref, w1_ref, w2_ref, g_ref, o_ref, *, n_valid, normalize, transpose_out):
    a = a_ref[0]
    bm = b_ref[0]
    n_chunks = a.shape[0]
    b_next = pltpu.roll(bm, n_chunks - 1, 0)
    cpe = jnp.dot(pe_ref[...], w1_ref[...], preferred_element_type=F32,
                  precision=lax.Precision.HIGHEST)[0:1]
    x = a + b_next + jnp.concatenate([cpe, cpe], axis=1)
    hid = x * (0.5 * (1.0 + jnp.tanh(math.sqrt(2.0 / math.pi) * (x + 0.044715 * (x * x * x)))))
    out = _dot(hid.astype(BF16), w2_ref[...])
    if normalize:
        lo = lax.broadcasted_iota(jnp.int32, out.shape, 1) < HEAD_DIM
        out = _pair_rms(out, g_ref[...], lo)
    rows = lax.broadcasted_iota(jnp.int32, out.shape, 0)
    out = jnp.where(rows < n_valid, out, 0.0)
    o_ref[0] = out.T if transpose_out else out


def _cmp_fin(a, b, cw, n_valid, normalize, transpose_out=False):
    nb, n_chunks, wid = a.shape
    blk = pl.BlockSpec((1, n_chunks, wid), lambda i: (i, 0, 0))
    oshape = (LANES, n_chunks) if transpose_out else (n_chunks, LANES)
    return pl.pallas_call(
        functools.partial(_cmp_fin_kernel, n_valid=n_valid, normalize=normalize, transpose_out=transpose_out),
        grid=(nb,),
        in_specs=[blk, blk, _resident(cw["pe"].shape), _resident(cw["w1f"].shape),
                  _resident(cw["w2bd"].shape), _resident(cw["gain"].shape)],
        out_specs=pl.BlockSpec((1,) + oshape, lambda i: (i, 0, 0)),
        out_shape=jax.ShapeDtypeStruct((nb,) + oshape, F32),
        compiler_params=_params(1),
        name="cmp_fin",
    )(a, b, cw["pe"], cw["w1f"], cw["w2bd"], cw["gain"])


def _compress_rows(rows, cw, normalize, transpose_out):
    nb, t, _ = rows.shape
    n_chunks = t // CMP_STRIDE
    wid = cw["wab"].shape[1] // 2
    out_blk = pl.BlockSpec((1, n_chunks, wid), lambda i: (i, 0, 0))
    a, b = pl.pallas_call(
        _cmp_ab_kernel,
        grid=(nb,),
        in_specs=[pl.BlockSpec((1, t, LANES), lambda i: (i, 0, 0)), _resident(cw["wab"].shape)],
        out_specs=[out_blk, out_blk],
        out_shape=[jax.ShapeDtypeStruct((nb, n_chunks, wid), F32)] * 2,
        compiler_params=_params(1),
        name="cmp_ab_rows",
    )(rows, cw["wab"])
    return _cmp_fin(a, b, cw, (t - CMP_LEN) // CMP_STRIDE + 1, normalize, transpose_out)


def _page_spec(i, base, pages_per_step):
    return pl.BlockSpec((1, PAGE_SIZE, LANES),
                        lambda b, s, pt: (base + pt[b, s * pages_per_step + i], 0, 0))


def _compress_paged(cache, page_table, base, cw, normalize):
    nb, n_pages = page_table.shape
    pps = PAGES_PER_STEP
    per_page = PAGE_SIZE // CMP_STRIDE
    ch = pps * per_page
    wid = cw["wab"].shape[1] // 2
    n_chunks = n_pages * per_page
    out_blk = pl.BlockSpec((1, ch, wid), lambda b, s, pt: (b, s, 0))
    r = np.arange(PAGE_SIZE)
    perm = jnp.asarray(r[None, :] == (r % per_page)[:, None] * CMP_STRIDE + (r // per_page)[:, None], BF16)
    cst = lambda a: pl.BlockSpec(a.shape, lambda b, s, pt: (0, 0), pipeline_mode=pl.Buffered(1))
    a, b = pl.pallas_call(
        functools.partial(_cmp_ab_paged_kernel, n_in=pps),
        grid_spec=pltpu.PrefetchScalarGridSpec(
            num_scalar_prefetch=1,
            grid=(nb, n_pages // pps),
            in_specs=[_page_spec(i, base, pps) for i in range(pps)] + [cst(perm), cst(cw["wab"])],
            out_specs=[out_blk, out_blk]),
        out_shape=[jax.ShapeDtypeStruct((nb, n_chunks, wid), F32)] * 2,
        compiler_params=_params(2),
        name="cmp_ab_paged",
    )(page_table, *([cache] * pps), perm, cw["wab"])
    return _cmp_fin(a, b, cw, n_chunks - 1, normalize)


def _split_bf16(x):
    hi = x.astype(BF16)
    return hi, (x - hi.astype(F32)).astype(BF16)


def _online_update(state, s, mask, v_t):
    m, l, acc = state
    s = jnp.where(mask, s, NEG)
    m_new = jnp.maximum(m, jnp.max(s, axis=-1, keepdims=True))
    alpha = jnp.exp2(m - m_new)
    e = jnp.where(mask, jnp.exp2(s - m_new), 0.0)
    l = alpha * l + jnp.sum(e, axis=-1, keepdims=True)
    acc = alpha * acc + _dot_nt(e.astype(BF16), v_t)
    return m_new, l, acc


def _finish(state):
    _, l, acc = state
    return acc / jnp.where(l > 0, l, 1.0)


def _reset_t(m_ref, acc_ref):
    m_ref[...] = jnp.full(m_ref.shape, NEG, F32)
    acc_ref[...] = jnp.zeros(acc_ref.shape, F32)


def _update_t(m_ref, acc_ref, s_t, v_t):
    m = m_ref[...]
    m_new = jnp.maximum(m, jnp.max(s_t, axis=0, keepdims=True))
    alpha = jnp.exp2(m - m_new)
    e = jnp.exp2(s_t - m_new).astype(BF16)
    acc_ref[...] = alpha * acc_ref[...] + _dot(v_t, e)
    m_ref[...] = m_new


def _pattn_kernel(qt_ref, gt_ref, kcmp_ref, vcmpt_ref, ksa_ref, vst_ref, kwa_ref, vwt_ref,
                  bcmp_ref, stab_ref, wa_ref, wb_ref, diag_ref, covt_ref, o_ref, m_ref, acc_ref, qa_ref, out_ref, *, n_sel, topk):
    qi = pl.program_id(1)
    q0 = qi * TQ
    n_slabs = qt_ref.shape[1] // LANES
    wide = n_slabs * TQ
    sub = lax.broadcasted_iota(jnp.int32, (LANES, TQ), 0)
    lane = lax.broadcasted_iota(jnp.int32, (LANES, TQ), 1)
    sub_w = lax.broadcasted_iota(jnp.int32, (LANES, wide), 0)
    tpos_w = q0 + (lax.broadcasted_iota(jnp.int32, (LANES, wide), 1) & (TQ - 1))
    kcmp = kcmp_ref[0].astype(BF16)
    vcmp_t = vcmpt_ref[0].astype(BF16)
    covt = covt_ref[...]
    gt = gt_ref[0]
    zero_rows = jnp.zeros((LANES - n_sel, TQ), F32)
    def gate(br, g):
        return jnp.concatenate([gt[br * N_HEADS + g * GROUP + j:br * N_HEADS + g * GROUP + j + 1, :]
                                for j in range(n_slabs)], axis=1)

    for g in range(N_KV):
        mine = (sub < HEAD_DIM) if g == 0 else (sub >= HEAD_DIM)
        q_t = jnp.concatenate(
            [jnp.where(mine, qt_ref[0, j * LANES:(j + 1) * LANES, :], jnp.zeros((), BF16))
             for j in range(n_slabs)], axis=1)

        s = _dot(kcmp, q_t) + bcmp_ref[g, 0]
        mask = sub_w * CMP_STRIDE + (CMP_LEN - 1) <= tpos_w
        s = jnp.where(mask, s, NEG)
        e = jnp.where(mask, jnp.exp2(s - jnp.max(s, axis=0, keepdims=True)), 0.0)
        l = jnp.sum(e, axis=0, keepdims=True)
        p = e * (1.0 / jnp.where(l > 0, l, 1.0))
        out_ref[g] = gate(0, g) * _dot(vcmp_t, p.astype(BF16))
        p_sum = p[:, 0:TQ]
        for j in range(1, n_slabs):
            p_sum = p_sum + p[:, j * TQ:(j + 1) * TQ]

        hi, lo_ = _split_bf16(p_sum)
        sc = _dot(covt, hi) + _dot(covt, lo_)
        tq_pos = q0 + lane
        cur = tq_pos >> 6
        forced = (sub == 0) | (sub == cur) | (sub == cur - 1)
        sc = jnp.where(forced, 1e6, jnp.where(sub * SEL_BLOCK > tq_pos, -1e6, sc))
        sc = sc[0:n_sel]
        blk = sub[0:n_sel]
        rank = jnp.zeros((n_sel, TQ), F32)
        for i in range(n_sel):
            si = sc[i:i + 1, :]
            beats = (si > sc) | ((si == sc) & (blk > i))
            rank = rank + jnp.where(beats, 1.0, 0.0)
        pen = jnp.concatenate([jnp.where(rank < topk, 0.0, NEG), zero_rows], axis=0)
        pen = jnp.where(sub == LANES - 1, NEG, pen).astype(BF16)
        qa_ref[g, 0:LANES, :] = q_t
        qa_ref[g, LANES:2 * LANES, :] = jnp.concatenate([pen] * n_slabs, axis=1)

    def keys(k_ref, v_ref, g, kt, n):
        start = pl.multiple_of(kt * TQ, TQ)
        return k_ref[0, pl.ds(start, n * TQ), :], v_ref[0, g, :, pl.ds(start, n * TQ)]

    def add_branch(br):
        for g in range(N_KV):
            acc = acc_ref[g]
            den = acc[HEAD_DIM:HEAD_DIM + 1] if g == 0 else acc[0:1]
            out_ref[g] += (gate(br, g) * (1.0 / den)) * acc

    _reset_t(m_ref, acc_ref)
    n_far = jnp.maximum(qi - 1, 0)

    @pl.loop(0, n_far // 2)
    def _(i):
        for g in range(N_KV):
            k_, v_ = keys(ksa_ref, vst_ref, g, 1 + 2 * i, 2)
            _update_t(m_ref.at[g], acc_ref.at[g], _dot(k_, qa_ref[g]), v_)

    odd = jnp.where(n_far % 2 == 1, n_far, 0)
    for g in range(N_KV):
        k0, v0 = keys(ksa_ref, vst_ref, g, odd, 1)
        k1, v1 = keys(ksa_ref, vst_ref, g, qi, 2)
        s_ = _dot(jnp.concatenate([k0, k1], axis=0), qa_ref[g]) + stab_ref[g]
        _update_t(m_ref.at[g], acc_ref.at[g], s_, jnp.concatenate([v0, v1], axis=1))
    add_branch(1)

    _reset_t(m_ref, acc_ref)
    pad_pen = jnp.where(sub_w == 0, NEG, 0.0).astype(BF16)
    for g in range(N_KV):
        q_win = jnp.concatenate([qa_ref[g, 0:LANES, :], pad_pen], axis=0)
        k_, v_ = keys(kwa_ref, vwt_ref, g, qi, 2)
        _update_t(m_ref.at[g], acc_ref.at[g], _dot(k_, q_win) + wa_ref[...], v_)
        k_, v_ = keys(kwa_ref, vwt_ref, g, qi + 2, 2)
        _update_t(m_ref.at[g], acc_ref.at[g], _dot(k_, q_win) + wb_ref[g], v_)
        k_, v_ = keys(kwa_ref, vwt_ref, g, qi + 4, 1)
        _update_t(m_ref.at[g], acc_ref.at[g], _dot(k_, q_win) + diag_ref[g], v_)
    add_branch(2)

    out = jnp.where(sub_w < HEAD_DIM, out_ref[0], out_ref[1])
    for j in range(n_slabs):
        o_ref[0, j] = out[:, j * TQ:(j + 1) * TQ]


def _pattn(q_t, gates_t, kcmp, vcmp_t, ks_aug, vs_t, kw_aug, vw_t, tabs):
    nb, d_att, t = q_t.shape
    n_q = t // TQ
    n_sel = -(-t // SEL_BLOCK)
    n_slabs = d_att // LANES
    whole = lambda a: pl.BlockSpec((1,) + a.shape[1:], lambda b, i: (b,) + (0,) * (a.ndim - 1))
    return pl.pallas_call(
        functools.partial(_pattn_kernel, n_sel=n_sel, topk=min(SEL_TOPK, n_sel)),
        grid=(nb, n_q),
        in_specs=[pl.BlockSpec((1, d_att, TQ), lambda b, i: (b, 0, i)),
                  pl.BlockSpec((1, GATE_ROWS, TQ), lambda b, i: (b, 0, i)),
                  whole(kcmp), whole(vcmp_t), whole(ks_aug), whole(vs_t), whole(kw_aug), whole(vw_t),
                  pl.BlockSpec((N_KV, 1, LANES, n_slabs * TQ), lambda b, i: (0, i, 0, 0)),
                  _resident(tabs["sel"].shape), _resident(tabs["win_a"].shape), _resident(tabs["win_b"].shape),
                  _resident(tabs["diag"].shape), _resident(tabs["covt"].shape)],
        out_specs=pl.BlockSpec((1, n_slabs, LANES, TQ), lambda b, i: (b, 0, 0, i)),
        out_shape=jax.ShapeDtypeStruct((nb, n_slabs, LANES, t), F32),
        scratch_shapes=[pltpu.VMEM((N_KV, 1, n_slabs * TQ), F32), pltpu.VMEM((N_KV, LANES, n_slabs * TQ), F32),
                        pltpu.VMEM((N_KV, 2 * LANES, n_slabs * TQ), BF16), pltpu.VMEM((N_KV, LANES, n_slabs * TQ), F32)],
        compiler_params=_params(2),
        name="pattn",
    )(q_t, gates_t, kcmp, vcmp_t, ks_aug, vs_t, kw_aug, vw_t, tabs["bcmp"], tabs["sel"], tabs["win_a"],
      tabs["win_b"], tabs["diag"], tabs["covt"])


def _sattn1_kernel(qm_ref, kcmp_ref, vcmp_ref, kwc_ref, vwc_ref, kwn_ref, vwn_ref,
                   bcmp_ref, bwc_ref, bwn_ref, cov_ref, ocmp_ref, owin_ref, sel_ref,
                   *, n_cmp, n_sel, topk, t_new, pos0, win_start):
    qm = qm_ref[0]
    n_rows = qm.shape[0]
    n_gt = N_KV * t_new
    s = _dot_nt(qm, kcmp_ref[0].astype(BF16)) + bcmp_ref[...]
    ncol = lax.broadcasted_iota(jnp.int32, s.shape, 1)
    trow = pos0 + (lax.broadcasted_iota(jnp.int32, s.shape, 0) & (t_new - 1))
    mask = (ncol < n_cmp) & (ncol * CMP_STRIDE + (CMP_LEN - 1) <= trow)
    s = jnp.where(mask, s, NEG)
    e = jnp.where(mask, jnp.exp2(s - jnp.max(s, axis=-1, keepdims=True)), 0.0)
    l = jnp.sum(e, axis=-1, keepdims=True)
    p = e / jnp.where(l > 0, l, 1.0)
    ocmp_ref[0] = _dot(p.astype(BF16), vcmp_ref[0].astype(BF16))
    p_sum = p[0:n_gt]
    for j in range(1, n_rows // n_gt):
        p_sum = p_sum + p[j * n_gt:(j + 1) * n_gt]
    hi, lo = _split_bf16(p_sum)
    sc = _dot(hi, cov_ref[...]) + _dot(lo, cov_ref[...])
    blk = lax.broadcasted_iota(jnp.int32, sc.shape, 1)
    qpos = pos0 + (lax.broadcasted_iota(jnp.int32, sc.shape, 0) & (t_new - 1))
    cur = qpos >> 6
    forced = (blk == 0) | (blk == cur) | (blk == cur - 1)
    sc = jnp.where(forced, 1e6, jnp.where(blk * SEL_BLOCK > qpos, -1e6, sc))
    sc = jnp.where(blk < n_sel, sc, -3e38)
    rank = jnp.zeros(sc.shape, F32)
    for i in range(n_sel):
        si = sc[:, i:i + 1]
        beats = (si > sc) | ((si == sc) & (blk > i))
        rank = rank + jnp.where(beats, 1.0, 0.0)
    sel_ref[0] = jnp.where((rank < topk) & (blk < n_sel), 1.0, 0.0)

    init = (jnp.full((n_rows, 1), NEG, F32), jnp.zeros((n_rows, 1), F32), jnp.zeros((n_rows, LANES), F32))
    s = _dot(qm, kwc_ref[0].astype(BF16)) + bwc_ref[...]
    trow = pos0 + (lax.broadcasted_iota(jnp.int32, s.shape, 0) & (t_new - 1))
    dist = trow - (win_start + lax.broadcasted_iota(jnp.int32, s.shape, 1))
    st = _online_update(init, s, (dist >= 0) & (dist < WINDOW), vwc_ref[0].astype(BF16))
    s = _dot(qm, kwn_ref[0].astype(BF16)) + bwn_ref[...]
    trow = pos0 + (lax.broadcasted_iota(jnp.int32, s.shape, 0) & (t_new - 1))
    col = lax.broadcasted_iota(jnp.int32, s.shape, 1)
    dist = trow - (pos0 + col)
    st = _online_update(st, s, (dist >= 0) & (dist < WINDOW) & (col < t_new), vwn_ref[0].astype(BF16))
    owin_ref[0] = _finish(st)


def _sattn2_kernel(pt_ref, *refs, pps, t_new):
    k_refs = refs[:pps]
    v_refs = refs[pps:2 * pps]
    (qm_ref, selc_ref, seln_ref, ksn_ref, vsn_ref, bsel_ref, bseln_ref, exp_ref,
     ocmp_ref, owin_ref, gate_ref, o_ref, m_sc, l_sc, acc_sc) = refs[2 * pps:]
    step = pl.program_id(1)
    qm = qm_ref[0]
    n_rows = qm.shape[0]
    reps = n_rows // selc_ref.shape[2]

    @pl.when(step == 0)
    def _():
        s = _dot(qm, ksn_ref[0].astype(BF16)) + bseln_ref[...]
        trow = lax.broadcasted_iota(jnp.int32, s.shape, 0) & (t_new - 1)
        col = lax.broadcasted_iota(jnp.int32, s.shape, 1)
        hit = jnp.concatenate([seln_ref[0]] * reps, axis=0) > 0.5
        init = (jnp.full((n_rows, 1), NEG, F32), jnp.zeros((n_rows, 1), F32),
                jnp.zeros((n_rows, LANES), F32))
        m, l, acc = _online_update(init, s, hit & (col <= trow) & (col < t_new), vsn_ref[0].astype(BF16))
        m_sc[...] = m
        l_sc[...] = l
        acc_sc[...] = acc

    k_t = jnp.concatenate([r[0] for r in k_refs], axis=1).astype(BF16)
    v_t = jnp.concatenate([r[0] for r in v_refs], axis=1).astype(BF16)
    s = _dot(qm, k_t) + bsel_ref[...]
    hit = _dot(selc_ref[0, 0].astype(BF16), exp_ref[...])
    mask = jnp.concatenate([hit] * reps, axis=0) > 0.5
    m, l, acc = _online_update((m_sc[...], l_sc[...], acc_sc[...]), s, mask, v_t)
    m_sc[...] = m
    l_sc[...] = l
    acc_sc[...] = acc

    @pl.when(step == pl.num_programs(1) - 1)
    def _():
        o_sel = _finish((m, l, acc))
        o_ref[0] = gate_ref[0, 0] * ocmp_ref[0] + gate_ref[0, 1] * o_sel + gate_ref[0, 2] * owin_ref[0]


def _sattn(qm, kcmp, vcmp, kwc_t, vwc_t, kwn_t, vwn_t, ksn_t, vsn_t, cache_k, cache_v, page_table, base,
           gates, tabs, t_new, pos0):
    nb, n_rows, _ = qm.shape
    n_pages = page_table.shape[1]
    n_chunks = kcmp.shape[1]
    n_gt = N_KV * t_new
    n_sel = -(-(pos0 + t_new) // SEL_BLOCK)
    selw = tabs["cov"].shape[1]
    per_b = lambda shape: pl.BlockSpec((1,) + shape, lambda b: (b,) + (0,) * len(shape))
    ocmp, owin, sel = pl.pallas_call(
        functools.partial(_sattn1_kernel, n_cmp=n_chunks - 1, n_sel=n_sel, topk=min(SEL_TOPK, n_sel),
                          t_new=t_new, pos0=pos0, win_start=pos0 - kwc_t.shape[2]),
        grid=(nb,),
        in_specs=[per_b((n_rows, LANES)), per_b((n_chunks, LANES)), per_b((n_chunks, LANES)),
                  per_b(kwc_t.shape[1:]), per_b(vwc_t.shape[1:]), per_b(kwn_t.shape[1:]), per_b(vwn_t.shape[1:]),
                  _resident(tabs["bcmp"].shape), _resident(tabs["bwc"].shape), _resident(tabs["bwn"].shape),
                  _resident(tabs["cov"].shape)],
        out_specs=[per_b((n_rows, LANES)), per_b((n_rows, LANES)), per_b((n_gt, selw))],
        out_shape=[jax.ShapeDtypeStruct((nb, n_rows, LANES), F32), jax.ShapeDtypeStruct((nb, n_rows, LANES), F32),
                   jax.ShapeDtypeStruct((nb, n_gt, selw), F32)],
        compiler_params=_params(1),
        name="sattn1",
    )(qm, kcmp, vcmp, kwc_t, vwc_t, kwn_t, vwn_t, tabs["bcmp"], tabs["bwc"], tabs["bwn"], tabs["cov"])

    pps = PAGES_PER_STEP
    n_steps = n_pages // pps
    bps = pps * PAGE_SIZE // SEL_BLOCK
    n_cblk = n_pages * PAGE_SIZE // SEL_BLOCK
    selc = sel[:, :, :n_cblk].reshape(nb, n_gt, n_steps, bps).transpose(0, 2, 1, 3)
    selc = jnp.pad(selc, ((0, 0), (0, 0), (0, 0), (0, LANES - bps)))
    seln = jnp.broadcast_to(sel[:, :, n_cblk:n_cblk + 1], (nb, n_gt, LANES))

    keys = pps * PAGE_SIZE
    cst = lambda shape: pl.BlockSpec(shape, lambda b, s, pt: (0,) * len(shape), pipeline_mode=pl.Buffered(1))
    pb = lambda shape: pl.BlockSpec((1,) + shape, lambda b, s, pt: (b,) + (0,) * len(shape))
    return pl.pallas_call(
        functools.partial(_sattn2_kernel, pps=pps, t_new=t_new),
        grid_spec=pltpu.PrefetchScalarGridSpec(
            num_scalar_prefetch=1,
            grid=(nb, n_steps),
            in_specs=[_page_spec(i, base, pps) for i in range(pps)] * 2
            + [pb((n_rows, LANES)),
               pl.BlockSpec((1, 1, n_gt, LANES), lambda b, s, pt: (b, s, 0, 0)),
               pb((n_gt, LANES)), pb(ksn_t.shape[1:]), pb(vsn_t.shape[1:]),
               pl.BlockSpec((n_rows, keys), lambda b, s, pt: (0, s)),
               cst(tabs["bseln"].shape), cst(tabs["expand"].shape),
               pb((n_rows, LANES)), pb((n_rows, LANES)), pb((3, n_rows, LANES))],
            out_specs=pb((n_rows, LANES)),
            scratch_shapes=[pltpu.VMEM((n_rows, 1), F32), pltpu.VMEM((n_rows, 1), F32),
                            pltpu.VMEM((n_rows, LANES), F32)]),
        out_shape=jax.ShapeDtypeStruct((nb, n_rows, LANES), F32),
        compiler_params=_params(2),
        name="sattn2",
    )(page_table, *([cache_k] * pps), *([cache_v] * pps), qm, selc, seln, ksn_t, vsn_t,
      tabs["bsel"], tabs["bseln"], tabs["expand"], ocmp, owin, gates)


def _mix_ffn_kernel(x_ref, gb_ref, u_ref, u1_ref, u2_ref, oa_ref, cw_ref, nc_ref, na_ref, wo_ref,
                    nf_ref, wg_ref, wu_ref, wd_ref, y_ref, acc_ref):
    d = gb_ref.shape[1]
    conv = u2_ref[...] * cw_ref[0:1, :] + u1_ref[...] * cw_ref[1:2, :] + u_ref[...] * cw_ref[2:3, :]
    yc = _rms(gb_ref[...] * conv, nc_ref[...]).astype(BF16)
    oa = _rms(oa_ref[...], na_ref[...]).astype(BF16)
    x1 = x_ref[...] + (_dot(yc, wo_ref[0:d, :]) + _dot(oa, wo_ref[d:, :]))
    h2 = _rms(x1, nf_ref[...]).astype(BF16)
    acc_ref[...] = jnp.zeros_like(acc_ref)
    for c in range(wg_ref.shape[0]):
        gate = _dot(h2, wg_ref[c])
        act = (gate * jax.nn.sigmoid(gate)) * _dot(h2, wu_ref[c])
        acc_ref[...] += _dot(act.astype(BF16), wd_ref[c])
    y_ref[...] = x1 + acc_ref[...]


def _mix_ffn(x, gb, u, u1, u2, oa, lw, tm):
    n, dm = x.shape
    d = gb.shape[1]
    row = lambda w_: pl.BlockSpec((tm, w_), lambda i: (i, 0))
    consts = [lw["conv_w"], lw["norm_conv"], lw["norm_att"], lw["w_out"], lw["norm_ffn"],
              lw["w_gate"], lw["w_up"], lw["w_down"]]
    return pl.pallas_call(
        _mix_ffn_kernel,
        grid=(n // tm,),
        in_specs=[row(dm)] + [row(d)] * 5 + [_resident(c.shape) for c in consts],
        out_specs=row(dm),
        out_shape=jax.ShapeDtypeStruct((n, dm), F32),
        scratch_shapes=[pltpu.VMEM((tm, dm), F32)],
        compiler_params=_params(1),
        name="mix_ffn",
    )(x, gb, u, u1, u2, oa, *consts)


def _t5_bucket(dist):
    n = jnp.maximum(dist, 0)
    max_exact = NUM_BUCKETS // 2
    nf = jnp.maximum(n, 1).astype(F32)
    large = max_exact + (jnp.log(nf / max_exact) / math.log(MAX_DISTANCE / max_exact)
                         * (NUM_BUCKETS - max_exact)).astype(jnp.int32)
    return jnp.where(n < max_exact, n, jnp.minimum(large, NUM_BUCKETS - 1))


def _bias(dist, tbl):
    onehot = jax.nn.one_hot(_t5_bucket(dist), NUM_BUCKETS, dtype=F32)
    return jnp.einsum("...k,kh->h...", onehot, tbl, precision=lax.Precision.HIGHEST)


def _cover(n_cmp, n_sel):
    c0 = np.arange(n_cmp)[:, None] * CMP_STRIDE
    s0 = np.arange(n_sel)[None, :] * SEL_BLOCK
    return np.clip(np.minimum(c0 + CMP_LEN, s0 + SEL_BLOCK) - np.maximum(c0, s0), 0, None) / CMP_STRIDE


def _prompt_tables(tbl, t):
    n_q = t // TQ
    n_chunks = t // CMP_STRIDE
    ar = jnp.arange(TQ, dtype=jnp.int32)
    far = tbl[NUM_BUCKETS - 1]
    assert MAX_DISTANCE <= TQ

    def key_major(b):
        k, q = b.shape[1:]
        return b.reshape(N_KV, GROUP, k, q).transpose(0, 2, 1, 3).reshape(N_KV, k, GROUP * q)

    near = []
    for c in range(2):
        dist = c * TQ + ar[None, :] - ar[:, None]
        b = (_bias(dist, tbl) - far[:, None, None]) * LOG2E
        near.append(key_major(jnp.where(dist[None] >= 0, b, NEG)))
    oldest = jnp.where(ar[:, None] > ar[None, :], 0.0, NEG)
    near.append(key_major(jnp.broadcast_to(oldest[None], (N_HEADS, TQ, TQ))))
    tpos = jnp.arange(t, dtype=jnp.int32)
    blk_end = jnp.arange(n_chunks, dtype=jnp.int32) * CMP_STRIDE + (CMP_LEN - 1)
    b = _bias(tpos[None, :] - blk_end[:, None], tbl) * LOG2E
    b = b.reshape(N_KV, GROUP, n_chunks, n_q, TQ).transpose(0, 3, 2, 1, 4).reshape(N_KV, n_q, n_chunks, GROUP * TQ)
    n_cmp = (t - CMP_LEN) // CMP_STRIDE + 1
    n_sel = -(-t // SEL_BLOCK)
    covt = np.zeros((LANES, n_chunks), np.float32)
    covt[:n_sel, :n_cmp] = _cover(n_cmp, n_sel).T
    keys = np.arange(-TQ, t)
    onehot = (keys[:, None] // SEL_BLOCK == np.arange(LANES)[None, :]) & (keys[:, None] >= 0)
    onehot[:, LANES - 1] = keys < 0
    wkeys = np.arange(-WINDOW, t)
    wflag = np.zeros((WINDOW + t, LANES), np.float32)
    wflag[:, 0] = wkeys < 0
    diag, prev, oldest = near
    none = jnp.zeros_like(prev)
    tabs = dict(sel=jnp.concatenate([none, prev, diag], axis=1), win_a=jnp.concatenate([oldest, none], axis=1)[0],
                win_b=jnp.concatenate([none, prev], axis=1), diag=diag)
    return dict(**tabs, bcmp=b, covt=jnp.asarray(covt, BF16),
                sel_onehot=jnp.asarray(onehot.astype(np.float32), BF16), win_flag=jnp.asarray(wflag, BF16))


def _sample_tables(tbl, t_new, pos0, n_win, n_pad):
    n_chunks = pos0 // CMP_STRIDE
    n_cmp = n_chunks - 1
    n_sel = -(-(pos0 + t_new) // SEL_BLOCK)
    selw = -(-n_sel // LANES) * LANES
    qpos = pos0 + jnp.arange(t_new, dtype=jnp.int32)

    def rows(dist):
        b = _bias(dist, tbl) * LOG2E
        return b.reshape(N_KV, GROUP, t_new, -1).transpose(1, 0, 2, 3).reshape(N_HEADS * t_new, -1)

    blk_end = jnp.arange(n_chunks, dtype=jnp.int32) * CMP_STRIDE + (CMP_LEN - 1)
    new_pos = pos0 + jnp.arange(n_pad, dtype=jnp.int32)
    cov = np.zeros((n_chunks, selw), np.float32)
    cov[:n_cmp, :n_sel] = _cover(n_cmp, n_sel)
    keys = PAGES_PER_STEP * PAGE_SIZE
    expand = (np.arange(LANES)[:, None] == np.arange(keys)[None, :] // SEL_BLOCK).astype(np.float32)
    bnew = rows(qpos[:, None] - new_pos[None, :])
    return dict(
        bcmp=rows(qpos[:, None] - blk_end[None, :]),
        bwc=rows(qpos[:, None] - (pos0 - n_win + jnp.arange(n_win, dtype=jnp.int32))[None, :]),
        bwn=bnew, bseln=bnew,
        bsel=rows(qpos[:, None] - jnp.arange(pos0, dtype=jnp.int32)[None, :]),
        cov=jnp.asarray(cov, BF16), expand=jnp.asarray(expand, BF16))


def _layer_weights(l, norm_mix, w_in, conv_w, q_norm, k_norm, cmp_pe, cmp_w1, cmp_w2, out_norm_conv,
                   out_norm_att, w_out, norm_ffn, w_gate, w_up, w_down, d_conv, d_att, tm_p):
    perm = np.arange(d_att).reshape(N_KV, GROUP, HEAD_DIM).transpose(1, 0, 2).reshape(-1)
    q0 = 3 * d_conv
    w = w_in[l]
    k0 = q0 + d_att
    n_gates = w.shape[1] - (k0 + 6 * D_KV)
    seg = lambda i: w[:, k0 + i * D_KV:k0 + (i + 1) * D_KV]
    gates = jnp.concatenate([w[:, k0 + 6 * D_KV:], jnp.zeros((w.shape[0], LANES - n_gates), w.dtype)], axis=1)
    w_rows = jnp.concatenate([w[:, :q0], w[:, q0:k0][:, perm], w[:, k0:], jnp.zeros_like(gates[:, n_gates:])],
                             axis=1).astype(BF16)
    w_tok = jnp.concatenate([w[:, :q0], seg(0), seg(1), seg(2), seg(4)], axis=1).astype(BF16)
    w_chan = jnp.concatenate([w[:, q0:k0][:, perm], seg(3), seg(5), gates[:, :GATE_ROWS]], axis=1).T.astype(BF16)
    wo = w_out[l]
    wo = jnp.concatenate([wo[:d_conv], wo[d_conv:][perm]], axis=0).astype(BF16)
    ck = 2 * LANES
    n_ck = w_gate.shape[2] // ck
    pair = lambda v: jnp.tile(v, 2)[None, :]
    q_gain = q_norm[l] * (SCALE * LOG2E)

    def cmp_weights(i, gain):
        w1 = cmp_w1[l, i]
        z = jnp.zeros_like(w1[0])
        halves = []
        for h in range(2):
            rows_ = [jnp.concatenate([jnp.concatenate([w1[h * CMP_STRIDE + r], z], axis=1),
                                      jnp.concatenate([z, w1[h * CMP_STRIDE + r]], axis=1)], axis=0)
                     for r in range(CMP_STRIDE)]
            halves.append(jnp.concatenate(rows_, axis=0))
        w2 = cmp_w2[l, i]
        z2 = jnp.zeros_like(w2)
        return dict(
            wab=jnp.concatenate(halves, axis=1).astype(BF16),
            pe=jnp.broadcast_to(cmp_pe[l, i].reshape(1, -1), (8, CMP_LEN * HEAD_DIM)),
            w1f=w1.reshape(CMP_LEN * HEAD_DIM, CMP_HID),
            w2bd=jnp.concatenate([jnp.concatenate([w2, z2], axis=1),
                                  jnp.concatenate([z2, w2], axis=1)], axis=0).astype(BF16),
            gain=gain)

    return dict(
        norm_mix=norm_mix[l][None, :], w_in=w_rows, w_tok=w_tok, w_chan=w_chan,
        q_gain=pair(q_gain), q_gain_t=jnp.broadcast_to(q_gain[:, None], (HEAD_DIM, tm_p)),
        k_gain=jnp.stack([jnp.tile(k_norm[l, 1], 2), jnp.tile(k_norm[l, 2], 2)]),
        cmp_k=cmp_weights(0, pair(k_norm[l, 0])), cmp_v=cmp_weights(1, jnp.ones((1, LANES), F32)),
        conv_w=jnp.pad(conv_w[l], ((0, 8 - CONV_WIDTH), (0, 0))),
        norm_conv=out_norm_conv[l][None, :], norm_att=out_norm_att[l][perm][None, :], w_out=wo,
        norm_ffn=norm_ffn[l][None, :],
        w_gate=w_gate[l].reshape(-1, n_ck, ck).transpose(1, 0, 2).astype(BF16),
        w_up=w_up[l].reshape(-1, n_ck, ck).transpose(1, 0, 2).astype(BF16),
        w_down=w_down[l].reshape(n_ck, ck, -1).astype(BF16))


def _shifted(u, hist):
    up = jnp.concatenate([hist, u], axis=1)
    t = u.shape[1]
    return up[:, 1:1 + t], up[:, 0:t]


def _chan_major(a):
    a = jnp.moveaxis(a, -3, -1)
    return a.reshape(a.shape[:-3] + (D_KV, a.shape[-1]))


def _token_major(a_t):
    a = a_t.reshape(a_t.shape[:-2] + (N_KV, HEAD_DIM, a_t.shape[-1]))
    return jnp.moveaxis(a, -1, -3)


def kernel(x_prompt, x_sample, state_conv, cache_cmp_k, cache_cmp_v, cache_slc_k, cache_slc_v,
           cache_win_k, cache_win_v, page_table, rel_bias, norm_mix, w_in, conv_w, q_norm, k_norm,
           cmp_pe, cmp_w1, cmp_w2, out_norm_conv, out_norm_att, w_out, norm_ffn, w_gate, w_up, w_down):
    nb, t, dm = x_prompt.shape
    ns, t_new, _ = x_sample.shape
    depth = w_in.shape[0]
    d_conv = conv_w.shape[2]
    d_att = dm - d_conv
    n_pool = cache_cmp_k.shape[1]
    n_pages = page_table.shape[1]
    pos0 = n_pages * PAGE_SIZE
    n_win = cache_win_k.shape[2]
    assert t % TQ == 0 and d_att == N_HEADS * HEAD_DIM and n_pages % PAGES_PER_STEP == 0
    assert n_win == WINDOW and WINDOW == 4 * TQ and pos0 >= WINDOW and t_new <= LANES and t_new & (t_new - 1) == 0
    assert t // CMP_STRIDE == LANES and SEL_BLOCK == 64 and t >= WINDOW and t // SEL_BLOCK < LANES - 1

    tm_p = 512
    kv = lambda a, b_, t_: a.reshape(b_, t_, N_KV, HEAD_DIM)
    ptab = _prompt_tables(rel_bias, t)
    stab = _sample_tables(rel_bias, t_new, pos0, n_win, LANES)
    chan_pool = lambda c: _chan_major(c).reshape(depth * n_pool, D_KV, PAGE_SIZE)
    cmp_pools = [chan_pool(cache_cmp_k), chan_pool(cache_cmp_v)]
    slc_pools = [chan_pool(cache_slc_k), chan_pool(cache_slc_v)]
    win_t = [_chan_major(cache_win_k), _chan_major(cache_win_v)]
    new_t = lambda a: jnp.pad(a.reshape(ns, t_new, LANES).transpose(0, 2, 1), ((0, 0), (0, 0), (0, LANES - t_new)))
    eye = jnp.eye(N_KV, dtype=BF16)[None, None, :, None, :, None]
    other = (jnp.arange(LANES)[None, :] // HEAD_DIM != jnp.arange(N_KV)[:, None])[None, :, :, None]

    xp = x_prompt.reshape(nb * t, dm)
    xs = x_sample.reshape(ns * t_new, dm)
    st_p, st_s = [], []
    for l in range(depth):
        lw = _layer_weights(l, norm_mix, w_in, conv_w, q_norm, k_norm, cmp_pe, cmp_w1, cmp_w2, out_norm_conv,
                            out_norm_att, w_out, norm_ffn, w_gate, w_up, w_down, d_conv, d_att, tm_p)

        gb, u, kc, vc, ks, kw, q_t, vs_t, vw_t, gates_t = _in_proj_t(
            xp, lw["norm_mix"], lw["w_tok"], lw["w_chan"], lw["q_gain_t"], lw["k_gain"], nb, d_conv, d_att, tm_p)
        seq = lambda a: a.reshape(nb, t, LANES)
        kcmp = _compress_rows(seq(kc), lw["cmp_k"], True, False)
        vcmp_t = _compress_rows(seq(vc), lw["cmp_v"], False, True)
        ks_aug = jnp.concatenate(
            [jnp.pad(seq(ks).astype(BF16), ((0, 0), (TQ, 0), (0, 0))),
             jnp.broadcast_to(ptab["sel_onehot"][None], (nb, TQ + t, LANES))], axis=2)
        kw_aug = jnp.concatenate(
            [jnp.pad(seq(kw).astype(BF16), ((0, 0), (WINDOW, 0), (0, 0))),
             jnp.broadcast_to(ptab["win_flag"][None], (nb, WINDOW + t, LANES))], axis=2)
        ones_rows = lambda v_t, front: jnp.where(
            other, 1.0, jnp.pad(v_t, ((0, 0), (0, 0), (front, 0)))[:, None]).astype(BF16)
        o_t = _pattn(q_t, gates_t, kcmp, vcmp_t, ks_aug, ones_rows(vs_t, TQ), kw_aug, ones_rows(vw_t, WINDOW), ptab)
        o_att = o_t.transpose(0, 3, 1, 2).reshape(nb * t, d_att)
        u3 = u.reshape(nb, t, d_conv)
        u1, u2 = _shifted(u3, jnp.zeros((nb, CONV_WIDTH - 1, d_conv), F32))
        xp = _mix_ffn(xp, gb, u, u1.reshape(nb * t, d_conv), u2.reshape(nb * t, d_conv), o_att, lw, tm_p)
        n_keep = min(WINDOW, t)
        st_p.append((u3[:, -(CONV_WIDTH - 1):], kv(kc, nb, t), kv(vc, nb, t), kv(ks, nb, t), _token_major(vs_t),
                     kv(kw, nb, t)[:, -n_keep:], _token_major(vw_t[:, :, -n_keep:])))

        gb, u, q, kc, vc, ks, vs, kw, vw, gates = _in_proj(
            xs, lw["norm_mix"], lw["w_in"], lw["q_gain"], lw["k_gain"], d_conv, d_att, ns * t_new)
        base = l * n_pool
        kcmp = _compress_paged(cmp_pools[0], page_table, base, lw["cmp_k"], True)
        vcmp = _compress_paged(cmp_pools[1], page_table, base, lw["cmp_v"], False)
        qr = q.reshape(ns, t_new, GROUP, N_KV, HEAD_DIM).transpose(0, 2, 3, 1, 4)
        qm = (qr[:, :, :, :, None, :] * eye).reshape(ns, N_HEADS * t_new, LANES)
        gr = gates[:, :3 * N_HEADS].reshape(ns, t_new, 3, N_KV, GROUP).transpose(0, 2, 4, 3, 1)
        gr = jnp.broadcast_to(gr.reshape(ns, 3, N_HEADS * t_new, 1), (ns, 3, N_HEADS * t_new, LANES))
        kw_t, vw_t = new_t(kw), new_t(vw)
        o = _sattn(qm, kcmp, vcmp, win_t[0][l], win_t[1][l], kw_t, vw_t, new_t(ks), new_t(vs),
                   slc_pools[0], slc_pools[1], page_table, base, gr, stab, t_new, pos0)
        o6 = o.reshape(ns, GROUP, N_KV, t_new, N_KV, HEAD_DIM)
        o_att = jnp.stack([o6[:, :, g, :, g] for g in range(N_KV)], axis=2)
        o_att = o_att.transpose(0, 3, 1, 2, 4).reshape(ns * t_new, d_att)
        u3 = u.reshape(ns, t_new, d_conv)
        u1, u2 = _shifted(u3, state_conv[l])
        xs = _mix_ffn(xs, gb, u, u1.reshape(-1, d_conv), u2.reshape(-1, d_conv), o_att, lw, ns * t_new)
        new_conv = jnp.concatenate([state_conv[l], u3], axis=1)[:, -(CONV_WIDTH - 1):]
        n_keep = min(WINDOW, n_win + t_new)
        keep = lambda c_t, n_t: _token_major(jnp.concatenate([c_t, n_t[:, :, :t_new]], axis=2)[:, :, -n_keep:])
        st_s.append((new_conv, kv(kc, ns, t_new), kv(vc, ns, t_new), kv(ks, ns, t_new), kv(vs, ns, t_new),
                     keep(win_t[0][l], kw_t), keep(win_t[1][l], vw_t)))

    outs_p = [jnp.stack(a) for a in zip(*st_p)]
    outs_s = [jnp.stack(a) for a in zip(*st_s)]
    return (xp.reshape(nb, t, dm), xs.reshape(ns, t_new, dm), *outs_p, *outs_s)
```

```python
import functools
import math

import jax
import jax.numpy as jnp
import numpy as np
from jax import lax
from jax.experimental import pallas as pl
from jax.experimental.pallas import tpu as pltpu

F32 = jnp.float32
BF16 = jnp.bfloat16

HEAD_DIM = 64
N_KV = 2
GROUP = 4
N_HEADS = N_KV * GROUP
D_KV = N_KV * HEAD_DIM
CMP_LEN = 32
CMP_STRIDE = 16
CMP_HID = 2 * HEAD_DIM
SEL_BLOCK = 64
SEL_TOPK = 16
WINDOW = 512
NUM_BUCKETS = 32
MAX_DISTANCE = 128
PAGE_SIZE = 128
CONV_WIDTH = 3
EPS = 1e-6
SCALE = HEAD_DIM ** -0.5
LOG2E = math.log2(math.e)
NEG = -1e30

LANES = 128
TQ = 128
GATE_ROWS = 32
CMP_PAGES_PER_STEP = 64
PAGES_PER_STEP = 32
VMEM_LIMIT = 56 * 1024 * 1024

_NT = (((1,), (1,)), ((), ()))


def _dot(a, b):
    return jnp.dot(a, b, preferred_element_type=F32)


def _dot_nt(a, b):
    return lax.dot_general(a, b, _NT, preferred_element_type=F32)


def _params(n_axes):
    return pltpu.CompilerParams(dimension_semantics=("arbitrary",) * n_axes,
                                vmem_limit_bytes=VMEM_LIMIT)


def _resident(shape):
    nd = len(shape)
    return pl.BlockSpec(shape, lambda *_: (0,) * nd, pipeline_mode=pl.Buffered(1))


def _rms(x, gain):
    return x * lax.rsqrt(jnp.mean(x * x, axis=-1, keepdims=True) + EPS) * gain


def _pair_rms(y, gain, lo):
    y2 = y * y
    s_lo = jnp.sum(jnp.where(lo, y2, 0.0), axis=-1, keepdims=True)
    s_hi = jnp.sum(jnp.where(lo, 0.0, y2), axis=-1, keepdims=True)
    inv = jnp.where(lo, lax.rsqrt(s_lo / HEAD_DIM + EPS), lax.rsqrt(s_hi / HEAD_DIM + EPS))
    return y * inv * gain


def _in_proj_kernel(x_ref, nm_ref, w_ref, qg_ref, kg_ref,
                    gb_ref, u_ref, q_ref, kc_ref, vc_ref, ks_ref, vs_ref, kw_ref, vw_ref, gt_ref):
    x = x_ref[...]
    hb = _rms(x, nm_ref[...]).astype(BF16)
    d = gb_ref.shape[1]

    def seg(a, b):
        return _dot(hb, w_ref[:, a:b])

    gb_ref[...] = seg(0, d)
    u_ref[...] = seg(d, 2 * d) * seg(2 * d, 3 * d)
    lo = lax.broadcasted_iota(jnp.int32, (x.shape[0], LANES), 1) < HEAD_DIM
    c = 3 * d
    for j in range(q_ref.shape[1] // LANES):
        q_ref[:, j * LANES:(j + 1) * LANES] = _pair_rms(
            seg(c + j * LANES, c + (j + 1) * LANES), qg_ref[...], lo).astype(BF16)
    c += q_ref.shape[1]
    kc_ref[...] = seg(c, c + LANES)
    vc_ref[...] = seg(c + LANES, c + 2 * LANES)
    ks_ref[...] = _pair_rms(seg(c + 2 * LANES, c + 3 * LANES), kg_ref[0:1, :], lo)
    vs_ref[...] = seg(c + 3 * LANES, c + 4 * LANES)
    kw_ref[...] = _pair_rms(seg(c + 4 * LANES, c + 5 * LANES), kg_ref[1:2, :], lo)
    vw_ref[...] = seg(c + 5 * LANES, c + 6 * LANES)
    gt_ref[...] = jax.nn.sigmoid(seg(c + 6 * LANES, c + 7 * LANES))


def _in_proj(x, nm, w, qg, kg, d_conv, d_att, tm):
    n, dm = x.shape
    row = lambda w_: pl.BlockSpec((tm, w_), lambda i: (i, 0))
    widths = [d_conv, d_conv, d_att] + [LANES] * 7
    dtypes = [F32, F32, BF16] + [F32] * 7
    return pl.pallas_call(
        _in_proj_kernel,
        grid=(n // tm,),
        in_specs=[row(dm), _resident(nm.shape), _resident(w.shape), _resident(qg.shape), _resident(kg.shape)],
        out_specs=[row(w_) for w_ in widths],
        out_shape=[jax.ShapeDtypeStruct((n, w_), dt) for w_, dt in zip(widths, dtypes)],
        compiler_params=_params(1),
        name="in_proj",
    )(x, nm, w, qg, kg)


def _in_proj_t_kernel(x_ref, nm_ref, w_ref, wt_ref, qgt_ref, kg_ref,
                      gb_ref, u_ref, kc_ref, vc_ref, ks_ref, kw_ref, qt_ref, vst_ref, vwt_ref, gtt_ref):
    x = x_ref[...]
    hb = _rms(x, nm_ref[...]).astype(BF16)
    d = gb_ref.shape[1]

    def seg(a, b):
        return _dot(hb, w_ref[:, a:b])

    gb_ref[...] = seg(0, d)
    u_ref[...] = seg(d, 2 * d) * seg(2 * d, 3 * d)
    lo = lax.broadcasted_iota(jnp.int32, (x.shape[0], LANES), 1) < HEAD_DIM
    c = 3 * d
    kc_ref[...] = seg(c, c + LANES)
    vc_ref[...] = seg(c + LANES, c + 2 * LANES)
    ks_ref[...] = _pair_rms(seg(c + 2 * LANES, c + 3 * LANES), kg_ref[0:1, :], lo)
    kw_ref[...] = _pair_rms(seg(c + 3 * LANES, c + 4 * LANES), kg_ref[1:2, :], lo)

    yt = _dot_nt(wt_ref[...], hb)
    d_att = qt_ref.shape[1]
    for h in range(d_att // HEAD_DIM):
        y = yt[h * HEAD_DIM:(h + 1) * HEAD_DIM]
        inv = lax.rsqrt(jnp.mean(y * y, axis=0, keepdims=True) + EPS)
        qt_ref[0, h * HEAD_DIM:(h + 1) * HEAD_DIM, :] = (y * inv * qgt_ref[...]).astype(BF16)
    vst_ref[0] = yt[d_att:d_att + LANES]
    vwt_ref[0] = yt[d_att + LANES:d_att + 2 * LANES]
    gtt_ref[0] = jax.nn.sigmoid(yt[d_att + 2 * LANES:d_att + 2 * LANES + GATE_ROWS])


def _in_proj_t(x, nm, w, wt, qgt, kg, nb, d_conv, d_att, tm):
    n, dm = x.shape
    t = n // nb
    per_b = t // tm
    row = lambda w_: pl.BlockSpec((tm, w_), lambda i: (i, 0))
    col = lambda r: pl.BlockSpec((1, r, tm), lambda i: (i // per_b, 0, i % per_b))
    widths = [d_conv, d_conv] + [LANES] * 4
    rows_t = [d_att, LANES, LANES, GATE_ROWS]
    dt_t = [BF16, F32, F32, F32]
    return pl.pallas_call(
        _in_proj_t_kernel,
        grid=(n // tm,),
        in_specs=[row(dm), _resident(nm.shape), _resident(w.shape), _resident(wt.shape),
                  _resident(qgt.shape), _resident(kg.shape)],
        out_specs=[row(w_) for w_ in widths] + [col(r) for r in rows_t],
        out_shape=[jax.ShapeDtypeStruct((n, w_), F32) for w_ in widths]
        + [jax.ShapeDtypeStruct((nb, r, t), dt) for r, dt in zip(rows_t, dt_t)],
        compiler_params=_params(1),
        name="in_proj_t",
    )(x, nm, w, wt, qgt, kg)


def _cmp_ab_kernel(x_ref, w_ref, a_ref, b_ref):
    n_chunks = x_ref.shape[1] // CMP_STRIDE
    cols = [x_ref[0, pl.ds(l, n_chunks, stride=CMP_STRIDE), :] for l in range(CMP_STRIDE)]
    xr = jnp.concatenate(cols, axis=1).astype(BF16)
    ab = _dot(xr, w_ref[...])
    half = ab.shape[1] // 2
    a_ref[0] = ab[:, :half]
    b_ref[0] = ab[:, half:]


def _cmp_ab_paged_kernel(pt_ref, *refs, n_in):
    x_refs = refs[:n_in]
    perm_ref, w_ref, a_ref, b_ref = refs[n_in:]
    per_page = PAGE_SIZE // CMP_STRIDE
    rows = []
    for xr in x_refs:
        xp = _dot_nt(perm_ref[...], xr[0].astype(BF16))
        rows.append(jnp.concatenate([xp[l * per_page:(l + 1) * per_page] for l in range(CMP_STRIDE)], axis=1))
    xr = jnp.concatenate(rows, axis=0).astype(BF16)
    ab = _dot(xr, w_ref[...])
    half = ab.shape[1] // 2
    a_ref[0] = ab[:, :half]
    b_ref[0] = ab[:, half:]


def _cmp_fin_kernel(a_ref, b_ref, pe_ref, w1_ref, w2_ref, g_ref, o_ref, *, n_valid, normalize, transpose_out):
    a = a_ref[0]
    bm = b_ref[0]
    n_chunks = a.shape[0]
    b_next = pltpu.roll(bm, n_chunks - 1, 0)
    cpe = jnp.dot(pe_ref[...], w1_ref[...], preferred_element_type=F32,
                  precision=lax.Precision.HIGHEST)[0:1]
    x = a + b_next + jnp.concatenate([cpe, cpe], axis=1)
    hid = x * (0.5 * (1.0 + jnp.tanh(math.sqrt(2.0 / math.pi) * (x + 0.044715 * (x * x * x)))))
    out = _dot(hid.astype(BF16), w2_ref[...])
    if normalize:
        lo = lax.broadcasted_iota(jnp.int32, out.shape, 1) < HEAD_DIM
        out = _pair_rms(out, g_ref[...], lo)
    rows = lax.broadcasted_iota(jnp.int32, out.shape, 0)
    out = jnp.where(rows < n_valid, out, 0.0)
    o_ref[0] = out.T if transpose_out else out


def _cmp_fin(a, b, cw, n_valid, normalize, transpose_out=False):
    nb, n_chunks, wid = a.shape
    blk = pl.BlockSpec((1, n_chunks, wid), lambda i: (i, 0, 0))
    oshape = (LANES, n_chunks) if transpose_out else (n_chunks, LANES)
    return pl.pallas_call(
        functools.partial(_cmp_fin_kernel, n_valid=n_valid, normalize=normalize, transpose_out=transpose_out),
        grid=(nb,),
        in_specs=[blk, blk, _resident(cw["pe"].shape), _resident(cw["w1f"].shape),
                  _resident(cw["w2bd"].shape), _resident(cw["gain"].shape)],
        out_specs=pl.BlockSpec((1,) + oshape, lambda i: (i, 0, 0)),
        out_shape=jax.ShapeDtypeStruct((nb,) + oshape, F32),
        compiler_params=_params(1),
        name="cmp_fin",
    )(a, b, cw["pe"], cw["w1f"], cw["w2bd"], cw["gain"])


def _compress_rows(rows, cw, normalize, transpose_out):
    nb, t, _ = rows.shape
    n_chunks = t // CMP_STRIDE
    wid = cw["wab"].shape[1] // 2
    out_blk = pl.BlockSpec((1, n_chunks, wid), lambda i: (i, 0, 0))
    a, b = pl.pallas_call(
        _cmp_ab_kernel,
        grid=(nb,),
        in_specs=[pl.BlockSpec((1, t, LANES), lambda i: (i, 0, 0)), _resident(cw["wab"].shape)],
        out_specs=[out_blk, out_blk],
        out_shape=[jax.ShapeDtypeStruct((nb, n_chunks, wid), F32)] * 2,
        compiler_params=_params(1),
        name="cmp_ab_rows",
    )(rows, cw["wab"])
    return _cmp_fin(a, b, cw, (t - CMP_LEN) // CMP_STRIDE + 1, normalize, transpose_out)


def _page_spec(i, base, pages_per_step):
    return pl.BlockSpec((1, PAGE_SIZE, LANES),
                        lambda b, s, pt: (base + pt[b, s * pages_per_step + i], 0, 0))


def _compress_paged(cache, page_table, base, cw, normalize):
    nb, n_pages = page_table.shape
    pps = CMP_PAGES_PER_STEP
    per_page = PAGE_SIZE // CMP_STRIDE
    ch = pps * per_page
    wid = cw["wab"].shape[1] // 2
    n_chunks = n_pages * per_page
    out_blk = pl.BlockSpec((1, ch, wid), lambda b, s, pt: (b, s, 0))
    r = np.arange(PAGE_SIZE)
    perm = jnp.asarray(r[None, :] == (r % per_page)[:, None] * CMP_STRIDE + (r // per_page)[:, None], BF16)
    cst = lambda a: pl.BlockSpec(a.shape, lambda b, s, pt: (0, 0), pipeline_mode=pl.Buffered(1))
    a, b = pl.pallas_call(
        functools.partial(_cmp_ab_paged_kernel, n_in=pps),
        grid_spec=pltpu.PrefetchScalarGridSpec(
            num_scalar_prefetch=1,
            grid=(nb, n_pages // pps),
            in_specs=[_page_spec(i, base, pps) for i in range(pps)] + [cst(perm), cst(cw["wab"])],
            out_specs=[out_blk, out_blk]),
        out_shape=[jax.ShapeDtypeStruct((nb, n_chunks, wid), F32)] * 2,
        compiler_params=_params(2),
        name="cmp_ab_paged",
    )(page_table, *([cache] * pps), perm, cw["wab"])
    return _cmp_fin(a, b, cw, n_chunks - 1, normalize)


def _split_bf16(x):
    hi = x.astype(BF16)
    return hi, (x - hi.astype(F32)).astype(BF16)


def _online_update(state, s, mask, v_t):
    m, l, acc = state
    s = jnp.where(mask, s, NEG)
    m_new = jnp.maximum(m, jnp.max(s, axis=-1, keepdims=True))
    alpha = jnp.exp2(m - m_new)
    e = jnp.where(mask, jnp.exp2(s - m_new), 0.0)
    l = alpha * l + jnp.sum(e, axis=-1, keepdims=True)
    acc = alpha * acc + _dot_nt(e.astype(BF16), v_t)
    return m_new, l, acc


def _finish(state):
    _, l, acc = state
    return acc / jnp.where(l > 0, l, 1.0)


def _reset_t(m_ref, acc_ref):
    m_ref[...] = jnp.full(m_ref.shape, NEG, F32)
    acc_ref[...] = jnp.zeros(acc_ref.shape, F32)


def _update_t(m_ref, acc_ref, s_t, v_t):
    m = m_ref[...]
    m_new = jnp.maximum(m, jnp.max(s_t, axis=0, keepdims=True))
    alpha = jnp.exp2(m - m_new)
    e = jnp.exp2(s_t - m_new).astype(BF16)
    acc_ref[...] = alpha * acc_ref[...] + _dot(v_t, e)
    m_ref[...] = m_new


def _pattn_kernel(qt_ref, gt_ref, kcmp_ref, vcmpt_ref, ksa_ref, vst_ref, kwa_ref, vwt_ref,
                  bcmp_ref, stab_ref, wa_ref, wb_ref, diag_ref, covt_ref, o_ref, m_ref, acc_ref, qa_ref, out_ref, *, n_sel, topk):
    qi = pl.program_id(1)
    q0 = qi * TQ
    n_slabs = qt_ref.shape[1] // LANES
    wide = n_slabs * TQ
    sub = lax.broadcasted_iota(jnp.int32, (LANES, TQ), 0)
    lane = lax.broadcasted_iota(jnp.int32, (LANES, TQ), 1)
    sub_w = lax.broadcasted_iota(jnp.int32, (LANES, wide), 0)
    tpos_w = q0 + (lax.broadcasted_iota(jnp.int32, (LANES, wide), 1) & (TQ - 1))
    kcmp = kcmp_ref[0].astype(BF16)
    vcmp_t = vcmpt_ref[0].astype(BF16)
    covt = covt_ref[...]
    gt = gt_ref[0]
    zero_rows = jnp.zeros((LANES - n_sel, TQ), F32)
    def gate(br, g):
        return jnp.concatenate([gt[br * N_HEADS + g * GROUP + j:br * N_HEADS + g * GROUP + j + 1, :]
                                for j in range(n_slabs)], axis=1)

    for g in range(N_KV):
        mine = (sub < HEAD_DIM) if g == 0 else (sub >= HEAD_DIM)
        q_t = jnp.concatenate(
            [jnp.where(mine, qt_ref[0, j * LANES:(j + 1) * LANES, :], jnp.zeros((), BF16))
             for j in range(n_slabs)], axis=1)

        s = _dot(kcmp, q_t) + bcmp_ref[g, 0]
        mask = sub_w * CMP_STRIDE + (CMP_LEN - 1) <= tpos_w
        s = jnp.where(mask, s, NEG)
        e = jnp.where(mask, jnp.exp2(s - jnp.max(s, axis=0, keepdims=True)), 0.0)
        l = jnp.sum(e, axis=0, keepdims=True)
        p = e * (1.0 / jnp.where(l > 0, l, 1.0))
        out_ref[g] = gate(0, g) * _dot(vcmp_t, p.astype(BF16))
        p_sum = p[:, 0:TQ]
        for j in range(1, n_slabs):
            p_sum = p_sum + p[:, j * TQ:(j + 1) * TQ]

        hi, lo_ = _split_bf16(p_sum)
        sc = _dot(covt, hi) + _dot(covt, lo_)
        tq_pos = q0 + lane
        cur = tq_pos >> 6
        forced = (sub == 0) | (sub == cur) | (sub == cur - 1)
        sc = jnp.where(forced, 1e6, jnp.where(sub * SEL_BLOCK > tq_pos, -1e6, sc))
        sc = sc[0:n_sel]
        blk = sub[0:n_sel]
        rank = jnp.zeros((n_sel, TQ), F32)
        for i in range(n_sel):
            si = sc[i:i + 1, :]
            beats = (si > sc) | ((si == sc) & (blk > i))
            rank = rank + jnp.where(beats, 1.0, 0.0)
        pen = jnp.concatenate([jnp.where(rank < topk, 0.0, NEG), zero_rows], axis=0)
        pen = jnp.where(sub == LANES - 1, NEG, pen).astype(BF16)
        qa_ref[g, 0:LANES, :] = q_t
        qa_ref[g, LANES:2 * LANES, :] = jnp.concatenate([pen] * n_slabs, axis=1)

    def keys(k_ref, v_ref, g, kt, n):
        start = pl.multiple_of(kt * TQ, TQ)
        return k_ref[0, pl.ds(start, n * TQ), :], v_ref[0, g, :, pl.ds(start, n * TQ)]

    def add_branch(br):
        for g in range(N_KV):
            acc = acc_ref[g]
            den = acc[HEAD_DIM:HEAD_DIM + 1] if g == 0 else acc[0:1]
            out_ref[g] += (gate(br, g) * (1.0 / den)) * acc

    _reset_t(m_ref, acc_ref)
    n_far = jnp.maximum(qi - 1, 0)

    @pl.loop(0, n_far // 2)
    def _(i):
        for g in range(N_KV):
            k_, v_ = keys(ksa_ref, vst_ref, g, 1 + 2 * i, 2)
            _update_t(m_ref.at[g], acc_ref.at[g], _dot(k_, qa_ref[g]), v_)

    odd = jnp.where(n_far % 2 == 1, n_far, 0)
    for g in range(N_KV):
        k0, v0 = keys(ksa_ref, vst_ref, g, odd, 1)
        k1, v1 = keys(ksa_ref, vst_ref, g, qi, 2)
        s_ = _dot(jnp.concatenate([k0, k1], axis=0), qa_ref[g]) + stab_ref[g]
        _update_t(m_ref.at[g], acc_ref.at[g], s_, jnp.concatenate([v0, v1], axis=1))
    add_branch(1)

    _reset_t(m_ref, acc_ref)
    pad_pen = jnp.where(sub_w == 0, NEG, 0.0).astype(BF16)
    for g in range(N_KV):
        q_win = jnp.concatenate([qa_ref[g, 0:LANES, :], pad_pen], axis=0)
        k_, v_ = keys(kwa_ref, vwt_ref, g, qi, 2)
        _update_t(m_ref.at[g], acc_ref.at[g], _dot(k_, q_win) + wa_ref[...], v_)
        k_, v_ = keys(kwa_ref, vwt_ref, g, qi + 2, 2)
        _update_t(m_ref.at[g], acc_ref.at[g], _dot(k_, q_win) + wb_ref[g], v_)
        k_, v_ = keys(kwa_ref, vwt_ref, g, qi + 4, 1)
        _update_t(m_ref.at[g], acc_ref.at[g], _dot(k_, q_win) + diag_ref[g], v_)
    add_branch(2)

    out = jnp.where(sub_w < HEAD_DIM, out_ref[0], out_ref[1])
    for j in range(n_slabs):
        o_ref[0, j] = out[:, j * TQ:(j + 1) * TQ]


def _pattn(q_t, gates_t, kcmp, vcmp_t, ks_aug, vs_t, kw_aug, vw_t, tabs):
    nb, d_att, t = q_t.shape
    n_q = t // TQ
    n_sel = -(-t // SEL_BLOCK)
    n_slabs = d_att // LANES
    whole = lambda a: pl.BlockSpec((1,) + a.shape[1:], lambda b, i: (b,) + (0,) * (a.ndim - 1))
    return pl.pallas_call(
        functools.partial(_pattn_kernel, n_sel=n_sel, topk=min(SEL_TOPK, n_sel)),
        grid=(nb, n_q),
        in_specs=[pl.BlockSpec((1, d_att, TQ), lambda b, i: (b, 0, i)),
                  pl.BlockSpec((1, GATE_ROWS, TQ), lambda b, i: (b, 0, i)),
                  whole(kcmp), whole(vcmp_t), whole(ks_aug), whole(vs_t), whole(kw_aug), whole(vw_t),
                  pl.BlockSpec((N_KV, 1, LANES, n_slabs * TQ), lambda b, i: (0, i, 0, 0)),
                  _resident(tabs["sel"].shape), _resident(tabs["win_a"].shape), _resident(tabs["win_b"].shape),
                  _resident(tabs["diag"].shape), _resident(tabs["covt"].shape)],
        out_specs=pl.BlockSpec((1, n_slabs, LANES, TQ), lambda b, i: (b, 0, 0, i)),
        out_shape=jax.ShapeDtypeStruct((nb, n_slabs, LANES, t), F32),
        scratch_shapes=[pltpu.VMEM((N_KV, 1, n_slabs * TQ), F32), pltpu.VMEM((N_KV, LANES, n_slabs * TQ), F32),
                        pltpu.VMEM((N_KV, 2 * LANES, n_slabs * TQ), BF16), pltpu.VMEM((N_KV, LANES, n_slabs * TQ), F32)],
        compiler_params=_params(2),
        name="pattn",
    )(q_t, gates_t, kcmp, vcmp_t, ks_aug, vs_t, kw_aug, vw_t, tabs["bcmp"], tabs["sel"], tabs["win_a"],
      tabs["win_b"], tabs["diag"], tabs["covt"])


def _sattn1_kernel(qm_ref, kcmp_ref, vcmp_ref, kwc_ref, vwc_ref, kwn_ref, vwn_ref,
                   bcmp_ref, bwc_ref, bwn_ref, cov_ref, ocmp_ref, owin_ref, sel_ref,
                   *, n_cmp, n_sel, topk, t_new, pos0, win_start):
    qm = qm_ref[0]
    n_rows = qm.shape[0]
    n_gt = N_KV * t_new
    s = _dot_nt(qm, kcmp_ref[0].astype(BF16)) + bcmp_ref[...]
    ncol = lax.broadcasted_iota(jnp.int32, s.shape, 1)
    trow = pos0 + (lax.broadcasted_iota(jnp.int32, s.shape, 0) & (t_new - 1))
    mask = (ncol < n_cmp) & (ncol * CMP_STRIDE + (CMP_LEN - 1) <= trow)
    s = jnp.where(mask, s, NEG)
    e = jnp.where(mask, jnp.exp2(s - jnp.max(s, axis=-1, keepdims=True)), 0.0)
    l = jnp.sum(e, axis=-1, keepdims=True)
    p = e / jnp.where(l > 0, l, 1.0)
    ocmp_ref[0] = _dot(p.astype(BF16), vcmp_ref[0].astype(BF16))
    p_sum = p[0:n_gt]
    for j in range(1, n_rows // n_gt):
        p_sum = p_sum + p[j * n_gt:(j + 1) * n_gt]
    hi, lo = _split_bf16(p_sum)
    sc = _dot(hi, cov_ref[...]) + _dot(lo, cov_ref[...])
    blk = lax.broadcasted_iota(jnp.int32, sc.shape, 1)
    qpos = pos0 + (lax.broadcasted_iota(jnp.int32, sc.shape, 0) & (t_new - 1))
    cur = qpos >> 6
    forced = (blk == 0) | (blk == cur) | (blk == cur - 1)
    sc = jnp.where(forced, 1e6, jnp.where(blk * SEL_BLOCK > qpos, -1e6, sc))
    sc = jnp.where(blk < n_sel, sc, -3e38)
    rank = jnp.zeros(sc.shape, F32)
    for i in range(n_sel):
        si = sc[:, i:i + 1]
        beats = (si > sc) | ((si == sc) & (blk > i))
        rank = rank + jnp.where(beats, 1.0, 0.0)
    sel_ref[0] = jnp.where((rank < topk) & (blk < n_sel), 1.0, 0.0)

    init = (jnp.full((n_rows, 1), NEG, F32), jnp.zeros((n_rows, 1), F32), jnp.zeros((n_rows, LANES), F32))
    s = _dot(qm, kwc_ref[0].astype(BF16)) + bwc_ref[...]
    trow = pos0 + (lax.broadcasted_iota(jnp.int32, s.shape, 0) & (t_new - 1))
    dist = trow - (win_start + lax.broadcasted_iota(jnp.int32, s.shape, 1))
    st = _online_update(init, s, (dist >= 0) & (dist < WINDOW), vwc_ref[0].astype(BF16))
    s = _dot(qm, kwn_ref[0].astype(BF16)) + bwn_ref[...]
    trow = pos0 + (lax.broadcasted_iota(jnp.int32, s.shape, 0) & (t_new - 1))
    col = lax.broadcasted_iota(jnp.int32, s.shape, 1)
    dist = trow - (pos0 + col)
    st = _online_update(st, s, (dist >= 0) & (dist < WINDOW) & (col < t_new), vwn_ref[0].astype(BF16))
    owin_ref[0] = _finish(st)


def _sattn2_kernel(pt_ref, *refs, pps, t_new):
    k_refs = refs[:pps]
    v_refs = refs[pps:2 * pps]
    (qm_ref, selc_ref, seln_ref, ksn_ref, vsn_ref, bsel_ref, bseln_ref, exp_ref,
     ocmp_ref, owin_ref, gate_ref, o_ref, m_sc, l_sc, acc_sc) = refs[2 * pps:]
    step = pl.program_id(1)
    qm = qm_ref[0]
    n_rows = qm.shape[0]
    reps = n_rows // selc_ref.shape[2]

    @pl.when(step == 0)
    def _():
        s = _dot(qm, ksn_ref[0].astype(BF16)) + bseln_ref[...]
        trow = lax.broadcasted_iota(jnp.int32, s.shape, 0) & (t_new - 1)
        col = lax.broadcasted_iota(jnp.int32, s.shape, 1)
        hit = jnp.concatenate([seln_ref[0]] * reps, axis=0) > 0.5
        init = (jnp.full((n_rows, 1), NEG, F32), jnp.zeros((n_rows, 1), F32),
                jnp.zeros((n_rows, LANES), F32))
        m, l, acc = _online_update(init, s, hit & (col <= trow) & (col < t_new), vsn_ref[0].astype(BF16))
        m_sc[...] = m
        l_sc[...] = l
        acc_sc[...] = acc

    k_t = jnp.concatenate([r[0] for r in k_refs], axis=1).astype(BF16)
    v_t = jnp.concatenate([r[0] for r in v_refs], axis=1).astype(BF16)
    s = _dot(qm, k_t) + bsel_ref[...]
    hit = _dot(selc_ref[0, 0].astype(BF16), exp_ref[...])
    mask = jnp.concatenate([hit] * reps, axis=0) > 0.5
    m, l, acc = _online_update((m_sc[...], l_sc[...], acc_sc[...]), s, mask, v_t)
    m_sc[...] = m
    l_sc[...] = l
    acc_sc[...] = acc

    @pl.when(step == pl.num_programs(1) - 1)
    def _():
        o_sel = _finish((m, l, acc))
        o_ref[0] = gate_ref[0, 0] * ocmp_ref[0] + gate_ref[0, 1] * o_sel + gate_ref[0, 2] * owin_ref[0]


def _sattn(qm, kcmp, vcmp, kwc_t, vwc_t, kwn_t, vwn_t, ksn_t, vsn_t, cache_k, cache_v, page_table, base,
           gates, tabs, t_new, pos0):
    nb, n_rows, _ = qm.shape
    n_pages = page_table.shape[1]
    n_chunks = kcmp.shape[1]
    n_gt = N_KV * t_new
    n_sel = -(-(pos0 + t_new) // SEL_BLOCK)
    selw = tabs["cov"].shape[1]
    per_b = lambda shape: pl.BlockSpec((1,) + shape, lambda b: (b,) + (0,) * len(shape))
    ocmp, owin, sel = pl.pallas_call(
        functools.partial(_sattn1_kernel, n_cmp=n_chunks - 1, n_sel=n_sel, topk=min(SEL_TOPK, n_sel),
                          t_new=t_new, pos0=pos0, win_start=pos0 - kwc_t.shape[2]),
        grid=(nb,),
        in_specs=[per_b((n_rows, LANES)), per_b((n_chunks, LANES)), per_b((n_chunks, LANES)),
                  per_b(kwc_t.shape[1:]), per_b(vwc_t.shape[1:]), per_b(kwn_t.shape[1:]), per_b(vwn_t.shape[1:]),
                  _resident(tabs["bcmp"].shape), _resident(tabs["bwc"].shape), _resident(tabs["bwn"].shape),
                  _resident(tabs["cov"].shape)],
        out_specs=[per_b((n_rows, LANES)), per_b((n_rows, LANES)), per_b((n_gt, selw))],
        out_shape=[jax.ShapeDtypeStruct((nb, n_rows, LANES), F32), jax.ShapeDtypeStruct((nb, n_rows, LANES), F32),
                   jax.ShapeDtypeStruct((nb, n_gt, selw), F32)],
        compiler_params=_params(1),
        name="sattn1",
    )(qm, kcmp, vcmp, kwc_t, vwc_t, kwn_t, vwn_t, tabs["bcmp"], tabs["bwc"], tabs["bwn"], tabs["cov"])

    pps = PAGES_PER_STEP
    n_steps = n_pages // pps
    bps = pps * PAGE_SIZE // SEL_BLOCK
    n_cblk = n_pages * PAGE_SIZE // SEL_BLOCK
    selc = sel[:, :, :n_cblk].reshape(nb, n_gt, n_steps, bps).transpose(0, 2, 1, 3)
    selc = jnp.pad(selc, ((0, 0), (0, 0), (0, 0), (0, LANES - bps)))
    seln = jnp.broadcast_to(sel[:, :, n_cblk:n_cblk + 1], (nb, n_gt, LANES))

    keys = pps * PAGE_SIZE
    cst = lambda shape: pl.BlockSpec(shape, lambda b, s, pt: (0,) * len(shape), pipeline_mode=pl.Buffered(1))
    pb = lambda shape: pl.BlockSpec((1,) + shape, lambda b, s, pt: (b,) + (0,) * len(shape))
    return pl.pallas_call(
        functools.partial(_sattn2_kernel, pps=pps, t_new=t_new),
        grid_spec=pltpu.PrefetchScalarGridSpec(
            num_scalar_prefetch=1,
            grid=(nb, n_steps),
            in_specs=[_page_spec(i, base, pps) for i in range(pps)] * 2
            + [pb((n_rows, LANES)),
               pl.BlockSpec((1, 1, n_gt, LANES), lambda b, s, pt: (b, s, 0, 0)),
               pb((n_gt, LANES)), pb(ksn_t.shape[1:]), pb(vsn_t.shape[1:]),
               pl.BlockSpec((n_rows, keys), lambda b, s, pt: (0, s)),
               cst(tabs["bseln"].shape), cst(tabs["expand"].shape),
               pb((n_rows, LANES)), pb((n_rows, LANES)), pb((3, n_rows, LANES))],
            out_specs=pb((n_rows, LANES)),
            scratch_shapes=[pltpu.VMEM((n_rows, 1), F32), pltpu.VMEM((n_rows, 1), F32),
                            pltpu.VMEM((n_rows, LANES), F32)]),
        out_shape=jax.ShapeDtypeStruct((nb, n_rows, LANES), F32),
        compiler_params=_params(2),
        name="sattn2",
    )(page_table, *([cache_k] * pps), *([cache_v] * pps), qm, selc, seln, ksn_t, vsn_t,
      tabs["bsel"], tabs["bseln"], tabs["expand"], ocmp, owin, gates)


def _mix_ffn_kernel(x_ref, gb_ref, u_ref, u1_ref, u2_ref, oa_ref, cw_ref, nc_ref, na_ref, wo_ref,
                    nf_ref, wg_ref, wu_ref, wd_ref, y_ref, acc_ref, *, tiles_per_seq):
    d = gb_ref.shape[1]
    u = u_ref[...]
    if tiles_per_seq is None:
        u1, u2 = u1_ref[...], u2_ref[...]
    else:
        prev = u1_ref[...] * jnp.where(pl.program_id(0) % tiles_per_seq == 0, 0.0, 1.0)
        row = lax.broadcasted_iota(jnp.int32, u.shape, 0)
        u1 = jnp.where(row == 0, prev[7:8], pltpu.roll(u, 1, 0))
        u2 = jnp.where(row == 0, prev[6:7], jnp.where(row == 1, prev[7:8], pltpu.roll(u, 2, 0)))
    conv = u2 * cw_ref[0:1, :] + u1 * cw_ref[1:2, :] + u * cw_ref[2:3, :]
    yc = _rms(gb_ref[...] * conv, nc_ref[...]).astype(BF16)
    oa = _rms(oa_ref[...], na_ref[...]).astype(BF16)
    x1 = x_ref[...] + (_dot(yc, wo_ref[0:d, :]) + _dot(oa, wo_ref[d:, :]))
    h2 = _rms(x1, nf_ref[...]).astype(BF16)
    acc_ref[...] = jnp.zeros_like(acc_ref)
    for c in range(wg_ref.shape[0]):
        gate = _dot(h2, wg_ref[c])
        act = (gate * jax.nn.sigmoid(gate)) * _dot(h2, wu_ref[c])
        acc_ref[...] += _dot(act.astype(BF16), wd_ref[c])
    y_ref[...] = x1 + acc_ref[...]


def _mix_ffn(x, gb, u, u1, u2, oa, lw, tm, tiles_per_seq=None):
    n, dm = x.shape
    d = gb.shape[1]
    row = lambda w_: pl.BlockSpec((tm, w_), lambda i: (i, 0))
    hist = row(d) if tiles_per_seq is None else pl.BlockSpec((8, d), lambda i: (jnp.maximum(i * (tm // 8) - 1, 0), 0))
    consts = [lw["conv_w"], lw["norm_conv"], lw["norm_att"], lw["w_out"], lw["norm_ffn"],
              lw["w_gate"], lw["w_up"], lw["w_down"]]
    return pl.pallas_call(
        functools.partial(_mix_ffn_kernel, tiles_per_seq=tiles_per_seq),
        grid=(n // tm,),
        in_specs=[row(dm), row(d), row(d), hist, hist, row(d)] + [_resident(c.shape) for c in consts],
        out_specs=row(dm),
        out_shape=jax.ShapeDtypeStruct((n, dm), F32),
        scratch_shapes=[pltpu.VMEM((tm, dm), F32)],
        compiler_params=_params(1),
        name="mix_ffn",
    )(x, gb, u, u1, u2, oa, *consts)


def _t5_bucket(dist):
    n = jnp.maximum(dist, 0)
    max_exact = NUM_BUCKETS // 2
    nf = jnp.maximum(n, 1).astype(F32)
    large = max_exact + (jnp.log(nf / max_exact) / math.log(MAX_DISTANCE / max_exact)
                         * (NUM_BUCKETS - max_exact)).astype(jnp.int32)
    return jnp.where(n < max_exact, n, jnp.minimum(large, NUM_BUCKETS - 1))


def _bias(dist, tbl):
    onehot = jax.nn.one_hot(_t5_bucket(dist), NUM_BUCKETS, dtype=F32)
    return jnp.einsum("...k,kh->h...", onehot, tbl, precision=lax.Precision.HIGHEST)


def _cover(n_cmp, n_sel):
    c0 = np.arange(n_cmp)[:, None] * CMP_STRIDE
    s0 = np.arange(n_sel)[None, :] * SEL_BLOCK
    return np.clip(np.minimum(c0 + CMP_LEN, s0 + SEL_BLOCK) - np.maximum(c0, s0), 0, None) / CMP_STRIDE


def _prompt_tables(tbl, t):
    n_q = t // TQ
    n_chunks = t // CMP_STRIDE
    ar = jnp.arange(TQ, dtype=jnp.int32)
    far = tbl[NUM_BUCKETS - 1]
    assert MAX_DISTANCE <= TQ

    def key_major(b):
        k, q = b.shape[1:]
        return b.reshape(N_KV, GROUP, k, q).transpose(0, 2, 1, 3).reshape(N_KV, k, GROUP * q)

    near = []
    for c in range(2):
        dist = c * TQ + ar[None, :] - ar[:, None]
        b = (_bias(dist, tbl) - far[:, None, None]) * LOG2E
        near.append(key_major(jnp.where(dist[None] >= 0, b, NEG)))
    oldest = jnp.where(ar[:, None] > ar[None, :], 0.0, NEG)
    near.append(key_major(jnp.broadcast_to(oldest[None], (N_HEADS, TQ, TQ))))
    tpos = jnp.arange(t, dtype=jnp.int32)
    blk_end = jnp.arange(n_chunks, dtype=jnp.int32) * CMP_STRIDE + (CMP_LEN - 1)
    b = _bias(tpos[None, :] - blk_end[:, None], tbl) * LOG2E
    b = b.reshape(N_KV, GROUP, n_chunks, n_q, TQ).transpose(0, 3, 2, 1, 4).reshape(N_KV, n_q, n_chunks, GROUP * TQ)
    n_cmp = (t - CMP_LEN) // CMP_STRIDE + 1
    n_sel = -(-t // SEL_BLOCK)
    covt = np.zeros((LANES, n_chunks), np.float32)
    covt[:n_sel, :n_cmp] = _cover(n_cmp, n_sel).T
    keys = np.arange(-TQ, t)
    onehot = (keys[:, None] // SEL_BLOCK == np.arange(LANES)[None, :]) & (keys[:, None] >= 0)
    onehot[:, LANES - 1] = keys < 0
    wkeys = np.arange(-WINDOW, t)
    wflag = np.zeros((WINDOW + t, LANES), np.float32)
    wflag[:, 0] = wkeys < 0
    diag, prev, oldest = near
    none = jnp.zeros_like(prev)
    tabs = dict(sel=jnp.concatenate([none, prev, diag], axis=1), win_a=jnp.concatenate([oldest, none], axis=1)[0],
                win_b=jnp.concatenate([none, prev], axis=1), diag=diag)
    return dict(**tabs, bcmp=b, covt=jnp.asarray(covt, BF16),
                sel_onehot=jnp.asarray(onehot.astype(np.float32), BF16), win_flag=jnp.asarray(wflag, BF16))


def _sample_tables(tbl, t_new, pos0, n_win, n_pad):
    n_chunks = pos0 // CMP_STRIDE
    n_cmp = n_chunks - 1
    n_sel = -(-(pos0 + t_new) // SEL_BLOCK)
    selw = -(-n_sel // LANES) * LANES
    qpos = pos0 + jnp.arange(t_new, dtype=jnp.int32)

    def rows(dist):
        b = _bias(dist, tbl) * LOG2E
        return b.reshape(N_KV, GROUP, t_new, -1).transpose(1, 0, 2, 3).reshape(N_HEADS * t_new, -1)

    blk_end = jnp.arange(n_chunks, dtype=jnp.int32) * CMP_STRIDE + (CMP_LEN - 1)
    new_pos = pos0 + jnp.arange(n_pad, dtype=jnp.int32)
    cov = np.zeros((n_chunks, selw), np.float32)
    cov[:n_cmp, :n_sel] = _cover(n_cmp, n_sel)
    keys = PAGES_PER_STEP * PAGE_SIZE
    expand = (np.arange(LANES)[:, None] == np.arange(keys)[None, :] // SEL_BLOCK).astype(np.float32)
    bnew = rows(qpos[:, None] - new_pos[None, :])
    return dict(
        bcmp=rows(qpos[:, None] - blk_end[None, :]),
        bwc=rows(qpos[:, None] - (pos0 - n_win + jnp.arange(n_win, dtype=jnp.int32))[None, :]),
        bwn=bnew, bseln=bnew,
        bsel=rows(qpos[:, None] - jnp.arange(pos0, dtype=jnp.int32)[None, :]),
        cov=jnp.asarray(cov, BF16), expand=jnp.asarray(expand, BF16))


def _layer_weights(l, norm_mix, w_in, conv_w, q_norm, k_norm, cmp_pe, cmp_w1, cmp_w2, out_norm_conv,
                   out_norm_att, w_out, norm_ffn, w_gate, w_up, w_down, d_conv, d_att, tm_p):
    perm = np.arange(d_att).reshape(N_KV, GROUP, HEAD_DIM).transpose(1, 0, 2).reshape(-1)
    q0 = 3 * d_conv
    w = w_in[l]
    k0 = q0 + d_att
    n_gates = w.shape[1] - (k0 + 6 * D_KV)
    seg = lambda i: w[:, k0 + i * D_KV:k0 + (i + 1) * D_KV]
    gates = jnp.concatenate([w[:, k0 + 6 * D_KV:], jnp.zeros((w.shape[0], LANES - n_gates), w.dtype)], axis=1)
    w_rows = jnp.concatenate([w[:, :q0], w[:, q0:k0][:, perm], w[:, k0:], jnp.zeros_like(gates[:, n_gates:])],
                             axis=1).astype(BF16)
    w_tok = jnp.concatenate([w[:, :q0], seg(0), seg(1), seg(2), seg(4)], axis=1).astype(BF16)
    w_chan = jnp.concatenate([w[:, q0:k0][:, perm], seg(3), seg(5), gates[:, :GATE_ROWS]], axis=1).T.astype(BF16)
    wo = w_out[l]
    wo = jnp.concatenate([wo[:d_conv], wo[d_conv:][perm]], axis=0).astype(BF16)
    ck = 2 * LANES
    n_ck = w_gate.shape[2] // ck
    pair = lambda v: jnp.tile(v, 2)[None, :]
    q_gain = q_norm[l] * (SCALE * LOG2E)
    eye2 = jnp.eye(N_KV, dtype=F32)

    def cmp_weights(i, gain):
        w1 = cmp_w1[l, i]
        w2 = cmp_w2[l, i]
        wab = jnp.einsum("hrde,gk->rgdhke", w1.reshape(2, CMP_STRIDE, HEAD_DIM, CMP_HID), eye2)
        return dict(
            wab=wab.reshape(CMP_STRIDE * D_KV, 2 * N_KV * CMP_HID).astype(BF16),
            pe=jnp.broadcast_to(cmp_pe[l, i].reshape(1, -1), (8, CMP_LEN * HEAD_DIM)),
            w1f=w1.reshape(CMP_LEN * HEAD_DIM, CMP_HID),
            w2bd=jnp.einsum("ed,gk->gekd", w2, eye2).reshape(N_KV * CMP_HID, D_KV).astype(BF16),
            gain=gain)

    return dict(
        norm_mix=norm_mix[l][None, :], w_in=w_rows, w_tok=w_tok, w_chan=w_chan,
        q_gain=pair(q_gain), q_gain_t=jnp.broadcast_to(q_gain[:, None], (HEAD_DIM, tm_p)),
        k_gain=jnp.stack([jnp.tile(k_norm[l, 1], 2), jnp.tile(k_norm[l, 2], 2)]),
        cmp_k=cmp_weights(0, pair(k_norm[l, 0])), cmp_v=cmp_weights(1, jnp.ones((1, LANES), F32)),
        conv_w=jnp.pad(conv_w[l], ((0, 8 - CONV_WIDTH), (0, 0))),
        norm_conv=out_norm_conv[l][None, :], norm_att=out_norm_att[l][perm][None, :], w_out=wo,
        norm_ffn=norm_ffn[l][None, :],
        w_gate=w_gate[l].reshape(-1, n_ck, ck).transpose(1, 0, 2).astype(BF16),
        w_up=w_up[l].reshape(-1, n_ck, ck).transpose(1, 0, 2).astype(BF16),
        w_down=w_down[l].reshape(n_ck, ck, -1).astype(BF16))


def _shifted(u, hist):
    up = jnp.concatenate([hist, u], axis=1)
    t = u.shape[1]
    return up[:, 1:1 + t], up[:, 0:t]


def _chan_major(a):
    a = jnp.moveaxis(a, -3, -1)
    return a.reshape(a.shape[:-3] + (D_KV, a.shape[-1]))


def _token_major(a_t):
    a = a_t.reshape(a_t.shape[:-2] + (N_KV, HEAD_DIM, a_t.shape[-1]))
    return jnp.moveaxis(a, -1, -3)


def kernel(x_prompt, x_sample, state_conv, cache_cmp_k, cache_cmp_v, cache_slc_k, cache_slc_v,
           cache_win_k, cache_win_v, page_table, rel_bias, norm_mix, w_in, conv_w, q_norm, k_norm,
           cmp_pe, cmp_w1, cmp_w2, out_norm_conv, out_norm_att, w_out, norm_ffn, w_gate, w_up, w_down):
    nb, t, dm = x_prompt.shape
    ns, t_new, _ = x_sample.shape
    depth = w_in.shape[0]
    d_conv = conv_w.shape[2]
    d_att = dm - d_conv
    n_pool = cache_cmp_k.shape[1]
    n_pages = page_table.shape[1]
    pos0 = n_pages * PAGE_SIZE
    n_win = cache_win_k.shape[2]
    assert t % TQ == 0 and d_att == N_HEADS * HEAD_DIM
    assert n_pages % PAGES_PER_STEP == 0 and n_pages % CMP_PAGES_PER_STEP == 0
    assert n_win == WINDOW and WINDOW == 4 * TQ and pos0 >= WINDOW and t_new <= LANES and t_new & (t_new - 1) == 0
    assert t // CMP_STRIDE == LANES and SEL_BLOCK == 64 and t >= WINDOW and t // SEL_BLOCK < LANES - 1

    tm_p = 512
    kv = lambda a, b_, t_: a.reshape(b_, t_, N_KV, HEAD_DIM)
    ptab = _prompt_tables(rel_bias, t)
    stab = _sample_tables(rel_bias, t_new, pos0, n_win, LANES)
    chan_pool = lambda c: _chan_major(c).reshape(depth * n_pool, D_KV, PAGE_SIZE)
    cmp_pools = [chan_pool(cache_cmp_k), chan_pool(cache_cmp_v)]
    slc_pools = [chan_pool(cache_slc_k), chan_pool(cache_slc_v)]
    win_t = [_chan_major(cache_win_k), _chan_major(cache_win_v)]
    new_t = lambda a: jnp.pad(a.reshape(ns, t_new, LANES).transpose(0, 2, 1), ((0, 0), (0, 0), (0, LANES - t_new)))
    eye = jnp.eye(N_KV, dtype=BF16)[None, None, :, None, :, None]
    other = (jnp.arange(LANES)[None, :] // HEAD_DIM != jnp.arange(N_KV)[:, None])[None, :, :, None]

    xp = x_prompt.reshape(nb * t, dm)
    xs = x_sample.reshape(ns * t_new, dm)
    st_p, st_s = [], []
    for l in range(depth):
        lw = _layer_weights(l, norm_mix, w_in, conv_w, q_norm, k_norm, cmp_pe, cmp_w1, cmp_w2, out_norm_conv,
                            out_norm_att, w_out, norm_ffn, w_gate, w_up, w_down, d_conv, d_att, tm_p)

        gb, u, kc, vc, ks, kw, q_t, vs_t, vw_t, gates_t = _in_proj_t(
            xp, lw["norm_mix"], lw["w_tok"], lw["w_chan"], lw["q_gain_t"], lw["k_gain"], nb, d_conv, d_att, tm_p)
        seq = lambda a: a.reshape(nb, t, LANES)
        kcmp = _compress_rows(seq(kc), lw["cmp_k"], True, False)
        vcmp_t = _compress_rows(seq(vc), lw["cmp_v"], False, True)
        ks_aug = jnp.concatenate(
            [jnp.pad(seq(ks).astype(BF16), ((0, 0), (TQ, 0), (0, 0))),
             jnp.broadcast_to(ptab["sel_onehot"][None], (nb, TQ + t, LANES))], axis=2)
        kw_aug = jnp.concatenate(
            [jnp.pad(seq(kw).astype(BF16), ((0, 0), (WINDOW, 0), (0, 0))),
             jnp.broadcast_to(ptab["win_flag"][None], (nb, WINDOW + t, LANES))], axis=2)
        ones_rows = lambda v_t, front: jnp.where(
            other, 1.0, jnp.pad(v_t, ((0, 0), (0, 0), (front, 0)))[:, None]).astype(BF16)
        o_t = _pattn(q_t, gates_t, kcmp, vcmp_t, ks_aug, ones_rows(vs_t, TQ), kw_aug, ones_rows(vw_t, WINDOW), ptab)
        o_att = o_t.transpose(0, 3, 1, 2).reshape(nb * t, d_att)
        u3 = u.reshape(nb, t, d_conv)
        xp = _mix_ffn(xp, gb, u, u, u, o_att, lw, tm_p, tiles_per_seq=t // tm_p)
        n_keep = min(WINDOW, t)
        st_p.append((u3[:, -(CONV_WIDTH - 1):], kv(kc, nb, t), kv(vc, nb, t), kv(ks, nb, t), _token_major(vs_t),
                     kv(kw, nb, t)[:, -n_keep:], _token_major(vw_t[:, :, -n_keep:])))

        gb, u, q, kc, vc, ks, vs, kw, vw, gates = _in_proj(
            xs, lw["norm_mix"], lw["w_in"], lw["q_gain"], lw["k_gain"], d_conv, d_att, ns * t_new)
        base = l * n_pool
        kcmp = _compress_paged(cmp_pools[0], page_table, base, lw["cmp_k"], True)
        vcmp = _compress_paged(cmp_pools[1], page_table, base, lw["cmp_v"], False)
        qr = q.reshape(ns, t_new, GROUP, N_KV, HEAD_DIM).transpose(0, 2, 3, 1, 4)
        qm = (qr[:, :, :, :, None, :] * eye).reshape(ns, N_HEADS * t_new, LANES)
        gr = gates[:, :3 * N_HEADS].reshape(ns, t_new, 3, N_KV, GROUP).transpose(0, 2, 4, 3, 1)
        gr = jnp.broadcast_to(gr.reshape(ns, 3, N_HEADS * t_new, 1), (ns, 3, N_HEADS * t_new, LANES))
        kw_t, vw_t = new_t(kw), new_t(vw)
        o = _sattn(qm, kcmp, vcmp, win_t[0][l], win_t[1][l], kw_t, vw_t, new_t(ks), new_t(vs),
                   slc_pools[0], slc_pools[1], page_table, base, gr, stab, t_new, pos0)
        o6 = o.reshape(ns, GROUP, N_KV, t_new, N_KV, HEAD_DIM)
        o_att = jnp.stack([o6[:, :, g, :, g] for g in range(N_KV)], axis=2)
        o_att = o_att.transpose(0, 3, 1, 2, 4).reshape(ns * t_new, d_att)
        u3 = u.reshape(ns, t_new, d_conv)
        u1, u2 = _shifted(u3, state_conv[l])
        xs = _mix_ffn(xs, gb, u, u1.reshape(-1, d_conv), u2.reshape(-1, d_conv), o_att, lw, ns * t_new)
        new_conv = jnp.concatenate([state_conv[l], u3], axis=1)[:, -(CONV_WIDTH - 1):]
        n_keep = min(WINDOW, n_win + t_new)
        keep = lambda c_t, n_t: _token_major(jnp.concatenate([c_t, n_t[:, :, :t_new]], axis=2)[:, :, -n_keep:])
        st_s.append((new_conv, kv(kc, ns, t_new), kv(vc, ns, t_new), kv(ks, ns, t_new), kv(vs, ns, t_new),
                     keep(win_t[0][l], kw_t), keep(win_t[1][l], vw_t)))

    outs_p = [jnp.stack(a) for a in zip(*st_p)]
    outs_s = [jnp.stack(a) for a in zip(*st_s)]
    return (xp.reshape(nb, t, dm), xs.reshape(ns, t_new, dm), *outs_p, *outs_s)
```

```python
import functools
import math

import jax
import jax.numpy as jnp
import numpy as np
from jax import lax
from jax.experimental import pallas as pl
from jax.experimental.pallas import tpu as pltpu

F32 = jnp.float32
BF16 = jnp.bfloat16

HEAD_DIM = 64
N_KV = 2
GROUP = 4
N_HEADS = N_KV * GROUP
D_KV = N_KV * HEAD_DIM
CMP_LEN = 32
CMP_STRIDE = 16
CMP_HID = 2 * HEAD_DIM
SEL_BLOCK = 64
SEL_TOPK = 16
WINDOW = 512
NUM_BUCKETS = 32
MAX_DISTANCE = 128
PAGE_SIZE = 128
CONV_WIDTH = 3
EPS = 1e-6
SCALE = HEAD_DIM ** -0.5
LOG2E = math.log2(math.e)
NEG = -1e30

LANES = 128
TQ = 128
GATE_ROWS = 32
FF_CHUNK = 256
CMP_PAGES_PER_STEP = 64
PAGES_PER_STEP = 32
VMEM_LIMIT = 56 * 1024 * 1024

_NT = (((1,), (1,)), ((), ()))


def _dot(a, b):
    return jnp.dot(a, b, preferred_element_type=F32)


def _dot_nt(a, b):
    return lax.dot_general(a, b, _NT, preferred_element_type=F32)


def _params(n_axes):
    return pltpu.CompilerParams(dimension_semantics=("arbitrary",) * n_axes,
                                vmem_limit_bytes=VMEM_LIMIT)


def _resident(shape):
    nd = len(shape)
    return pl.BlockSpec(shape, lambda *_: (0,) * nd, pipeline_mode=pl.Buffered(1))


def _rms(x, gain):
    return x * lax.rsqrt(jnp.mean(x * x, axis=-1, keepdims=True) + EPS) * gain


def _pair_rms(y, gain, lo):
    y2 = y * y
    s_lo = jnp.sum(jnp.where(lo, y2, 0.0), axis=-1, keepdims=True)
    s_hi = jnp.sum(jnp.where(lo, 0.0, y2), axis=-1, keepdims=True)
    inv = jnp.where(lo, lax.rsqrt(s_lo / HEAD_DIM + EPS), lax.rsqrt(s_hi / HEAD_DIM + EPS))
    return y * inv * gain


def _in_proj_t_kernel(x_ref, nm_ref, w_ref, wt_ref, qgt_ref, kg_ref,
                      gb_ref, u_ref, kc_ref, vc_ref, ks_ref, kw_ref, qt_ref, vst_ref, vwt_ref, gtt_ref):
    x = x_ref[...]
    hb = _rms(x, nm_ref[...]).astype(BF16)
    d = gb_ref.shape[1]

    def seg(a, b):
        return _dot(hb, w_ref[:, a:b])

    gb_ref[...] = seg(0, d)
    u_ref[...] = seg(d, 2 * d) * seg(2 * d, 3 * d)
    lo = lax.broadcasted_iota(jnp.int32, (x.shape[0], LANES), 1) < HEAD_DIM
    c = 3 * d
    kc_ref[...] = seg(c, c + LANES)
    vc_ref[...] = seg(c + LANES, c + 2 * LANES)
    ks_ref[...] = _pair_rms(seg(c + 2 * LANES, c + 3 * LANES), kg_ref[0:1, :], lo)
    kw_ref[...] = _pair_rms(seg(c + 3 * LANES, c + 4 * LANES), kg_ref[1:2, :], lo)

    yt = _dot_nt(wt_ref[...], hb)
    d_att = qt_ref.shape[1]
    for h in range(d_att // HEAD_DIM):
        y = yt[h * HEAD_DIM:(h + 1) * HEAD_DIM]
        inv = lax.rsqrt(jnp.mean(y * y, axis=0, keepdims=True) + EPS)
        qt_ref[0, h * HEAD_DIM:(h + 1) * HEAD_DIM, :] = (y * inv * qgt_ref[...]).astype(BF16)
    vst_ref[0] = yt[d_att:d_att + LANES]
    vwt_ref[0] = yt[d_att + LANES:d_att + 2 * LANES]
    gtt_ref[0] = jax.nn.sigmoid(yt[d_att + 2 * LANES:d_att + 2 * LANES + GATE_ROWS])


def _in_proj_t(x, nm, w, wt, qgt, kg, nb, d_conv, d_att, tm):
    n, dm = x.shape
    t = n // nb
    per_b = t // tm
    row = lambda w_: pl.BlockSpec((tm, w_), lambda i: (i, 0))
    col = lambda r: pl.BlockSpec((1, r, tm), lambda i: (i // per_b, 0, i % per_b))
    widths = [d_conv, d_conv] + [LANES] * 4
    rows_t = [d_att, LANES, LANES, GATE_ROWS]
    dt_t = [BF16, F32, F32, F32]
    return pl.pallas_call(
        _in_proj_t_kernel,
        grid=(n // tm,),
        in_specs=[row(dm), _resident(nm.shape), _resident(w.shape), _resident(wt.shape),
                  _resident(qgt.shape), _resident(kg.shape)],
        out_specs=[row(w_) for w_ in widths] + [col(r) for r in rows_t],
        out_shape=[jax.ShapeDtypeStruct((n, w_), F32) for w_ in widths]
        + [jax.ShapeDtypeStruct((nb, r, t), dt) for r, dt in zip(rows_t, dt_t)],
        compiler_params=_params(1),
        name="in_proj_t",
    )(x, nm, w, wt, qgt, kg)


def _cmp_ab_kernel(x_ref, w_ref, a_ref, b_ref):
    n_chunks = x_ref.shape[1] // CMP_STRIDE
    cols = [x_ref[0, pl.ds(l, n_chunks, stride=CMP_STRIDE), :] for l in range(CMP_STRIDE)]
    xr = jnp.concatenate(cols, axis=1).astype(BF16)
    ab = _dot(xr, w_ref[...])
    half = ab.shape[1] // 2
    a_ref[0] = ab[:, :half]
    b_ref[0] = ab[:, half:]


def _cmp_ab_paged_kernel(pt_ref, *refs, n_in):
    x_refs = refs[:n_in]
    perm_ref, w_ref, a_ref, b_ref = refs[n_in:]
    per_page = PAGE_SIZE // CMP_STRIDE
    rows = []
    for xr in x_refs:
        xp = _dot_nt(perm_ref[...], xr[0].astype(BF16))
        rows.append(jnp.concatenate([xp[l * per_page:(l + 1) * per_page] for l in range(CMP_STRIDE)], axis=1))
    xr = jnp.concatenate(rows, axis=0).astype(BF16)
    ab = _dot(xr, w_ref[...])
    half = ab.shape[1] // 2
    a_ref[0] = ab[:, :half]
    b_ref[0] = ab[:, half:]


def _cmp_fin_kernel(a_ref, b_ref, pe_ref, w1_ref, w2_ref, g_ref, o_ref, *, n_valid, normalize, transpose_out):
    a = a_ref[0]
    bm = b_ref[0]
    n_chunks = a.shape[0]
    b_next = pltpu.roll(bm, n_chunks - 1, 0)
    cpe = jnp.dot(pe_ref[...], w1_ref[...], preferred_element_type=F32,
                  precision=lax.Precision.HIGHEST)[0:1]
    x = a + b_next + jnp.concatenate([cpe, cpe], axis=1)
    hid = x * (0.5 * (1.0 + jnp.tanh(math.sqrt(2.0 / math.pi) * (x + 0.044715 * (x * x * x)))))
    out = _dot(hid.astype(BF16), w2_ref[...])
    if normalize:
        lo = lax.broadcasted_iota(jnp.int32, out.shape, 1) < HEAD_DIM
        out = _pair_rms(out, g_ref[...], lo)
    rows = lax.broadcasted_iota(jnp.int32, out.shape, 0)
    out = jnp.where(rows < n_valid, out, 0.0)
    o_ref[0] = out.T if transpose_out else out


def _cmp_fin(a, b, cw, n_valid, normalize, transpose_out=False):
    nb, n_chunks, wid = a.shape
    blk = pl.BlockSpec((1, n_chunks, wid), lambda i: (i, 0, 0))
    oshape = (LANES, n_chunks) if transpose_out else (n_chunks, LANES)
    return pl.pallas_call(
        functools.partial(_cmp_fin_kernel, n_valid=n_valid, normalize=normalize, transpose_out=transpose_out),
        grid=(nb,),
        in_specs=[blk, blk, _resident(cw["pe"].shape), _resident(cw["w1f"].shape),
                  _resident(cw["w2bd"].shape), _resident(cw["gain"].shape)],
        out_specs=pl.BlockSpec((1,) + oshape, lambda i: (i, 0, 0)),
        out_shape=jax.ShapeDtypeStruct((nb,) + oshape, F32),
        compiler_params=_params(1),
        name="cmp_fin",
    )(a, b, cw["pe"], cw["w1f"], cw["w2bd"], cw["gain"])


def _compress_rows(rows, cw, normalize, transpose_out):
    nb, t, _ = rows.shape
    n_chunks = t // CMP_STRIDE
    wid = cw["wab"].shape[1] // 2
    out_blk = pl.BlockSpec((1, n_chunks, wid), lambda i: (i, 0, 0))
    a, b = pl.pallas_call(
        _cmp_ab_kernel,
        grid=(nb,),
        in_specs=[pl.BlockSpec((1, t, LANES), lambda i: (i, 0, 0)), _resident(cw["wab"].shape)],
        out_specs=[out_blk, out_blk],
        out_shape=[jax.ShapeDtypeStruct((nb, n_chunks, wid), F32)] * 2,
        compiler_params=_params(1),
        name="cmp_ab_rows",
    )(rows, cw["wab"])
    return _cmp_fin(a, b, cw, (t - CMP_LEN) // CMP_STRIDE + 1, normalize, transpose_out)


def _page_spec(i, base, pages_per_step):
    return pl.BlockSpec((1, PAGE_SIZE, LANES),
                        lambda b, s, pt: (base + pt[b, s * pages_per_step + i], 0, 0))


def _compress_paged(cache, page_table, base, cw, normalize):
    nb, n_pages = page_table.shape
    pps = CMP_PAGES_PER_STEP
    per_page = PAGE_SIZE // CMP_STRIDE
    ch = pps * per_page
    wid = cw["wab"].shape[1] // 2
    n_chunks = n_pages * per_page
    out_blk = pl.BlockSpec((1, ch, wid), lambda b, s, pt: (b, s, 0))
    r = np.arange(PAGE_SIZE)
    perm = jnp.asarray(r[None, :] == (r % per_page)[:, None] * CMP_STRIDE + (r // per_page)[:, None], BF16)
    cst = lambda a: pl.BlockSpec(a.shape, lambda b, s, pt: (0, 0), pipeline_mode=pl.Buffered(1))
    a, b = pl.pallas_call(
        functools.partial(_cmp_ab_paged_kernel, n_in=pps),
        grid_spec=pltpu.PrefetchScalarGridSpec(
            num_scalar_prefetch=1,
            grid=(nb, n_pages // pps),
            in_specs=[_page_spec(i, base, pps) for i in range(pps)] + [cst(perm), cst(cw["wab"])],
            out_specs=[out_blk, out_blk]),
        out_shape=[jax.ShapeDtypeStruct((nb, n_chunks, wid), F32)] * 2,
        compiler_params=_params(2),
        name="cmp_ab_paged",
    )(page_table, *([cache] * pps), perm, cw["wab"])
    return _cmp_fin(a, b, cw, n_chunks - 1, normalize)


def _split_bf16(x):
    hi = x.astype(BF16)
    return hi, (x - hi.astype(F32)).astype(BF16)


def _online_update(state, s, mask, v_t):
    m, l, acc = state
    s = jnp.where(mask, s, NEG)
    m_new = jnp.maximum(m, jnp.max(s, axis=-1, keepdims=True))
    alpha = jnp.exp2(m - m_new)
    e = jnp.where(mask, jnp.exp2(s - m_new), 0.0)
    l = alpha * l + jnp.sum(e, axis=-1, keepdims=True)
    acc = alpha * acc + _dot_nt(e.astype(BF16), v_t)
    return m_new, l, acc


def _finish(state):
    _, l, acc = state
    return acc / jnp.where(l > 0, l, 1.0)


def _reset_t(m_ref, acc_ref):
    m_ref[...] = jnp.full(m_ref.shape, NEG, F32)
    acc_ref[...] = jnp.zeros(acc_ref.shape, F32)


def _update_t(m_ref, acc_ref, s_t, v_t):
    m = m_ref[...]
    m_new = jnp.maximum(m, jnp.max(s_t, axis=0, keepdims=True))
    alpha = jnp.exp2(m - m_new)
    e = jnp.exp2(s_t - m_new).astype(BF16)
    acc_ref[...] = alpha * acc_ref[...] + _dot(v_t, e)
    m_ref[...] = m_new


def _pattn_kernel(qt_ref, gt_ref, kcmp_ref, vcmpt_ref, ksa_ref, vst_ref, kwa_ref, vwt_ref,
                  bcmp_ref, stab_ref, wa_ref, wb_ref, diag_ref, covt_ref, o_ref, m_ref, acc_ref, mw_ref, accw_ref, qa_ref, out_ref, *, n_sel, topk):
    qi = pl.program_id(1)
    q0 = qi * TQ
    n_slabs = qt_ref.shape[1] // LANES
    wide = n_slabs * TQ
    sub = lax.broadcasted_iota(jnp.int32, (LANES, TQ), 0)
    lane = lax.broadcasted_iota(jnp.int32, (LANES, TQ), 1)
    sub_w = lax.broadcasted_iota(jnp.int32, (LANES, wide), 0)
    tpos_w = q0 + (lax.broadcasted_iota(jnp.int32, (LANES, wide), 1) & (TQ - 1))
    kcmp = kcmp_ref[0].astype(BF16)
    vcmp_t = vcmpt_ref[0].astype(BF16)
    covt = covt_ref[...]
    gt = gt_ref[0]
    zero_rows = jnp.zeros((LANES - n_sel, TQ), F32)
    def gate(br, g):
        return jnp.concatenate([gt[br * N_HEADS + g * GROUP + j:br * N_HEADS + g * GROUP + j + 1, :]
                                for j in range(n_slabs)], axis=1)

    def keys(k_ref, v_ref, g, kt, n):
        start = pl.multiple_of(kt * TQ, TQ)
        return k_ref[0, pl.ds(start, n * TQ), :], v_ref[0, g, :, pl.ds(start, n * TQ)]

    def add_branch(br, acc_ref_):
        for g in range(N_KV):
            acc = acc_ref_[g]
            den = acc[HEAD_DIM:HEAD_DIM + 1] if g == 0 else acc[0:1]
            out_ref[g] += (gate(br, g) * (1.0 / den)) * acc

    q_ts = []
    for g in range(N_KV):
        mine = (sub < HEAD_DIM) if g == 0 else (sub >= HEAD_DIM)
        q_ts.append(jnp.concatenate(
            [jnp.where(mine, qt_ref[0, j * LANES:(j + 1) * LANES, :], jnp.zeros((), BF16))
             for j in range(n_slabs)], axis=1))
        qa_ref[g, 0:LANES, :] = q_ts[g]

    pad_pen = jnp.where(sub_w == 0, NEG, 0.0).astype(BF16)
    win = []
    for g in range(N_KV):
        q_win = jnp.concatenate([q_ts[g], pad_pen], axis=0)
        jobs = []
        for kt, n, tab in ((qi, 2, wa_ref[...]), (qi + 2, 2, wb_ref[g]), (qi + 4, 1, diag_ref[g])):
            k_, v_ = keys(kwa_ref, vwt_ref, g, kt, n)
            jobs.append((_dot(k_, q_win) + tab, v_))
        win.append(jobs)
    s_cmp = [_dot(kcmp, q_ts[g]) + bcmp_ref[g, 0] for g in range(N_KV)]

    for g in range(N_KV):
        mask = sub_w * CMP_STRIDE + (CMP_LEN - 1) <= tpos_w
        s = jnp.where(mask, s_cmp[g], NEG)
        e = jnp.where(mask, jnp.exp2(s - jnp.max(s, axis=0, keepdims=True)), 0.0)
        l = jnp.sum(e, axis=0, keepdims=True)
        p = e * (1.0 / jnp.where(l > 0, l, 1.0))
        out_ref[g] = gate(0, g) * _dot(vcmp_t, p.astype(BF16))
        p_sum = p[:, 0:TQ]
        for j in range(1, n_slabs):
            p_sum = p_sum + p[:, j * TQ:(j + 1) * TQ]

        hi, lo_ = _split_bf16(p_sum)
        sc = _dot(covt, hi) + _dot(covt, lo_)
        tq_pos = q0 + lane
        cur = tq_pos >> 6
        forced = (sub == 0) | (sub == cur) | (sub == cur - 1)
        sc = jnp.where(forced, 1e6, jnp.where(sub * SEL_BLOCK > tq_pos, -1e6, sc))
        sc = sc[0:n_sel]
        blk = sub[0:n_sel]
        rank = jnp.zeros((n_sel, TQ), F32)
        for i in range(n_sel):
            si = sc[i:i + 1, :]
            beats = (si > sc) | ((si == sc) & (blk > i))
            rank = rank + jnp.where(beats, 1.0, 0.0)
        pen = jnp.concatenate([jnp.where(rank < topk, 0.0, NEG), zero_rows], axis=0)
        pen = jnp.where(sub == LANES - 1, NEG, pen).astype(BF16)
        qa_ref[g, LANES:2 * LANES, :] = jnp.concatenate([pen] * n_slabs, axis=1)

    _reset_t(mw_ref, accw_ref)
    for job in range(len(win[0])):
        for g in range(N_KV):
            _update_t(mw_ref.at[g], accw_ref.at[g], *win[g][job])

    _reset_t(m_ref, acc_ref)
    n_far = jnp.maximum(qi - 1, 0)

    @pl.loop(0, n_far // 2)
    def _(i):
        kv = [keys(ksa_ref, vst_ref, g, 1 + 2 * i, 2) for g in range(N_KV)]
        s = [_dot(kv[g][0], qa_ref[g]) for g in range(N_KV)]
        for g in range(N_KV):
            _update_t(m_ref.at[g], acc_ref.at[g], s[g], kv[g][1])

    odd = jnp.where(n_far % 2 == 1, n_far, 0)
    last = []
    for g in range(N_KV):
        k0, v0 = keys(ksa_ref, vst_ref, g, odd, 1)
        k1, v1 = keys(ksa_ref, vst_ref, g, qi, 2)
        last.append((_dot(jnp.concatenate([k0, k1], axis=0), qa_ref[g]) + stab_ref[g],
                     jnp.concatenate([v0, v1], axis=1)))
    for g in range(N_KV):
        _update_t(m_ref.at[g], acc_ref.at[g], *last[g])
    add_branch(1, acc_ref)
    add_branch(2, accw_ref)

    out = jnp.where(sub_w < HEAD_DIM, out_ref[0], out_ref[1])
    for j in range(n_slabs):
        o_ref[0, :, j * LANES:(j + 1) * LANES] = out[:, j * TQ:(j + 1) * TQ].T


def _pattn(q_t, gates_t, kcmp, vcmp_t, ks_aug, vs_t, kw_aug, vw_t, tabs):
    nb, d_att, t = q_t.shape
    n_q = t // TQ
    n_sel = -(-t // SEL_BLOCK)
    n_slabs = d_att // LANES
    whole = lambda a: pl.BlockSpec((1,) + a.shape[1:], lambda b, i: (b,) + (0,) * (a.ndim - 1))
    return pl.pallas_call(
        functools.partial(_pattn_kernel, n_sel=n_sel, topk=min(SEL_TOPK, n_sel)),
        grid=(nb, n_q),
        in_specs=[pl.BlockSpec((1, d_att, TQ), lambda b, i: (b, 0, i)),
                  pl.BlockSpec((1, GATE_ROWS, TQ), lambda b, i: (b, 0, i)),
                  whole(kcmp), whole(vcmp_t), whole(ks_aug), whole(vs_t), whole(kw_aug), whole(vw_t),
                  pl.BlockSpec((N_KV, 1, LANES, n_slabs * TQ), lambda b, i: (0, i, 0, 0)),
                  _resident(tabs["sel"].shape), _resident(tabs["win_a"].shape), _resident(tabs["win_b"].shape),
                  _resident(tabs["diag"].shape), _resident(tabs["covt"].shape)],
        out_specs=pl.BlockSpec((1, TQ, d_att), lambda b, i: (b, i, 0)),
        out_shape=jax.ShapeDtypeStruct((nb, t, d_att), F32),
        scratch_shapes=[pltpu.VMEM((N_KV, 1, n_slabs * TQ), F32), pltpu.VMEM((N_KV, LANES, n_slabs * TQ), F32)] * 2
        + [pltpu.VMEM((N_KV, 2 * LANES, n_slabs * TQ), BF16), pltpu.VMEM((N_KV, LANES, n_slabs * TQ), F32)],
        compiler_params=_params(2),
        name="pattn",
    )(q_t, gates_t, kcmp, vcmp_t, ks_aug, vs_t, kw_aug, vw_t, tabs["bcmp"], tabs["sel"], tabs["win_a"],
      tabs["win_b"], tabs["diag"], tabs["covt"])


def _sattn1_kernel(qm_ref, kcmp_ref, vcmp_ref, kwc_ref, vwc_ref, kwn_ref, vwn_ref,
                   bcmp_ref, bwc_ref, bwn_ref, cov_ref, ocmp_ref, owin_ref, sel_ref,
                   *, n_cmp, n_sel, topk, t_new, pos0, win_start):
    qm = qm_ref[0]
    n_rows = qm.shape[0]
    n_gt = N_KV * t_new
    s = _dot_nt(qm, kcmp_ref[0].astype(BF16)) + bcmp_ref[...]
    ncol = lax.broadcasted_iota(jnp.int32, s.shape, 1)
    trow = pos0 + (lax.broadcasted_iota(jnp.int32, s.shape, 0) & (t_new - 1))
    mask = (ncol < n_cmp) & (ncol * CMP_STRIDE + (CMP_LEN - 1) <= trow)
    s = jnp.where(mask, s, NEG)
    e = jnp.where(mask, jnp.exp2(s - jnp.max(s, axis=-1, keepdims=True)), 0.0)
    l = jnp.sum(e, axis=-1, keepdims=True)
    p = e / jnp.where(l > 0, l, 1.0)
    ocmp_ref[0] = _dot(p.astype(BF16), vcmp_ref[0].astype(BF16))
    p_sum = p[0:n_gt]
    for j in range(1, n_rows // n_gt):
        p_sum = p_sum + p[j * n_gt:(j + 1) * n_gt]
    hi, lo = _split_bf16(p_sum)
    sc = _dot(hi, cov_ref[...]) + _dot(lo, cov_ref[...])
    blk = lax.broadcasted_iota(jnp.int32, sc.shape, 1)
    qpos = pos0 + (lax.broadcasted_iota(jnp.int32, sc.shape, 0) & (t_new - 1))
    cur = qpos >> 6
    forced = (blk == 0) | (blk == cur) | (blk == cur - 1)
    sc = jnp.where(forced, 1e6, jnp.where(blk * SEL_BLOCK > qpos, -1e6, sc))
    sc = jnp.where(blk < n_sel, sc, -3e38)
    rank = jnp.zeros(sc.shape, F32)
    for i in range(n_sel):
        si = sc[:, i:i + 1]
        beats = (si > sc) | ((si == sc) & (blk > i))
        rank = rank + jnp.where(beats, 1.0, 0.0)
    sel_ref[0] = jnp.where((rank < topk) & (blk < n_sel), 1.0, 0.0)

    init = (jnp.full((n_rows, 1), NEG, F32), jnp.zeros((n_rows, 1), F32), jnp.zeros((n_rows, LANES), F32))
    s = _dot(qm, kwc_ref[0].astype(BF16)) + bwc_ref[...]
    trow = pos0 + (lax.broadcasted_iota(jnp.int32, s.shape, 0) & (t_new - 1))
    dist = trow - (win_start + lax.broadcasted_iota(jnp.int32, s.shape, 1))
    st = _online_update(init, s, (dist >= 0) & (dist < WINDOW), vwc_ref[0].astype(BF16))
    s = _dot(qm, kwn_ref[0].astype(BF16)) + bwn_ref[...]
    trow = pos0 + (lax.broadcasted_iota(jnp.int32, s.shape, 0) & (t_new - 1))
    col = lax.broadcasted_iota(jnp.int32, s.shape, 1)
    dist = trow - (pos0 + col)
    st = _online_update(st, s, (dist >= 0) & (dist < WINDOW) & (col < t_new), vwn_ref[0].astype(BF16))
    owin_ref[0] = _finish(st)


def _sattn2_kernel(pt_ref, *refs, pps, t_new):
    k_refs = refs[:pps]
    v_refs = refs[pps:2 * pps]
    (qm_ref, selc_ref, seln_ref, ksn_ref, vsn_ref, bsel_ref, bseln_ref, exp_ref,
     ocmp_ref, owin_ref, gate_ref, o_ref, m_sc, l_sc, acc_sc) = refs[2 * pps:]
    step = pl.program_id(1)
    qm = qm_ref[0]
    n_rows = qm.shape[0]
    reps = n_rows // selc_ref.shape[2]

    @pl.when(step == 0)
    def _():
        s = _dot(qm, ksn_ref[0].astype(BF16)) + bseln_ref[...]
        trow = lax.broadcasted_iota(jnp.int32, s.shape, 0) & (t_new - 1)
        col = lax.broadcasted_iota(jnp.int32, s.shape, 1)
        hit = jnp.concatenate([seln_ref[0]] * reps, axis=0) > 0.5
        init = (jnp.full((n_rows, 1), NEG, F32), jnp.zeros((n_rows, 1), F32),
                jnp.zeros((n_rows, LANES), F32))
        m, l, acc = _online_update(init, s, hit & (col <= trow) & (col < t_new), vsn_ref[0].astype(BF16))
        m_sc[...] = m
        l_sc[...] = l
        acc_sc[...] = acc

    k_t = jnp.concatenate([r[0] for r in k_refs], axis=1).astype(BF16)
    v_t = jnp.concatenate([r[0] for r in v_refs], axis=1).astype(BF16)
    s = _dot(qm, k_t) + bsel_ref[...]
    hit = _dot(selc_ref[0, 0].astype(BF16), exp_ref[...])
    mask = jnp.concatenate([hit] * reps, axis=0) > 0.5
    m, l, acc = _online_update((m_sc[...], l_sc[...], acc_sc[...]), s, mask, v_t)
    m_sc[...] = m
    l_sc[...] = l
    acc_sc[...] = acc

    @pl.when(step == pl.num_programs(1) - 1)
    def _():
        o_sel = _finish((m, l, acc))
        o_ref[0] = gate_ref[0, 0] * ocmp_ref[0] + gate_ref[0, 1] * o_sel + gate_ref[0, 2] * owin_ref[0]


def _sattn(qm, kcmp, vcmp, kwc_t, vwc_t, kwn_t, vwn_t, ksn_t, vsn_t, cache_k, cache_v, page_table, base,
           gates, tabs, t_new, pos0):
    nb, n_rows, _ = qm.shape
    n_pages = page_table.shape[1]
    n_chunks = kcmp.shape[1]
    n_gt = N_KV * t_new
    n_sel = -(-(pos0 + t_new) // SEL_BLOCK)
    selw = tabs["cov"].shape[1]
    per_b = lambda shape: pl.BlockSpec((1,) + shape, lambda b: (b,) + (0,) * len(shape))
    ocmp, owin, sel = pl.pallas_call(
        functools.partial(_sattn1_kernel, n_cmp=n_chunks - 1, n_sel=n_sel, topk=min(SEL_TOPK, n_sel),
                          t_new=t_new, pos0=pos0, win_start=pos0 - kwc_t.shape[2]),
        grid=(nb,),
        in_specs=[per_b((n_rows, LANES)), per_b((n_chunks, LANES)), per_b((n_chunks, LANES)),
                  per_b(kwc_t.shape[1:]), per_b(vwc_t.shape[1:]), per_b(kwn_t.shape[1:]), per_b(vwn_t.shape[1:]),
                  _resident(tabs["bcmp"].shape), _resident(tabs["bwc"].shape), _resident(tabs["bwn"].shape),
                  _resident(tabs["cov"].shape)],
        out_specs=[per_b((n_rows, LANES)), per_b((n_rows, LANES)), per_b((n_gt, selw))],
        out_shape=[jax.ShapeDtypeStruct((nb, n_rows, LANES), F32), jax.ShapeDtypeStruct((nb, n_rows, LANES), F32),
                   jax.ShapeDtypeStruct((nb, n_gt, selw), F32)],
        compiler_params=_params(1),
        name="sattn1",
    )(qm, kcmp, vcmp, kwc_t, vwc_t, kwn_t, vwn_t, tabs["bcmp"], tabs["bwc"], tabs["bwn"], tabs["cov"])

    pps = PAGES_PER_STEP
    n_steps = n_pages // pps
    bps = pps * PAGE_SIZE // SEL_BLOCK
    n_cblk = n_pages * PAGE_SIZE // SEL_BLOCK
    selc = sel[:, :, :n_cblk].reshape(nb, n_gt, n_steps, bps).transpose(0, 2, 1, 3)
    selc = jnp.pad(selc, ((0, 0), (0, 0), (0, 0), (0, LANES - bps)))
    seln = jnp.broadcast_to(sel[:, :, n_cblk:n_cblk + 1], (nb, n_gt, LANES))

    keys = pps * PAGE_SIZE
    cst = lambda shape: pl.BlockSpec(shape, lambda b, s, pt: (0,) * len(shape), pipeline_mode=pl.Buffered(1))
    pb = lambda shape: pl.BlockSpec((1,) + shape, lambda b, s, pt: (b,) + (0,) * len(shape))
    return pl.pallas_call(
        functools.partial(_sattn2_kernel, pps=pps, t_new=t_new),
        grid_spec=pltpu.PrefetchScalarGridSpec(
            num_scalar_prefetch=1,
            grid=(nb, n_steps),
            in_specs=[_page_spec(i, base, pps) for i in range(pps)] * 2
            + [pb((n_rows, LANES)),
               pl.BlockSpec((1, 1, n_gt, LANES), lambda b, s, pt: (b, s, 0, 0)),
               pb((n_gt, LANES)), pb(ksn_t.shape[1:]), pb(vsn_t.shape[1:]),
               pl.BlockSpec((n_rows, keys), lambda b, s, pt: (0, s)),
               cst(tabs["bseln"].shape), cst(tabs["expand"].shape),
               pb((n_rows, LANES)), pb((n_rows, LANES)), pb((3, n_rows, LANES))],
            out_specs=pb((n_rows, LANES)),
            scratch_shapes=[pltpu.VMEM((n_rows, 1), F32), pltpu.VMEM((n_rows, 1), F32),
                            pltpu.VMEM((n_rows, LANES), F32)]),
        out_shape=jax.ShapeDtypeStruct((nb, n_rows, LANES), F32),
        compiler_params=_params(2),
        name="sattn2",
    )(page_table, *([cache_k] * pps), *([cache_v] * pps), qm, selc, seln, ksn_t, vsn_t,
      tabs["bsel"], tabs["bseln"], tabs["expand"], ocmp, owin, gates)


def _mix_ffn_kernel(x_ref, gb_ref, u_ref, u1_ref, u2_ref, oa_ref, cw_ref, nc_ref, na_ref, wo_ref,
                    nf_ref, wg_ref, wu_ref, wd_ref, y_ref, acc_ref, *, tiles_per_seq):
    d = gb_ref.shape[1]
    u = u_ref[...]
    if tiles_per_seq is None:
        u1, u2 = u1_ref[...], u2_ref[...]
    else:
        prev = u1_ref[...] * jnp.where(pl.program_id(0) % tiles_per_seq == 0, 0.0, 1.0)
        row = lax.broadcasted_iota(jnp.int32, u.shape, 0)
        u1 = jnp.where(row == 0, prev[7:8], pltpu.roll(u, 1, 0))
        u2 = jnp.where(row == 0, prev[6:7], jnp.where(row == 1, prev[7:8], pltpu.roll(u, 2, 0)))
    conv = u2 * cw_ref[0:1, :] + u1 * cw_ref[1:2, :] + u * cw_ref[2:3, :]
    yc = _rms(gb_ref[...] * conv, nc_ref[...]).astype(BF16)
    oa = _rms(oa_ref[...], na_ref[...]).astype(BF16)
    x1 = x_ref[...] + (_dot(yc, wo_ref[0:d, :]) + _dot(oa, wo_ref[d:, :]))
    h2 = _rms(x1, nf_ref[...]).astype(BF16)
    acc_ref[...] = jnp.zeros_like(acc_ref)
    for c in range(0, wg_ref.shape[1], FF_CHUNK):
        gate = _dot(h2, wg_ref[:, c:c + FF_CHUNK])
        act = (gate * jax.nn.sigmoid(gate)) * _dot(h2, wu_ref[:, c:c + FF_CHUNK])
        acc_ref[...] += _dot(act.astype(BF16), wd_ref[c:c + FF_CHUNK, :])
    y_ref[...] = x1 + acc_ref[...]


def _mix_ffn(x, gb, u, u1, u2, oa, lw, tm, tiles_per_seq=None):
    n, dm = x.shape
    d = gb.shape[1]
    row = lambda w_: pl.BlockSpec((tm, w_), lambda i: (i, 0))
    hist = row(d) if tiles_per_seq is None else pl.BlockSpec((8, d), lambda i: (jnp.maximum(i * (tm // 8) - 1, 0), 0))
    consts = [lw["conv_w"], lw["norm_conv"], lw["norm_att"], lw["w_out"], lw["norm_ffn"],
              lw["w_gate"], lw["w_up"], lw["w_down"]]
    return pl.pallas_call(
        functools.partial(_mix_ffn_kernel, tiles_per_seq=tiles_per_seq),
        grid=(n // tm,),
        in_specs=[row(dm), row(d), row(d), hist, hist, row(d)] + [_resident(c.shape) for c in consts],
        out_specs=row(dm),
        out_shape=jax.ShapeDtypeStruct((n, dm), F32),
        scratch_shapes=[pltpu.VMEM((tm, dm), F32)],
        compiler_params=_params(1),
        name="mix_ffn",
    )(x, gb, u, u1, u2, oa, *consts)


def _t5_bucket(dist):
    n = jnp.maximum(dist, 0)
    max_exact = NUM_BUCKETS // 2
    nf = jnp.maximum(n, 1).astype(F32)
    large = max_exact + (jnp.log(nf / max_exact) / math.log(MAX_DISTANCE / max_exact)
                         * (NUM_BUCKETS - max_exact)).astype(jnp.int32)
    return jnp.where(n < max_exact, n, jnp.minimum(large, NUM_BUCKETS - 1))


def _bias_lut(tbl):
    return tbl[_t5_bucket(jnp.arange(MAX_DISTANCE + 1, dtype=jnp.int32))]


def _bias(dist, lut):
    idx = jnp.clip(dist, 0, MAX_DISTANCE)
    onehot = (idx[..., None] == jnp.arange(MAX_DISTANCE + 1, dtype=jnp.int32)).astype(F32)
    return jnp.einsum("...k,kh->h...", onehot, lut, precision=lax.Precision.HIGHEST)


def _cover(n_cmp, n_sel):
    c0 = np.arange(n_cmp)[:, None] * CMP_STRIDE
    s0 = np.arange(n_sel)[None, :] * SEL_BLOCK
    return np.clip(np.minimum(c0 + CMP_LEN, s0 + SEL_BLOCK) - np.maximum(c0, s0), 0, None) / CMP_STRIDE


def _prompt_tables(tbl, t):
    n_q = t // TQ
    n_chunks = t // CMP_STRIDE
    ar = jnp.arange(TQ, dtype=jnp.int32)
    lut = _bias_lut(tbl)
    far = lut[MAX_DISTANCE]
    assert MAX_DISTANCE <= TQ

    def key_major(b):
        k, q = b.shape[1:]
        return b.reshape(N_KV, GROUP, k, q).transpose(0, 2, 1, 3).reshape(N_KV, k, GROUP * q)

    near = []
    for c in range(2):
        dist = c * TQ + ar[None, :] - ar[:, None]
        b = (_bias(dist, lut) - far[:, None, None]) * LOG2E
        near.append(key_major(jnp.where(dist[None] >= 0, b, NEG)))
    oldest = jnp.where(ar[:, None] > ar[None, :], 0.0, NEG)
    near.append(key_major(jnp.broadcast_to(oldest[None], (N_HEADS, TQ, TQ))))
    lo = (n_chunks - 1) * CMP_STRIDE + (CMP_LEN - 1)
    ext = jnp.concatenate([jnp.broadcast_to(lut[:1], (lo, N_HEADS)), lut,
                           jnp.broadcast_to(lut[-1:], (t - MAX_DISTANCE, N_HEADS))], axis=0).T * LOG2E
    b = jnp.stack([lax.slice_in_dim(ext, lo - n * CMP_STRIDE - (CMP_LEN - 1), lo - n * CMP_STRIDE - (CMP_LEN - 1) + t,
                                    axis=1) for n in range(n_chunks)], axis=1)
    b = b.reshape(N_KV, GROUP, n_chunks, n_q, TQ).transpose(0, 3, 2, 1, 4).reshape(N_KV, n_q, n_chunks, GROUP * TQ)
    n_cmp = (t - CMP_LEN) // CMP_STRIDE + 1
    n_sel = -(-t // SEL_BLOCK)
    covt = np.zeros((LANES, n_chunks), np.float32)
    covt[:n_sel, :n_cmp] = _cover(n_cmp, n_sel).T
    keys = np.arange(-TQ, t)
    onehot = (keys[:, None] // SEL_BLOCK == np.arange(LANES)[None, :]) & (keys[:, None] >= 0)
    onehot[:, LANES - 1] = keys < 0
    wkeys = np.arange(-WINDOW, t)
    wflag = np.zeros((WINDOW + t, LANES), np.float32)
    wflag[:, 0] = wkeys < 0
    diag, prev, oldest = near
    none = jnp.zeros_like(prev)
    tabs = dict(sel=jnp.concatenate([none, prev, diag], axis=1), win_a=jnp.concatenate([oldest, none], axis=1)[0],
                win_b=jnp.concatenate([none, prev], axis=1), diag=diag)
    return dict(**tabs, bcmp=b, covt=jnp.asarray(covt, BF16),
                sel_onehot=jnp.asarray(onehot.astype(np.float32), BF16), win_flag=jnp.asarray(wflag, BF16))


def _sample_tables(tbl, t_new, pos0, n_win, n_pad):
    n_chunks = pos0 // CMP_STRIDE
    n_cmp = n_chunks - 1
    n_sel = -(-(pos0 + t_new) // SEL_BLOCK)
    selw = -(-n_sel // LANES) * LANES
    qpos = pos0 + jnp.arange(t_new, dtype=jnp.int32)
    lut = _bias_lut(tbl)

    def rows(dist):
        b = _bias(dist, lut) * LOG2E
        return b.reshape(N_KV, GROUP, t_new, -1).transpose(1, 0, 2, 3).reshape(N_HEADS * t_new, -1)

    blk_end = jnp.arange(n_chunks, dtype=jnp.int32) * CMP_STRIDE + (CMP_LEN - 1)
    new_pos = pos0 + jnp.arange(n_pad, dtype=jnp.int32)
    cov = np.zeros((n_chunks, selw), np.float32)
    cov[:n_cmp, :n_sel] = _cover(n_cmp, n_sel)
    keys = PAGES_PER_STEP * PAGE_SIZE
    expand = (np.arange(LANES)[:, None] == np.arange(keys)[None, :] // SEL_BLOCK).astype(np.float32)
    bnew = rows(qpos[:, None] - new_pos[None, :])
    return dict(
        bcmp=rows(qpos[:, None] - blk_end[None, :]),
        bwc=rows(qpos[:, None] - (pos0 - n_win + jnp.arange(n_win, dtype=jnp.int32))[None, :]),
        bwn=bnew, bseln=bnew,
        bsel=rows(qpos[:, None] - jnp.arange(pos0, dtype=jnp.int32)[None, :]),
        cov=jnp.asarray(cov, BF16), expand=jnp.asarray(expand, BF16))


def _layer_weights(l, norm_mix, w_in, conv_w, q_norm, k_norm, cmp_pe, cmp_w1, cmp_w2, out_norm_conv,
                   out_norm_att, w_out, norm_ffn, w_gate, w_up, w_down, d_conv, d_att, tm_p):
    perm = np.arange(d_att).reshape(N_KV, GROUP, HEAD_DIM).transpose(1, 0, 2).reshape(-1)
    q0 = 3 * d_conv
    w = w_in[l]
    k0 = q0 + d_att
    n_gates = w.shape[1] - (k0 + 6 * D_KV)
    seg = lambda i: w[:, k0 + i * D_KV:k0 + (i + 1) * D_KV]
    gates = jnp.concatenate([w[:, k0 + 6 * D_KV:], jnp.zeros((w.shape[0], GATE_ROWS - n_gates), w.dtype)], axis=1)
    w_tok = jnp.concatenate([w[:, :q0], seg(0), seg(1), seg(2), seg(4)], axis=1).astype(BF16)
    w_chan = jnp.concatenate([w[:, q0:k0][:, perm], seg(3), seg(5), gates], axis=1).T.astype(BF16)
    wo = w_out[l]
    wo = jnp.concatenate([wo[:d_conv], wo[d_conv:][perm]], axis=0).astype(BF16)
    pair = lambda v: jnp.tile(v, 2)[None, :]
    q_gain = q_norm[l] * (SCALE * LOG2E)
    eye2 = jnp.eye(N_KV, dtype=F32)

    def cmp_weights(i, gain):
        w1 = cmp_w1[l, i]
        w2 = cmp_w2[l, i]
        wab = jnp.einsum("hrde,gk->rgdhke", w1.reshape(2, CMP_STRIDE, HEAD_DIM, CMP_HID), eye2)
        return dict(
            wab=wab.reshape(CMP_STRIDE * D_KV, 2 * N_KV * CMP_HID).astype(BF16),
            pe=jnp.broadcast_to(cmp_pe[l, i].reshape(1, -1), (8, CMP_LEN * HEAD_DIM)),
            w1f=w1.reshape(CMP_LEN * HEAD_DIM, CMP_HID),
            w2bd=jnp.einsum("ed,gk->gekd", w2, eye2).reshape(N_KV * CMP_HID, D_KV).astype(BF16),
            gain=gain)

    return dict(
        norm_mix=norm_mix[l][None, :], w_tok=w_tok, w_chan=w_chan,
        q_gain_t=jnp.broadcast_to(q_gain[:, None], (HEAD_DIM, tm_p)),
        k_gain=jnp.stack([jnp.tile(k_norm[l, 1], 2), jnp.tile(k_norm[l, 2], 2)]),
        cmp_k=cmp_weights(0, pair(k_norm[l, 0])), cmp_v=cmp_weights(1, jnp.ones((1, LANES), F32)),
        conv_w=jnp.pad(conv_w[l], ((0, 8 - CONV_WIDTH), (0, 0))),
        norm_conv=out_norm_conv[l][None, :], norm_att=out_norm_att[l][perm][None, :], w_out=wo,
        norm_ffn=norm_ffn[l][None, :],
        w_gate=w_gate[l].astype(BF16), w_up=w_up[l].astype(BF16), w_down=w_down[l].astype(BF16))


def _shifted(u, hist):
    up = jnp.concatenate([hist, u], axis=1)
    t = u.shape[1]
    return up[:, 1:1 + t], up[:, 0:t]


def _chan_major(a):
    a = jnp.moveaxis(a, -3, -1)
    return a.reshape(a.shape[:-3] + (D_KV, a.shape[-1]))


def _token_major(a_t):
    a = a_t.reshape(a_t.shape[:-2] + (N_KV, HEAD_DIM, a_t.shape[-1]))
    return jnp.moveaxis(a, -1, -3)


def kernel(x_prompt, x_sample, state_conv, cache_cmp_k, cache_cmp_v, cache_slc_k, cache_slc_v,
           cache_win_k, cache_win_v, page_table, rel_bias, norm_mix, w_in, conv_w, q_norm, k_norm,
           cmp_pe, cmp_w1, cmp_w2, out_norm_conv, out_norm_att, w_out, norm_ffn, w_gate, w_up, w_down):
    nb, t, dm = x_prompt.shape
    ns, t_new, _ = x_sample.shape
    depth = w_in.shape[0]
    d_conv = conv_w.shape[2]
    d_att = dm - d_conv
    n_pool = cache_cmp_k.shape[1]
    n_pages = page_table.shape[1]
    pos0 = n_pages * PAGE_SIZE
    n_win = cache_win_k.shape[2]
    assert t % TQ == 0 and d_att == N_HEADS * HEAD_DIM
    assert n_pages % PAGES_PER_STEP == 0 and n_pages % CMP_PAGES_PER_STEP == 0
    assert n_win == WINDOW and WINDOW == 4 * TQ and pos0 >= WINDOW and t_new <= LANES and t_new & (t_new - 1) == 0
    assert t // CMP_STRIDE == LANES and SEL_BLOCK == 64 and t >= WINDOW and t // SEL_BLOCK < LANES - 1

    tm_p = 512
    kv = lambda a, b_, t_: a.reshape(b_, t_, N_KV, HEAD_DIM)
    ptab = _prompt_tables(rel_bias, t)
    stab = _sample_tables(rel_bias, t_new, pos0, n_win, LANES)
    chan_pool = lambda c: _chan_major(c).reshape(depth * n_pool, D_KV, PAGE_SIZE)
    cmp_pools = [chan_pool(cache_cmp_k), chan_pool(cache_cmp_v)]
    slc_pools = [chan_pool(cache_slc_k), chan_pool(cache_slc_v)]
    win_t = [_chan_major(cache_win_k), _chan_major(cache_win_v)]
    new_t = lambda a: jnp.pad(a.reshape(ns, t_new, LANES).transpose(0, 2, 1), ((0, 0), (0, 0), (0, LANES - t_new)))
    eye = jnp.eye(N_KV, dtype=BF16)[None, None, :, None, :, None]
    other = (jnp.arange(LANES)[None, :] // HEAD_DIM != jnp.arange(N_KV)[:, None])[None, :, :, None]

    xp = x_prompt.reshape(nb * t, dm)
    xs = x_sample.reshape(ns * t_new, dm)
    st_p, st_s = [], []
    for l in range(depth):
        lw = _layer_weights(l, norm_mix, w_in, conv_w, q_norm, k_norm, cmp_pe, cmp_w1, cmp_w2, out_norm_conv,
                            out_norm_att, w_out, norm_ffn, w_gate, w_up, w_down, d_conv, d_att, tm_p)

        gb, u, kc, vc, ks, kw, q_t, vs_t, vw_t, gates_t = _in_proj_t(
            xp, lw["norm_mix"], lw["w_tok"], lw["w_chan"], lw["q_gain_t"], lw["k_gain"], nb, d_conv, d_att, tm_p)
        seq = lambda a: a.reshape(nb, t, LANES)
        kcmp = _compress_rows(seq(kc), lw["cmp_k"], True, False)
        vcmp_t = _compress_rows(seq(vc), lw["cmp_v"], False, True)
        ks_aug = jnp.concatenate(
            [jnp.pad(seq(ks).astype(BF16), ((0, 0), (TQ, 0), (0, 0))),
             jnp.broadcast_to(ptab["sel_onehot"][None], (nb, TQ + t, LANES))], axis=2)
        kw_aug = jnp.concatenate(
            [jnp.pad(seq(kw).astype(BF16), ((0, 0), (WINDOW, 0), (0, 0))),
             jnp.broadcast_to(ptab["win_flag"][None], (nb, WINDOW + t, LANES))], axis=2)
        ones_rows = lambda v_t, front: jnp.where(
            other, 1.0, jnp.pad(v_t, ((0, 0), (0, 0), (front, 0)))[:, None]).astype(BF16)
        o_t = _pattn(q_t, gates_t, kcmp, vcmp_t, ks_aug, ones_rows(vs_t, TQ), kw_aug, ones_rows(vw_t, WINDOW), ptab)
        o_att = o_t.reshape(nb * t, d_att)
        u3 = u.reshape(nb, t, d_conv)
        xp = _mix_ffn(xp, gb, u, u, u, o_att, lw, tm_p, tiles_per_seq=t // tm_p)
        n_keep = min(WINDOW, t)
        st_p.append((u3[:, -(CONV_WIDTH - 1):], kv(kc, nb, t), kv(vc, nb, t), kv(ks, nb, t), _token_major(vs_t),
                     kv(kw, nb, t)[:, -n_keep:], _token_major(vw_t[:, :, -n_keep:])))

        n_s = ns * t_new
        gb, u, kc, vc, ks, kw, q_t, vs_t, vw_t, gates_t = _in_proj_t(
            xs, lw["norm_mix"], lw["w_tok"], lw["w_chan"], lw["q_gain_t"][:, :n_s], lw["k_gain"], 1, d_conv, d_att, n_s)
        q, vs, vw, gates = q_t[0].T, vs_t[0].T, vw_t[0].T, gates_t[0].T
        base = l * n_pool
        kcmp = _compress_paged(cmp_pools[0], page_table, base, lw["cmp_k"], True)
        vcmp = _compress_paged(cmp_pools[1], page_table, base, lw["cmp_v"], False)
        qr = q.reshape(ns, t_new, GROUP, N_KV, HEAD_DIM).transpose(0, 2, 3, 1, 4)
        qm = (qr[:, :, :, :, None, :] * eye).reshape(ns, N_HEADS * t_new, LANES)
        gr = gates[:, :3 * N_HEADS].reshape(ns, t_new, 3, N_KV, GROUP).transpose(0, 2, 4, 3, 1)
        gr = jnp.broadcast_to(gr.reshape(ns, 3, N_HEADS * t_new, 1), (ns, 3, N_HEADS * t_new, LANES))
        kw_t, vw_t = new_t(kw), new_t(vw)
        o = _sattn(qm, kcmp, vcmp, win_t[0][l], win_t[1][l], kw_t, vw_t, new_t(ks), new_t(vs),
                   slc_pools[0], slc_pools[1], page_table, base, gr, stab, t_new, pos0)
        o6 = o.reshape(ns, GROUP, N_KV, t_new, N_KV, HEAD_DIM)
        o_att = jnp.stack([o6[:, :, g, :, g] for g in range(N_KV)], axis=2)
        o_att = o_att.transpose(0, 3, 1, 2, 4).reshape(ns * t_new, d_att)
        u3 = u.reshape(ns, t_new, d_conv)
        u1, u2 = _shifted(u3, state_conv[l])
        xs = _mix_ffn(xs, gb, u, u1.reshape(-1, d_conv), u2.reshape(-1, d_conv), o_att, lw, ns * t_new)
        new_conv = jnp.concatenate([state_conv[l], u3], axis=1)[:, -(CONV_WIDTH - 1):]
        n_keep = min(WINDOW, n_win + t_new)
        keep = lambda c_t, n_t: _token_major(jnp.concatenate([c_t, n_t[:, :, :t_new]], axis=2)[:, :, -n_keep:])
        st_s.append((new_conv, kv(kc, ns, t_new), kv(vc, ns, t_new), kv(ks, ns, t_new), kv(vs, ns, t_new),
                     keep(win_t[0][l], kw_t), keep(win_t[1][l], vw_t)))

    outs_p = [jnp.stack(a) for a in zip(*st_p)]
    outs_s = [jnp.stack(a) for a in zip(*st_s)]
    return (xp.reshape(nb, t, dm), xs.reshape(ns, t_new, dm), *outs_p, *outs_s)
```

```python
import functools
import math

import jax
import jax.numpy as jnp
import numpy as np
from jax import lax
from jax.experimental import pallas as pl
from jax.experimental.pallas import tpu as pltpu

F32 = jnp.float32
BF16 = jnp.bfloat16

HEAD_DIM = 64
N_KV = 2
GROUP = 4
N_HEADS = N_KV * GROUP
D_KV = N_KV * HEAD_DIM
CMP_LEN = 32
CMP_STRIDE = 16
CMP_HID = 2 * HEAD_DIM
SEL_BLOCK = 64
SEL_TOPK = 16
WINDOW = 512
NUM_BUCKETS = 32
MAX_DISTANCE = 128
PAGE_SIZE = 128
CONV_WIDTH = 3
EPS = 1e-6
SCALE = HEAD_DIM ** -0.5
LOG2E = math.log2(math.e)
NEG = -1e30

LANES = 128
TQ = 128
GATE_ROWS = 32
FF_CHUNK = 256
CMP_PAGES_PER_STEP = 64
PAGES_PER_STEP = 32
VMEM_LIMIT = 56 * 1024 * 1024

_NT = (((1,), (1,)), ((), ()))


def _dot(a, b):
    return jnp.dot(a, b, preferred_element_type=F32)


def _dot_nt(a, b):
    return lax.dot_general(a, b, _NT, preferred_element_type=F32)


def _params(n_axes):
    return pltpu.CompilerParams(dimension_semantics=("arbitrary",) * n_axes,
                                vmem_limit_bytes=VMEM_LIMIT)


def _resident(shape):
    nd = len(shape)
    return pl.BlockSpec(shape, lambda *_: (0,) * nd, pipeline_mode=pl.Buffered(1))


def _rms(x, gain):
    return x * lax.rsqrt(jnp.mean(x * x, axis=-1, keepdims=True) + EPS) * gain


def _pair_rms(y, gain, lo):
    y2 = y * y
    s_lo = jnp.sum(jnp.where(lo, y2, 0.0), axis=-1, keepdims=True)
    s_hi = jnp.sum(jnp.where(lo, 0.0, y2), axis=-1, keepdims=True)
    inv = jnp.where(lo, lax.rsqrt(s_lo / HEAD_DIM + EPS), lax.rsqrt(s_hi / HEAD_DIM + EPS))
    return y * inv * gain


def _in_proj_t_kernel(x_ref, nm_ref, w_ref, wt_ref, qgt_ref, kg_ref,
                      gb_ref, u_ref, kc_ref, vc_ref, ks_ref, kw_ref, qt_ref, vst_ref, vwt_ref, gtt_ref):
    x = x_ref[...]
    hb = _rms(x, nm_ref[...]).astype(BF16)
    d = gb_ref.shape[1]

    def seg(a, b):
        return _dot(hb, w_ref[:, a:b])

    gb_ref[...] = seg(0, d)
    u_ref[...] = seg(d, 2 * d) * seg(2 * d, 3 * d)
    lo = lax.broadcasted_iota(jnp.int32, (x.shape[0], LANES), 1) < HEAD_DIM
    c = 3 * d
    kc_ref[...] = seg(c, c + LANES)
    vc_ref[...] = seg(c + LANES, c + 2 * LANES)
    ks_ref[...] = _pair_rms(seg(c + 2 * LANES, c + 3 * LANES), kg_ref[0:1, :], lo)
    kw_ref[...] = _pair_rms(seg(c + 3 * LANES, c + 4 * LANES), kg_ref[1:2, :], lo)

    yt = _dot_nt(wt_ref[...], hb)
    d_att = qt_ref.shape[1]
    for h in range(d_att // HEAD_DIM):
        y = yt[h * HEAD_DIM:(h + 1) * HEAD_DIM]
        inv = lax.rsqrt(jnp.mean(y * y, axis=0, keepdims=True) + EPS)
        qt_ref[0, h * HEAD_DIM:(h + 1) * HEAD_DIM, :] = (y * inv * qgt_ref[...]).astype(BF16)
    vst_ref[0] = yt[d_att:d_att + LANES]
    vwt_ref[0] = yt[d_att + LANES:d_att + 2 * LANES]
    gtt_ref[0] = jax.nn.sigmoid(yt[d_att + 2 * LANES:d_att + 2 * LANES + GATE_ROWS])


def _in_proj_t(x, nm, w, wt, qgt, kg, nb, d_conv, d_att, tm):
    n, dm = x.shape
    t = n // nb
    per_b = t // tm
    row = lambda w_: pl.BlockSpec((tm, w_), lambda i: (i, 0))
    col = lambda r: pl.BlockSpec((1, r, tm), lambda i: (i // per_b, 0, i % per_b))
    widths = [d_conv, d_conv] + [LANES] * 4
    rows_t = [d_att, LANES, LANES, GATE_ROWS]
    dt_t = [BF16, F32, F32, F32]
    return pl.pallas_call(
        _in_proj_t_kernel,
        grid=(n // tm,),
        in_specs=[row(dm), _resident(nm.shape), _resident(w.shape), _resident(wt.shape),
                  _resident(qgt.shape), _resident(kg.shape)],
        out_specs=[row(w_) for w_ in widths] + [col(r) for r in rows_t],
        out_shape=[jax.ShapeDtypeStruct((n, w_), F32) for w_ in widths]
        + [jax.ShapeDtypeStruct((nb, r, t), dt) for r, dt in zip(rows_t, dt_t)],
        compiler_params=_params(1),
        name="in_proj_t",
    )(x, nm, w, wt, qgt, kg)


def _cmp_ab_kernel(x_ref, w_ref, a_ref, b_ref):
    n_chunks = x_ref.shape[1] // CMP_STRIDE
    cols = [x_ref[0, pl.ds(l, n_chunks, stride=CMP_STRIDE), :] for l in range(CMP_STRIDE)]
    xr = jnp.concatenate(cols, axis=1).astype(BF16)
    ab = _dot(xr, w_ref[...])
    half = ab.shape[1] // 2
    a_ref[0] = ab[:, :half]
    b_ref[0] = ab[:, half:]


def _cmp_ab_paged_kernel(pt_ref, *refs, n_in):
    x_refs = refs[:n_in]
    perm_ref, w_ref, a_ref, b_ref = refs[n_in:]
    per_page = PAGE_SIZE // CMP_STRIDE
    rows = []
    for xr in x_refs:
        xp = _dot_nt(perm_ref[...], xr[0].astype(BF16))
        rows.append(jnp.concatenate([xp[l * per_page:(l + 1) * per_page] for l in range(CMP_STRIDE)], axis=1))
    xr = jnp.concatenate(rows, axis=0).astype(BF16)
    ab = _dot(xr, w_ref[...])
    half = ab.shape[1] // 2
    a_ref[0] = ab[:, :half]
    b_ref[0] = ab[:, half:]


def _cmp_fin_kernel(a_ref, b_ref, pe_ref, w1_ref, w2_ref, g_ref, o_ref, *, n_valid, normalize, transpose_out):
    a = a_ref[0]
    bm = b_ref[0]
    n_chunks = a.shape[0]
    b_next = pltpu.roll(bm, n_chunks - 1, 0)
    cpe = jnp.dot(pe_ref[...], w1_ref[...], preferred_element_type=F32,
                  precision=lax.Precision.HIGHEST)[0:1]
    x = a + b_next + jnp.concatenate([cpe, cpe], axis=1)
    hid = x * (0.5 * (1.0 + jnp.tanh(math.sqrt(2.0 / math.pi) * (x + 0.044715 * (x * x * x)))))
    out = _dot(hid.astype(BF16), w2_ref[...])
    if normalize:
        lo = lax.broadcasted_iota(jnp.int32, out.shape, 1) < HEAD_DIM
        out = _pair_rms(out, g_ref[...], lo)
    rows = lax.broadcasted_iota(jnp.int32, out.shape, 0)
    out = jnp.where(rows < n_valid, out, 0.0)
    o_ref[0] = out.T if transpose_out else out


def _cmp_fin(a, b, cw, n_valid, normalize, transpose_out=False):
    nb, n_chunks, wid = a.shape
    blk = pl.BlockSpec((1, n_chunks, wid), lambda i: (i, 0, 0))
    oshape = (LANES, n_chunks) if transpose_out else (n_chunks, LANES)
    return pl.pallas_call(
        functools.partial(_cmp_fin_kernel, n_valid=n_valid, normalize=normalize, transpose_out=transpose_out),
        grid=(nb,),
        in_specs=[blk, blk, _resident(cw["pe"].shape), _resident(cw["w1f"].shape),
                  _resident(cw["w2bd"].shape), _resident(cw["gain"].shape)],
        out_specs=pl.BlockSpec((1,) + oshape, lambda i: (i, 0, 0)),
        out_shape=jax.ShapeDtypeStruct((nb,) + oshape, F32),
        compiler_params=_params(1),
        name="cmp_fin",
    )(a, b, cw["pe"], cw["w1f"], cw["w2bd"], cw["gain"])


def _compress_rows(rows, cw, normalize, transpose_out):
    nb, t, _ = rows.shape
    n_chunks = t // CMP_STRIDE
    wid = cw["wab"].shape[1] // 2
    out_blk = pl.BlockSpec((1, n_chunks, wid), lambda i: (i, 0, 0))
    a, b = pl.pallas_call(
        _cmp_ab_kernel,
        grid=(nb,),
        in_specs=[pl.BlockSpec((1, t, LANES), lambda i: (i, 0, 0)), _resident(cw["wab"].shape)],
        out_specs=[out_blk, out_blk],
        out_shape=[jax.ShapeDtypeStruct((nb, n_chunks, wid), F32)] * 2,
        compiler_params=_params(1),
        name="cmp_ab_rows",
    )(rows, cw["wab"])
    return _cmp_fin(a, b, cw, (t - CMP_LEN) // CMP_STRIDE + 1, normalize, transpose_out)


def _page_spec(i, base, pages_per_step):
    return pl.BlockSpec((1, PAGE_SIZE, LANES),
                        lambda b, s, pt: (base + pt[b, s * pages_per_step + i], 0, 0))


def _compress_paged(cache, page_table, base, cw, normalize):
    nb, n_pages = page_table.shape
    pps = CMP_PAGES_PER_STEP
    per_page = PAGE_SIZE // CMP_STRIDE
    ch = pps * per_page
    wid = cw["wab"].shape[1] // 2
    n_chunks = n_pages * per_page
    out_blk = pl.BlockSpec((1, ch, wid), lambda b, s, pt: (b, s, 0))
    r = np.arange(PAGE_SIZE)
    perm = jnp.asarray(r[None, :] == (r % per_page)[:, None] * CMP_STRIDE + (r // per_page)[:, None], BF16)
    cst = lambda a: pl.BlockSpec(a.shape, lambda b, s, pt: (0, 0), pipeline_mode=pl.Buffered(1))
    a, b = pl.pallas_call(
        functools.partial(_cmp_ab_paged_kernel, n_in=pps),
        grid_spec=pltpu.PrefetchScalarGridSpec(
            num_scalar_prefetch=1,
            grid=(nb, n_pages // pps),
            in_specs=[_page_spec(i, base, pps) for i in range(pps)] + [cst(perm), cst(cw["wab"])],
            out_specs=[out_blk, out_blk]),
        out_shape=[jax.ShapeDtypeStruct((nb, n_chunks, wid), F32)] * 2,
        compiler_params=_params(2),
        name="cmp_ab_paged",
    )(page_table, *([cache] * pps), perm, cw["wab"])
    return _cmp_fin(a, b, cw, n_chunks - 1, normalize)


def _split_bf16(x):
    hi = x.astype(BF16)
    return hi, (x - hi.astype(F32)).astype(BF16)


def _online_update(state, s, mask, v_t):
    m, l, acc = state
    s = jnp.where(mask, s, NEG)
    m_new = jnp.maximum(m, jnp.max(s, axis=-1, keepdims=True))
    alpha = jnp.exp2(m - m_new)
    e = jnp.where(mask, jnp.exp2(s - m_new), 0.0)
    l = alpha * l + jnp.sum(e, axis=-1, keepdims=True)
    acc = alpha * acc + _dot_nt(e.astype(BF16), v_t)
    return m_new, l, acc


def _finish(state):
    _, l, acc = state
    return acc / jnp.where(l > 0, l, 1.0)


def _reset_t(m_ref, acc_ref):
    m_ref[...] = jnp.full(m_ref.shape, NEG, F32)
    acc_ref[...] = jnp.zeros(acc_ref.shape, F32)


def _update_t(m_ref, acc_ref, s_t, v_t):
    m = m_ref[...]
    m_new = jnp.maximum(m, jnp.max(s_t, axis=0, keepdims=True))
    alpha = jnp.exp2(m - m_new)
    e = jnp.exp2(s_t - m_new).astype(BF16)
    acc_ref[...] = alpha * acc_ref[...] + _dot(v_t, e)
    m_ref[...] = m_new


def _pattn_kernel(qt_ref, gt_ref, kcmp_ref, vcmpt_ref, ksa_ref, vst_ref, kwa_ref, vwt_ref,
                  bcmp_ref, stab_ref, wa_ref, wb_ref, diag_ref, covt_ref, o_ref, m_ref, acc_ref, mw_ref, accw_ref, qa_ref, out_ref, *, n_sel, topk):
    qi = pl.program_id(1)
    q0 = qi * TQ
    n_slabs = qt_ref.shape[1] // LANES
    wide = n_slabs * TQ
    sub = lax.broadcasted_iota(jnp.int32, (LANES, TQ), 0)
    lane = lax.broadcasted_iota(jnp.int32, (LANES, TQ), 1)
    sub_w = lax.broadcasted_iota(jnp.int32, (LANES, wide), 0)
    tpos_w = q0 + (lax.broadcasted_iota(jnp.int32, (LANES, wide), 1) & (TQ - 1))
    kcmp = kcmp_ref[0].astype(BF16)
    vcmp_t = vcmpt_ref[0].astype(BF16)
    covt = covt_ref[...]
    gt = gt_ref[0]
    zero_rows = jnp.zeros((LANES - n_sel, TQ), F32)
    def gate(br, g):
        return jnp.concatenate([gt[br * N_HEADS + g * GROUP + j:br * N_HEADS + g * GROUP + j + 1, :]
                                for j in range(n_slabs)], axis=1)

    def keys(k_ref, v_ref, g, kt, n):
        start = pl.multiple_of(kt * TQ, TQ)
        return k_ref[0, pl.ds(start, n * TQ), :], v_ref[0, g, :, pl.ds(start, n * TQ)]

    def add_branch(br, acc_ref_):
        for g in range(N_KV):
            acc = acc_ref_[g]
            den = acc[HEAD_DIM:HEAD_DIM + 1] if g == 0 else acc[0:1]
            out_ref[g] += (gate(br, g) * (1.0 / den)) * acc

    q_ts = []
    for g in range(N_KV):
        mine = (sub < HEAD_DIM) if g == 0 else (sub >= HEAD_DIM)
        q_ts.append(jnp.concatenate(
            [jnp.where(mine, qt_ref[0, j * LANES:(j + 1) * LANES, :], jnp.zeros((), BF16))
             for j in range(n_slabs)], axis=1))
        qa_ref[g, 0:LANES, :] = q_ts[g]

    pad_pen = jnp.where(sub_w == 0, NEG, 0.0).astype(BF16)
    win = []
    for g in range(N_KV):
        q_win = jnp.concatenate([q_ts[g], pad_pen], axis=0)
        jobs = []
        for kt, n, tab in ((qi, 2, wa_ref[...]), (qi + 2, 2, wb_ref[g]), (qi + 4, 1, diag_ref[g])):
            k_, v_ = keys(kwa_ref, vwt_ref, g, kt, n)
            jobs.append((_dot(k_, q_win) + tab, v_))
        win.append(jobs)
    shift = qi * (TQ // CMP_STRIDE)
    s_cmp = [_dot(kcmp, q_ts[g]) + pltpu.roll(bcmp_ref[g], shift, 0) for g in range(N_KV)]

    for g in range(N_KV):
        mask = sub_w * CMP_STRIDE + (CMP_LEN - 1) <= tpos_w
        s = jnp.where(mask, s_cmp[g], NEG)
        e = jnp.where(mask, jnp.exp2(s - jnp.max(s, axis=0, keepdims=True)), 0.0)
        l = jnp.sum(e, axis=0, keepdims=True)
        p = e * (1.0 / jnp.where(l > 0, l, 1.0))
        out_ref[g] = gate(0, g) * _dot(vcmp_t, p.astype(BF16))
        p_sum = p[:, 0:TQ]
        for j in range(1, n_slabs):
            p_sum = p_sum + p[:, j * TQ:(j + 1) * TQ]

        hi, lo_ = _split_bf16(p_sum)
        sc = _dot(covt, hi) + _dot(covt, lo_)
        tq_pos = q0 + lane
        cur = tq_pos >> 6
        forced = (sub == 0) | (sub == cur) | (sub == cur - 1)
        sc = jnp.where(forced, 1e6, jnp.where(sub * SEL_BLOCK > tq_pos, -1e6, sc))
        sc = sc[0:n_sel]
        blk = sub[0:n_sel]
        rank = jnp.zeros((n_sel, TQ), F32)
        for i in range(n_sel):
            si = sc[i:i + 1, :]
            beats = (si > sc) | ((si == sc) & (blk > i))
            rank = rank + jnp.where(beats, 1.0, 0.0)
        pen = jnp.concatenate([jnp.where(rank < topk, 0.0, NEG), zero_rows], axis=0)
        pen = jnp.where(sub == LANES - 1, NEG, pen).astype(BF16)
        qa_ref[g, LANES:2 * LANES, :] = jnp.concatenate([pen] * n_slabs, axis=1)

    _reset_t(mw_ref, accw_ref)
    for job in range(len(win[0])):
        for g in range(N_KV):
            _update_t(mw_ref.at[g], accw_ref.at[g], *win[g][job])

    _reset_t(m_ref, acc_ref)
    n_far = jnp.maximum(qi - 1, 0)

    @pl.loop(0, n_far // 2)
    def _(i):
        kv = [keys(ksa_ref, vst_ref, g, 1 + 2 * i, 2) for g in range(N_KV)]
        s = [_dot(kv[g][0], qa_ref[g]) for g in range(N_KV)]
        for g in range(N_KV):
            _update_t(m_ref.at[g], acc_ref.at[g], s[g], kv[g][1])

    odd = jnp.where(n_far % 2 == 1, n_far, 0)
    last = []
    for g in range(N_KV):
        k0, v0 = keys(ksa_ref, vst_ref, g, odd, 1)
        k1, v1 = keys(ksa_ref, vst_ref, g, qi, 2)
        last.append((_dot(jnp.concatenate([k0, k1], axis=0), qa_ref[g]) + stab_ref[g],
                     jnp.concatenate([v0, v1], axis=1)))
    for g in range(N_KV):
        _update_t(m_ref.at[g], acc_ref.at[g], *last[g])
    add_branch(1, acc_ref)
    add_branch(2, accw_ref)

    out = jnp.where(sub_w < HEAD_DIM, out_ref[0], out_ref[1])
    for j in range(n_slabs):
        o_ref[0, :, j * LANES:(j + 1) * LANES] = out[:, j * TQ:(j + 1) * TQ].T


def _pattn(q_t, gates_t, kcmp, vcmp_t, ks_aug, vs_t, kw_aug, vw_t, tabs):
    nb, d_att, t = q_t.shape
    n_q = t // TQ
    n_sel = -(-t // SEL_BLOCK)
    n_slabs = d_att // LANES
    whole = lambda a: pl.BlockSpec((1,) + a.shape[1:], lambda b, i: (b,) + (0,) * (a.ndim - 1))
    return pl.pallas_call(
        functools.partial(_pattn_kernel, n_sel=n_sel, topk=min(SEL_TOPK, n_sel)),
        grid=(nb, n_q),
        in_specs=[pl.BlockSpec((1, d_att, TQ), lambda b, i: (b, 0, i)),
                  pl.BlockSpec((1, GATE_ROWS, TQ), lambda b, i: (b, 0, i)),
                  whole(kcmp), whole(vcmp_t), whole(ks_aug), whole(vs_t), whole(kw_aug), whole(vw_t),
                  _resident(tabs["bcmp"].shape),
                  _resident(tabs["sel"].shape), _resident(tabs["win_a"].shape), _resident(tabs["win_b"].shape),
                  _resident(tabs["diag"].shape), _resident(tabs["covt"].shape)],
        out_specs=pl.BlockSpec((1, TQ, d_att), lambda b, i: (b, i, 0)),
        out_shape=jax.ShapeDtypeStruct((nb, t, d_att), F32),
        scratch_shapes=[pltpu.VMEM((N_KV, 1, n_slabs * TQ), F32), pltpu.VMEM((N_KV, LANES, n_slabs * TQ), F32)] * 2
        + [pltpu.VMEM((N_KV, 2 * LANES, n_slabs * TQ), BF16), pltpu.VMEM((N_KV, LANES, n_slabs * TQ), F32)],
        compiler_params=_params(2),
        name="pattn",
    )(q_t, gates_t, kcmp, vcmp_t, ks_aug, vs_t, kw_aug, vw_t, tabs["bcmp"], tabs["sel"], tabs["win_a"],
      tabs["win_b"], tabs["diag"], tabs["covt"])


def _sattn1_kernel(qm_ref, kcmp_ref, vcmp_ref, kwc_ref, vwc_ref, kwn_ref, vwn_ref,
                   bcmp_ref, bwc_ref, bwn_ref, cov_ref, ocmp_ref, owin_ref, sel_ref,
                   *, n_cmp, n_sel, topk, t_new, pos0, win_start):
    qm = qm_ref[0]
    n_rows = qm.shape[0]
    n_gt = N_KV * t_new
    s = _dot_nt(qm, kcmp_ref[0].astype(BF16)) + bcmp_ref[...]
    ncol = lax.broadcasted_iota(jnp.int32, s.shape, 1)
    trow = pos0 + (lax.broadcasted_iota(jnp.int32, s.shape, 0) & (t_new - 1))
    mask = (ncol < n_cmp) & (ncol * CMP_STRIDE + (CMP_LEN - 1) <= trow)
    s = jnp.where(mask, s, NEG)
    e = jnp.where(mask, jnp.exp2(s - jnp.max(s, axis=-1, keepdims=True)), 0.0)
    l = jnp.sum(e, axis=-1, keepdims=True)
    p = e / jnp.where(l > 0, l, 1.0)
    ocmp_ref[0] = _dot(p.astype(BF16), vcmp_ref[0].astype(BF16))
    p_sum = p[0:n_gt]
    for j in range(1, n_rows // n_gt):
        p_sum = p_sum + p[j * n_gt:(j + 1) * n_gt]
    hi, lo = _split_bf16(p_sum)
    sc = _dot(hi, cov_ref[...]) + _dot(lo, cov_ref[...])
    blk = lax.broadcasted_iota(jnp.int32, sc.shape, 1)
    qpos = pos0 + (lax.broadcasted_iota(jnp.int32, sc.shape, 0) & (t_new - 1))
    cur = qpos >> 6
    forced = (blk == 0) | (blk == cur) | (blk == cur - 1)
    sc = jnp.where(forced, 1e6, jnp.where(blk * SEL_BLOCK > qpos, -1e6, sc))
    sc = jnp.where(blk < n_sel, sc, -3e38)
    rank = jnp.zeros(sc.shape, F32)
    for i in range(n_sel):
        si = sc[:, i:i + 1]
        beats = (si > sc) | ((si == sc) & (blk > i))
        rank = rank + jnp.where(beats, 1.0, 0.0)
    sel_ref[0] = jnp.where((rank < topk) & (blk < n_sel), 1.0, 0.0)

    init = (jnp.full((n_rows, 1), NEG, F32), jnp.zeros((n_rows, 1), F32), jnp.zeros((n_rows, LANES), F32))
    s = _dot(qm, kwc_ref[0, 0].astype(BF16)) + bwc_ref[...]
    trow = pos0 + (lax.broadcasted_iota(jnp.int32, s.shape, 0) & (t_new - 1))
    dist = trow - (win_start + lax.broadcasted_iota(jnp.int32, s.shape, 1))
    st = _online_update(init, s, (dist >= 0) & (dist < WINDOW), vwc_ref[0, 0].astype(BF16))
    s = _dot(qm, kwn_ref[0].astype(BF16)) + bwn_ref[...]
    trow = pos0 + (lax.broadcasted_iota(jnp.int32, s.shape, 0) & (t_new - 1))
    col = lax.broadcasted_iota(jnp.int32, s.shape, 1)
    dist = trow - (pos0 + col)
    st = _online_update(st, s, (dist >= 0) & (dist < WINDOW) & (col < t_new), vwn_ref[0].astype(BF16))
    owin_ref[0] = _finish(st)


def _sattn2_kernel(pt_ref, *refs, pps, t_new):
    k_refs = refs[:pps]
    v_refs = refs[pps:2 * pps]
    (qm_ref, selc_ref, seln_ref, ksn_ref, vsn_ref, bsel_ref, bseln_ref, exp_ref,
     ocmp_ref, owin_ref, gate_ref, o_ref, m_sc, l_sc, acc_sc) = refs[2 * pps:]
    step = pl.program_id(1)
    qm = qm_ref[0]
    n_rows = qm.shape[0]
    reps = n_rows // selc_ref.shape[2]

    @pl.when(step == 0)
    def _():
        s = _dot(qm, ksn_ref[0].astype(BF16)) + bseln_ref[...]
        trow = lax.broadcasted_iota(jnp.int32, s.shape, 0) & (t_new - 1)
        col = lax.broadcasted_iota(jnp.int32, s.shape, 1)
        hit = jnp.concatenate([seln_ref[0]] * reps, axis=0) > 0.5
        init = (jnp.full((n_rows, 1), NEG, F32), jnp.zeros((n_rows, 1), F32),
                jnp.zeros((n_rows, LANES), F32))
        m, l, acc = _online_update(init, s, hit & (col <= trow) & (col < t_new), vsn_ref[0].astype(BF16))
        m_sc[...] = m
        l_sc[...] = l
        acc_sc[...] = acc

    k_t = jnp.concatenate([r[0] for r in k_refs], axis=1).astype(BF16)
    v_t = jnp.concatenate([r[0] for r in v_refs], axis=1).astype(BF16)
    s = _dot(qm, k_t) + bsel_ref[...]
    hit = _dot(selc_ref[0, 0].astype(BF16), exp_ref[...])
    mask = jnp.concatenate([hit] * reps, axis=0) > 0.5
    m, l, acc = _online_update((m_sc[...], l_sc[...], acc_sc[...]), s, mask, v_t)
    m_sc[...] = m
    l_sc[...] = l
    acc_sc[...] = acc

    @pl.when(step == pl.num_programs(1) - 1)
    def _():
        o_sel = _finish((m, l, acc))
        o_ref[0] = gate_ref[0, 0] * ocmp_ref[0] + gate_ref[0, 1] * o_sel + gate_ref[0, 2] * owin_ref[0]


def _sattn(qm, kcmp, vcmp, layer, kwc_t, vwc_t, kwn_t, vwn_t, ksn_t, vsn_t, cache_k, cache_v, page_table, base,
           gates, tabs, t_new, pos0):
    nb, n_rows, _ = qm.shape
    n_pages = page_table.shape[1]
    n_chunks = kcmp.shape[1]
    n_gt = N_KV * t_new
    n_sel = -(-(pos0 + t_new) // SEL_BLOCK)
    selw = tabs["cov"].shape[1]
    per_b = lambda shape: pl.BlockSpec((1,) + shape, lambda b: (b,) + (0,) * len(shape))
    win_c = pl.BlockSpec((1, 1) + kwc_t.shape[2:], lambda b: (layer, b, 0, 0))
    ocmp, owin, sel = pl.pallas_call(
        functools.partial(_sattn1_kernel, n_cmp=n_chunks - 1, n_sel=n_sel, topk=min(SEL_TOPK, n_sel),
                          t_new=t_new, pos0=pos0, win_start=pos0 - kwc_t.shape[3]),
        grid=(nb,),
        in_specs=[per_b((n_rows, LANES)), per_b((n_chunks, LANES)), per_b((n_chunks, LANES)),
                  win_c, win_c, per_b(kwn_t.shape[1:]), per_b(vwn_t.shape[1:]),
                  _resident(tabs["bcmp"].shape), _resident(tabs["bwc"].shape), _resident(tabs["bwn"].shape),
                  _resident(tabs["cov"].shape)],
        out_specs=[per_b((n_rows, LANES)), per_b((n_rows, LANES)), per_b((n_gt, selw))],
        out_shape=[jax.ShapeDtypeStruct((nb, n_rows, LANES), F32), jax.ShapeDtypeStruct((nb, n_rows, LANES), F32),
                   jax.ShapeDtypeStruct((nb, n_gt, selw), F32)],
        compiler_params=_params(1),
        name="sattn1",
    )(qm, kcmp, vcmp, kwc_t, vwc_t, kwn_t, vwn_t, tabs["bcmp"], tabs["bwc"], tabs["bwn"], tabs["cov"])

    pps = PAGES_PER_STEP
    n_steps = n_pages // pps
    bps = pps * PAGE_SIZE // SEL_BLOCK
    n_cblk = n_pages * PAGE_SIZE // SEL_BLOCK
    selc = sel[:, :, :n_cblk].reshape(nb, n_gt, n_steps, bps).transpose(0, 2, 1, 3)
    selc = jnp.pad(selc, ((0, 0), (0, 0), (0, 0), (0, LANES - bps)))
    seln = jnp.broadcast_to(sel[:, :, n_cblk:n_cblk + 1], (nb, n_gt, LANES))

    keys = pps * PAGE_SIZE
    cst = lambda shape: pl.BlockSpec(shape, lambda b, s, pt: (0,) * len(shape), pipeline_mode=pl.Buffered(1))
    pb = lambda shape: pl.BlockSpec((1,) + shape, lambda b, s, pt: (b,) + (0,) * len(shape))
    return pl.pallas_call(
        functools.partial(_sattn2_kernel, pps=pps, t_new=t_new),
        grid_spec=pltpu.PrefetchScalarGridSpec(
            num_scalar_prefetch=1,
            grid=(nb, n_steps),
            in_specs=[_page_spec(i, base, pps) for i in range(pps)] * 2
            + [pb((n_rows, LANES)),
               pl.BlockSpec((1, 1, n_gt, LANES), lambda b, s, pt: (b, s, 0, 0)),
               pb((n_gt, LANES)), pb(ksn_t.shape[1:]), pb(vsn_t.shape[1:]),
               pl.BlockSpec((n_rows, keys), lambda b, s, pt: (0, s)),
               cst(tabs["bseln"].shape), cst(tabs["expand"].shape),
               pb((n_rows, LANES)), pb((n_rows, LANES)), pb((3, n_rows, LANES))],
            out_specs=pb((n_rows, LANES)),
            scratch_shapes=[pltpu.VMEM((n_rows, 1), F32), pltpu.VMEM((n_rows, 1), F32),
                            pltpu.VMEM((n_rows, LANES), F32)]),
        out_shape=jax.ShapeDtypeStruct((nb, n_rows, LANES), F32),
        compiler_params=_params(2),
        name="sattn2",
    )(page_table, *([cache_k] * pps), *([cache_v] * pps), qm, selc, seln, ksn_t, vsn_t,
      tabs["bsel"], tabs["bseln"], tabs["expand"], ocmp, owin, gates)


def _mix_ffn_kernel(x_ref, gb_ref, u_ref, u1_ref, u2_ref, oa_ref, cw_ref, nc_ref, na_ref, wo_ref,
                    nf_ref, wg_ref, wu_ref, wd_ref, y_ref, acc_ref, *, tiles_per_seq):
    d = gb_ref.shape[1]
    u = u_ref[...]
    if tiles_per_seq is None:
        u1, u2 = u1_ref[...], u2_ref[...]
    else:
        prev = u1_ref[...] * jnp.where(pl.program_id(0) % tiles_per_seq == 0, 0.0, 1.0)
        row = lax.broadcasted_iota(jnp.int32, u.shape, 0)
        u1 = jnp.where(row == 0, prev[7:8], pltpu.roll(u, 1, 0))
        u2 = jnp.where(row == 0, prev[6:7], jnp.where(row == 1, prev[7:8], pltpu.roll(u, 2, 0)))
    conv = u2 * cw_ref[0:1, :] + u1 * cw_ref[1:2, :] + u * cw_ref[2:3, :]
    yc = _rms(gb_ref[...] * conv, nc_ref[...]).astype(BF16)
    oa = _rms(oa_ref[...], na_ref[...]).astype(BF16)
    x1 = x_ref[...] + (_dot(yc, wo_ref[0:d, :]) + _dot(oa, wo_ref[d:, :]))
    h2 = _rms(x1, nf_ref[...]).astype(BF16)
    acc_ref[...] = jnp.zeros_like(acc_ref)
    for c in range(0, wg_ref.shape[1], FF_CHUNK):
        gate = _dot(h2, wg_ref[:, c:c + FF_CHUNK])
        act = (gate * jax.nn.sigmoid(gate)) * _dot(h2, wu_ref[:, c:c + FF_CHUNK])
        acc_ref[...] += _dot(act.astype(BF16), wd_ref[c:c + FF_CHUNK, :])
    y_ref[...] = x1 + acc_ref[...]


def _mix_ffn(x, gb, u, u1, u2, oa, lw, tm, tiles_per_seq=None):
    n, dm = x.shape
    d = gb.shape[1]
    row = lambda w_: pl.BlockSpec((tm, w_), lambda i: (i, 0))
    hist = row(d) if tiles_per_seq is None else pl.BlockSpec((8, d), lambda i: (jnp.maximum(i * (tm // 8) - 1, 0), 0))
    consts = [lw["conv_w"], lw["norm_conv"], lw["norm_att"], lw["w_out"], lw["norm_ffn"],
              lw["w_gate"], lw["w_up"], lw["w_down"]]
    return pl.pallas_call(
        functools.partial(_mix_ffn_kernel, tiles_per_seq=tiles_per_seq),
        grid=(n // tm,),
        in_specs=[row(dm), row(d), row(d), hist, hist, row(d)] + [_resident(c.shape) for c in consts],
        out_specs=row(dm),
        out_shape=jax.ShapeDtypeStruct((n, dm), F32),
        scratch_shapes=[pltpu.VMEM((tm, dm), F32)],
        compiler_params=_params(1),
        name="mix_ffn",
    )(x, gb, u, u1, u2, oa, *consts)


def _t5_bucket(dist):
    n = jnp.maximum(dist, 0)
    max_exact = NUM_BUCKETS // 2
    nf = jnp.maximum(n, 1).astype(F32)
    large = max_exact + (jnp.log(nf / max_exact) / math.log(MAX_DISTANCE / max_exact)
                         * (NUM_BUCKETS - max_exact)).astype(jnp.int32)
    return jnp.where(n < max_exact, n, jnp.minimum(large, NUM_BUCKETS - 1))


def _bias_lut(tbl):
    return tbl[_t5_bucket(jnp.arange(MAX_DISTANCE + 1, dtype=jnp.int32))]


def _bias(dist, lut):
    idx = jnp.clip(dist, 0, MAX_DISTANCE)
    onehot = (idx[..., None] == jnp.arange(MAX_DISTANCE + 1, dtype=jnp.int32)).astype(F32)
    return jnp.einsum("...k,kh->h...", onehot, lut, precision=lax.Precision.HIGHEST)


def _cover(n_cmp, n_sel):
    c0 = np.arange(n_cmp)[:, None] * CMP_STRIDE
    s0 = np.arange(n_sel)[None, :] * SEL_BLOCK
    return np.clip(np.minimum(c0 + CMP_LEN, s0 + SEL_BLOCK) - np.maximum(c0, s0), 0, None) / CMP_STRIDE


def _prompt_tables(tbl, t):
    n_q = t // TQ
    n_chunks = t // CMP_STRIDE
    ar = jnp.arange(TQ, dtype=jnp.int32)
    lut = _bias_lut(tbl)
    far = lut[MAX_DISTANCE]
    assert MAX_DISTANCE <= TQ

    def key_major(b):
        k, q = b.shape[1:]
        return b.reshape(N_KV, GROUP, k, q).transpose(0, 2, 1, 3).reshape(N_KV, k, GROUP * q)

    near = []
    for c in range(2):
        dist = c * TQ + ar[None, :] - ar[:, None]
        b = (_bias(dist, lut) - far[:, None, None]) * LOG2E
        near.append(key_major(jnp.where(dist[None] >= 0, b, NEG)))
    oldest = jnp.where(ar[:, None] > ar[None, :], 0.0, NEG)
    near.append(key_major(jnp.broadcast_to(oldest[None], (N_HEADS, TQ, TQ))))
    rel = np.arange(n_chunks)
    rel = np.where(rel < n_chunks // 2, rel, rel - n_chunks)
    dist = np.arange(TQ)[None, :] - (rel[:, None] * CMP_STRIDE + (CMP_LEN - 1))
    dist = np.where(rel[:, None] >= TQ // CMP_STRIDE, MAX_DISTANCE, dist)
    assert (TQ // CMP_STRIDE) * (n_q - 1) < n_chunks and (n_chunks // 2) * CMP_STRIDE >= MAX_DISTANCE + CMP_LEN
    b = key_major(_bias(jnp.asarray(dist, jnp.int32), lut) * LOG2E)
    n_cmp = (t - CMP_LEN) // CMP_STRIDE + 1
    n_sel = -(-t // SEL_BLOCK)
    covt = np.zeros((LANES, n_chunks), np.float32)
    covt[:n_sel, :n_cmp] = _cover(n_cmp, n_sel).T
    keys = np.arange(-TQ, t)
    onehot = (keys[:, None] // SEL_BLOCK == np.arange(LANES)[None, :]) & (keys[:, None] >= 0)
    onehot[:, LANES - 1] = keys < 0
    wkeys = np.arange(-WINDOW, t)
    wflag = np.zeros((WINDOW + t, LANES), np.float32)
    wflag[:, 0] = wkeys < 0
    diag, prev, oldest = near
    none = jnp.zeros_like(prev)
    tabs = dict(sel=jnp.concatenate([none, prev, diag], axis=1), win_a=jnp.concatenate([oldest, none], axis=1)[0],
                win_b=jnp.concatenate([none, prev], axis=1), diag=diag)
    return dict(**tabs, bcmp=b, covt=jnp.asarray(covt, BF16),
                sel_onehot=jnp.asarray(onehot.astype(np.float32), BF16), win_flag=jnp.asarray(wflag, BF16))


def _sample_tables(tbl, t_new, pos0, n_win, n_pad):
    n_chunks = pos0 // CMP_STRIDE
    n_cmp = n_chunks - 1
    n_sel = -(-(pos0 + t_new) // SEL_BLOCK)
    selw = -(-n_sel // LANES) * LANES
    qpos = pos0 + jnp.arange(t_new, dtype=jnp.int32)
    lut = _bias_lut(tbl)

    def rows(dist):
        b = _bias(dist, lut) * LOG2E
        return b.reshape(N_KV, GROUP, t_new, -1).transpose(1, 0, 2, 3).reshape(N_HEADS * t_new, -1)

    blk_end = jnp.arange(n_chunks, dtype=jnp.int32) * CMP_STRIDE + (CMP_LEN - 1)
    new_pos = pos0 + jnp.arange(n_pad, dtype=jnp.int32)
    cov = np.zeros((n_chunks, selw), np.float32)
    cov[:n_cmp, :n_sel] = _cover(n_cmp, n_sel)
    keys = PAGES_PER_STEP * PAGE_SIZE
    expand = (np.arange(LANES)[:, None] == np.arange(keys)[None, :] // SEL_BLOCK).astype(np.float32)
    bnew = rows(qpos[:, None] - new_pos[None, :])
    return dict(
        bcmp=rows(qpos[:, None] - blk_end[None, :]),
        bwc=rows(qpos[:, None] - (pos0 - n_win + jnp.arange(n_win, dtype=jnp.int32))[None, :]),
        bwn=bnew, bseln=bnew,
        bsel=rows(qpos[:, None] - jnp.arange(pos0, dtype=jnp.int32)[None, :]),
        cov=jnp.asarray(cov, BF16), expand=jnp.asarray(expand, BF16))


def _layer_weights(l, norm_mix, w_in, conv_w, q_norm, k_norm, cmp_pe, cmp_w1, cmp_w2, out_norm_conv,
                   out_norm_att, w_out, norm_ffn, w_gate, w_up, w_down, d_conv, d_att, tm_p):
    perm = np.arange(d_att).reshape(N_KV, GROUP, HEAD_DIM).transpose(1, 0, 2).reshape(-1)
    q0 = 3 * d_conv
    w = w_in[l]
    k0 = q0 + d_att
    n_gates = w.shape[1] - (k0 + 6 * D_KV)
    seg = lambda i: w[:, k0 + i * D_KV:k0 + (i + 1) * D_KV]
    gates = jnp.concatenate([w[:, k0 + 6 * D_KV:], jnp.zeros((w.shape[0], GATE_ROWS - n_gates), w.dtype)], axis=1)
    w_tok = jnp.concatenate([w[:, :q0], seg(0), seg(1), seg(2), seg(4)], axis=1).astype(BF16)
    w_chan = jnp.concatenate([w[:, q0:k0][:, perm], seg(3), seg(5), gates], axis=1).T.astype(BF16)
    wo = w_out[l]
    wo = jnp.concatenate([wo[:d_conv], wo[d_conv:][perm]], axis=0).astype(BF16)
    pair = lambda v: jnp.tile(v, 2)[None, :]
    q_gain = q_norm[l] * (SCALE * LOG2E)
    eye2 = jnp.eye(N_KV, dtype=F32)

    def cmp_weights(i, gain):
        w1 = cmp_w1[l, i]
        w2 = cmp_w2[l, i]
        wab = jnp.einsum("hrde,gk->rgdhke", w1.reshape(2, CMP_STRIDE, HEAD_DIM, CMP_HID), eye2)
        return dict(
            wab=wab.reshape(CMP_STRIDE * D_KV, 2 * N_KV * CMP_HID).astype(BF16),
            pe=jnp.broadcast_to(cmp_pe[l, i].reshape(1, -1), (8, CMP_LEN * HEAD_DIM)),
            w1f=w1.reshape(CMP_LEN * HEAD_DIM, CMP_HID),
            w2bd=jnp.einsum("ed,gk->gekd", w2, eye2).reshape(N_KV * CMP_HID, D_KV).astype(BF16),
            gain=gain)

    return dict(
        norm_mix=norm_mix[l][None, :], w_tok=w_tok, w_chan=w_chan,
        q_gain_t=jnp.broadcast_to(q_gain[:, None], (HEAD_DIM, tm_p)),
        k_gain=jnp.stack([jnp.tile(k_norm[l, 1], 2), jnp.tile(k_norm[l, 2], 2)]),
        cmp_k=cmp_weights(0, pair(k_norm[l, 0])), cmp_v=cmp_weights(1, jnp.ones((1, LANES), F32)),
        conv_w=jnp.pad(conv_w[l], ((0, 8 - CONV_WIDTH), (0, 0))),
        norm_conv=out_norm_conv[l][None, :], norm_att=out_norm_att[l][perm][None, :], w_out=wo,
        norm_ffn=norm_ffn[l][None, :],
        w_gate=w_gate[l].astype(BF16), w_up=w_up[l].astype(BF16), w_down=w_down[l].astype(BF16))


def _shifted(u, hist):
    up = jnp.concatenate([hist, u], axis=1)
    t = u.shape[1]
    return up[:, 1:1 + t], up[:, 0:t]


def _chan_major(a):
    a = jnp.moveaxis(a, -3, -1)
    return a.reshape(a.shape[:-3] + (D_KV, a.shape[-1]))


def _token_major(a_t):
    a = a_t.reshape(a_t.shape[:-2] + (N_KV, HEAD_DIM, a_t.shape[-1]))
    return jnp.moveaxis(a, -1, -3)


def kernel(x_prompt, x_sample, state_conv, cache_cmp_k, cache_cmp_v, cache_slc_k, cache_slc_v,
           cache_win_k, cache_win_v, page_table, rel_bias, norm_mix, w_in, conv_w, q_norm, k_norm,
           cmp_pe, cmp_w1, cmp_w2, out_norm_conv, out_norm_att, w_out, norm_ffn, w_gate, w_up, w_down):
    nb, t, dm = x_prompt.shape
    ns, t_new, _ = x_sample.shape
    depth = w_in.shape[0]
    d_conv = conv_w.shape[2]
    d_att = dm - d_conv
    n_pool = cache_cmp_k.shape[1]
    n_pages = page_table.shape[1]
    pos0 = n_pages * PAGE_SIZE
    n_win = cache_win_k.shape[2]
    assert t % TQ == 0 and d_att == N_HEADS * HEAD_DIM
    assert n_pages % PAGES_PER_STEP == 0 and n_pages % CMP_PAGES_PER_STEP == 0
    assert n_win == WINDOW and WINDOW == 4 * TQ and pos0 >= WINDOW and t_new <= LANES and t_new & (t_new - 1) == 0
    assert t // CMP_STRIDE == LANES and SEL_BLOCK == 64 and t >= WINDOW and t // SEL_BLOCK < LANES - 1

    tm_p = 512
    kv = lambda a, b_, t_: a.reshape(b_, t_, N_KV, HEAD_DIM)
    ptab = _prompt_tables(rel_bias, t)
    stab = _sample_tables(rel_bias, t_new, pos0, n_win, LANES)
    chan_pool = lambda c: _chan_major(c).reshape(depth * n_pool, D_KV, PAGE_SIZE)
    cmp_pools = [chan_pool(cache_cmp_k), chan_pool(cache_cmp_v)]
    slc_pools = [chan_pool(cache_slc_k), chan_pool(cache_slc_v)]
    win_t = [_chan_major(cache_win_k), _chan_major(cache_win_v)]
    new_t = lambda a: jnp.pad(a.reshape(ns, t_new, LANES).transpose(0, 2, 1), ((0, 0), (0, 0), (0, LANES - t_new)))
    eye = jnp.eye(N_KV, dtype=BF16)[None, None, :, None, :, None]
    other = (jnp.arange(LANES)[None, :] // HEAD_DIM != jnp.arange(N_KV)[:, None])[None, :, :, None]

    xp = x_prompt.reshape(nb * t, dm)
    xs = x_sample.reshape(ns * t_new, dm)
    st_p, st_s = [], []
    for l in range(depth):
        lw = _layer_weights(l, norm_mix, w_in, conv_w, q_norm, k_norm, cmp_pe, cmp_w1, cmp_w2, out_norm_conv,
                            out_norm_att, w_out, norm_ffn, w_gate, w_up, w_down, d_conv, d_att, tm_p)

        gb, u, kc, vc, ks, kw, q_t, vs_t, vw_t, gates_t = _in_proj_t(
            xp, lw["norm_mix"], lw["w_tok"], lw["w_chan"], lw["q_gain_t"], lw["k_gain"], nb, d_conv, d_att, tm_p)
        seq = lambda a: a.reshape(nb, t, LANES)
        kcmp = _compress_rows(seq(kc), lw["cmp_k"], True, False)
        vcmp_t = _compress_rows(seq(vc), lw["cmp_v"], False, True)
        ks_aug = jnp.concatenate(
            [jnp.pad(seq(ks).astype(BF16), ((0, 0), (TQ, 0), (0, 0))),
             jnp.broadcast_to(ptab["sel_onehot"][None], (nb, TQ + t, LANES))], axis=2)
        kw_aug = jnp.concatenate(
            [jnp.pad(seq(kw).astype(BF16), ((0, 0), (WINDOW, 0), (0, 0))),
             jnp.broadcast_to(ptab["win_flag"][None], (nb, WINDOW + t, LANES))], axis=2)
        ones_rows = lambda v_t, front: jnp.where(
            other, 1.0, jnp.pad(v_t, ((0, 0), (0, 0), (front, 0)))[:, None]).astype(BF16)
        o_t = _pattn(q_t, gates_t, kcmp, vcmp_t, ks_aug, ones_rows(vs_t, TQ), kw_aug, ones_rows(vw_t, WINDOW), ptab)
        o_att = o_t.reshape(nb * t, d_att)
        u3 = u.reshape(nb, t, d_conv)
        xp = _mix_ffn(xp, gb, u, u, u, o_att, lw, tm_p, tiles_per_seq=t // tm_p)
        n_keep = min(WINDOW, t)
        st_p.append((u3[:, -(CONV_WIDTH - 1):], kv(kc, nb, t), kv(vc, nb, t), kv(ks, nb, t), _token_major(vs_t),
                     kv(kw, nb, t)[:, -n_keep:], _token_major(vw_t[:, :, -n_keep:])))

        n_s = ns * t_new
        gb, u, kc, vc, ks, kw, q_t, vs_t, vw_t, gates_t = _in_proj_t(
            xs, lw["norm_mix"], lw["w_tok"], lw["w_chan"], lw["q_gain_t"][:, :n_s], lw["k_gain"], 1, d_conv, d_att, n_s)
        q, vs, vw, gates = q_t[0].T, vs_t[0].T, vw_t[0].T, gates_t[0].T
        base = l * n_pool
        kcmp = _compress_paged(cmp_pools[0], page_table, base, lw["cmp_k"], True)
        vcmp = _compress_paged(cmp_pools[1], page_table, base, lw["cmp_v"], False)
        qr = q.reshape(ns, t_new, GROUP, N_KV, HEAD_DIM).transpose(0, 2, 3, 1, 4)
        qm = (qr[:, :, :, :, None, :] * eye).reshape(ns, N_HEADS * t_new, LANES)
        gr = gates[:, :3 * N_HEADS].reshape(ns, t_new, 3, N_KV, GROUP).transpose(0, 2, 4, 3, 1)
        gr = jnp.broadcast_to(gr.reshape(ns, 3, N_HEADS * t_new, 1), (ns, 3, N_HEADS * t_new, LANES))
        kw_t, vw_t = new_t(kw), new_t(vw)
        o = _sattn(qm, kcmp, vcmp, l, win_t[0], win_t[1], kw_t, vw_t, new_t(ks), new_t(vs),
                   slc_pools[0], slc_pools[1], page_table, base, gr, stab, t_new, pos0)
        o6 = o.reshape(ns, GROUP, N_KV, t_new, N_KV, HEAD_DIM)
        o_att = jnp.stack([o6[:, :, g, :, g] for g in range(N_KV)], axis=2)
        o_att = o_att.transpose(0, 3, 1, 2, 4).reshape(ns * t_new, d_att)
        u3 = u.reshape(ns, t_new, d_conv)
        u1, u2 = _shifted(u3, state_conv[l])
        xs = _mix_ffn(xs, gb, u, u1.reshape(-1, d_conv), u2.reshape(-1, d_conv), o_att, lw, ns * t_new)
        new_conv = jnp.concatenate([state_conv[l], u3], axis=1)[:, -(CONV_WIDTH - 1):]
        n_keep = min(WINDOW, n_win + t_new)
        keep = lambda c_t, n_t: _token_major(jnp.concatenate([c_t, n_t[:, :, :t_new]], axis=2)[:, :, -n_keep:])
        st_s.append((new_conv, kv(kc, ns, t_new), kv(vc, ns, t_new), kv(ks, ns, t_new), kv(vs, ns, t_new),
                     keep(win_t[0][l], kw_t), keep(win_t[1][l], vw_t)))

    outs_p = [jnp.stack(a) for a in zip(*st_p)]
    outs_s = [jnp.stack(a) for a in zip(*st_s)]
    return (xp.reshape(nb, t, dm), xs.reshape(ns, t_new, dm), *outs_p, *outs_s)
```

```python
import functools
import math

import jax
import jax.numpy as jnp
import numpy as np
from jax import lax
from jax.experimental import pallas as pl
from jax.experimental.pallas import tpu as pltpu

F32 = jnp.float32
BF16 = jnp.bfloat16

HEAD_DIM = 64
N_KV = 2
GROUP = 4
N_HEADS = N_KV * GROUP
D_KV = N_KV * HEAD_DIM
CMP_LEN = 32
CMP_STRIDE = 16
CMP_HID = 2 * HEAD_DIM
SEL_BLOCK = 64
SEL_TOPK = 16
WINDOW = 512
NUM_BUCKETS = 32
MAX_DISTANCE = 128
PAGE_SIZE = 128
CONV_WIDTH = 3
EPS = 1e-6
SCALE = HEAD_DIM ** -0.5
LOG2E = math.log2(math.e)
NEG = -1e30

LANES = 128
TQ = 128
GATE_ROWS = 32
FF_CHUNK = 256
CMP_PAGES_PER_STEP = 64
PAGES_PER_STEP = 32
VMEM_LIMIT = 56 * 1024 * 1024

_NT = (((1,), (1,)), ((), ()))


def _dot(a, b):
    return jnp.dot(a, b, preferred_element_type=F32)


def _dot_nt(a, b):
    return lax.dot_general(a, b, _NT, preferred_element_type=F32)


def _params(n_axes):
    return pltpu.CompilerParams(dimension_semantics=("arbitrary",) * n_axes,
                                vmem_limit_bytes=VMEM_LIMIT)


def _resident(shape):
    nd = len(shape)
    return pl.BlockSpec(shape, lambda *_: (0,) * nd, pipeline_mode=pl.Buffered(1))


def _rms(x, gain):
    return x * lax.rsqrt(jnp.mean(x * x, axis=-1, keepdims=True) + EPS) * gain


def _pair_rms(y, gain, lo):
    y2 = y * y
    s_lo = jnp.sum(jnp.where(lo, y2, 0.0), axis=-1, keepdims=True)
    s_hi = jnp.sum(jnp.where(lo, 0.0, y2), axis=-1, keepdims=True)
    inv = jnp.where(lo, lax.rsqrt(s_lo / HEAD_DIM + EPS), lax.rsqrt(s_hi / HEAD_DIM + EPS))
    return y * inv * gain


def _in_proj_t_kernel(x_ref, nm_ref, w_ref, wt_ref, qgt_ref, kg_ref,
                      gb_ref, u_ref, kc_ref, vc_ref, ks_ref, kw_ref, qt_ref, vst_ref, vwt_ref, gtt_ref):
    x = x_ref[...]
    hb = _rms(x, nm_ref[...]).astype(BF16)
    d = gb_ref.shape[1]

    def seg(a, b):
        return _dot(hb, w_ref[:, a:b])

    gb_ref[...] = seg(0, d)
    u_ref[...] = seg(d, 2 * d) * seg(2 * d, 3 * d)
    lo = lax.broadcasted_iota(jnp.int32, (x.shape[0], LANES), 1) < HEAD_DIM
    c = 3 * d
    kc_ref[...] = seg(c, c + LANES)
    vc_ref[...] = seg(c + LANES, c + 2 * LANES)
    ks_ref[...] = _pair_rms(seg(c + 2 * LANES, c + 3 * LANES), kg_ref[0:1, :], lo)
    kw_ref[...] = _pair_rms(seg(c + 3 * LANES, c + 4 * LANES), kg_ref[1:2, :], lo)

    yt = _dot_nt(wt_ref[...], hb)
    d_att = qt_ref.shape[1]
    for h in range(d_att // HEAD_DIM):
        y = yt[h * HEAD_DIM:(h + 1) * HEAD_DIM]
        inv = lax.rsqrt(jnp.mean(y * y, axis=0, keepdims=True) + EPS)
        qt_ref[0, h * HEAD_DIM:(h + 1) * HEAD_DIM, :] = (y * inv * qgt_ref[...]).astype(BF16)
    vst_ref[0] = yt[d_att:d_att + LANES]
    vwt_ref[0] = yt[d_att + LANES:d_att + 2 * LANES]
    gtt_ref[0] = jax.nn.sigmoid(yt[d_att + 2 * LANES:d_att + 2 * LANES + GATE_ROWS])


def _in_proj_t(x, nm, w, wt, qgt, kg, nb, d_conv, d_att, tm):
    n, dm = x.shape
    t = n // nb
    per_b = t // tm
    row = lambda w_: pl.BlockSpec((tm, w_), lambda i: (i, 0))
    col = lambda r: pl.BlockSpec((1, r, tm), lambda i: (i // per_b, 0, i % per_b))
    widths = [d_conv, d_conv] + [LANES] * 4
    rows_t = [d_att, LANES, LANES, GATE_ROWS]
    dt_t = [BF16, F32, F32, F32]
    return pl.pallas_call(
        _in_proj_t_kernel,
        grid=(n // tm,),
        in_specs=[row(dm), _resident(nm.shape), _resident(w.shape), _resident(wt.shape),
                  _resident(qgt.shape), _resident(kg.shape)],
        out_specs=[row(w_) for w_ in widths] + [col(r) for r in rows_t],
        out_shape=[jax.ShapeDtypeStruct((n, w_), F32) for w_ in widths]
        + [jax.ShapeDtypeStruct((nb, r, t), dt) for r, dt in zip(rows_t, dt_t)],
        compiler_params=_params(1),
        name="in_proj_t",
    )(x, nm, w, wt, qgt, kg)


def _cmp_ab_kernel(x_ref, w_ref, a_ref, b_ref):
    n_chunks = x_ref.shape[1] // CMP_STRIDE
    cols = [x_ref[0, pl.ds(l, n_chunks, stride=CMP_STRIDE), :] for l in range(CMP_STRIDE)]
    xr = jnp.concatenate(cols, axis=1).astype(BF16)
    ab = _dot(xr, w_ref[...])
    half = ab.shape[1] // 2
    a_ref[0] = ab[:, :half]
    b_ref[0] = ab[:, half:]


def _cmp_ab_paged_kernel(pt_ref, *refs, n_in):
    x_refs = refs[:n_in]
    perm_ref, w_ref, a_ref, b_ref = refs[n_in:]
    per_page = PAGE_SIZE // CMP_STRIDE
    rows = []
    for xr in x_refs:
        xp = _dot_nt(perm_ref[...], xr[0].astype(BF16))
        rows.append(jnp.concatenate([xp[l * per_page:(l + 1) * per_page] for l in range(CMP_STRIDE)], axis=1))
    xr = jnp.concatenate(rows, axis=0).astype(BF16)
    ab = _dot(xr, w_ref[...])
    half = ab.shape[1] // 2
    a_ref[0] = ab[:, :half]
    b_ref[0] = ab[:, half:]


def _cmp_fin_kernel(a_ref, b_ref, pe_ref, w1_ref, w2_ref, g_ref, o_ref, *, n_valid, normalize, transpose_out):
    a = a_ref[0]
    bm = b_ref[0]
    n_chunks = a.shape[0]
    b_next = pltpu.roll(bm, n_chunks - 1, 0)
    cpe = jnp.dot(pe_ref[...], w1_ref[...], preferred_element_type=F32,
                  precision=lax.Precision.HIGHEST)[0:1]
    x = a + b_next + jnp.concatenate([cpe, cpe], axis=1)
    hid = x * (0.5 * (1.0 + jnp.tanh(math.sqrt(2.0 / math.pi) * (x + 0.044715 * (x * x * x)))))
    out = _dot(hid.astype(BF16), w2_ref[...])
    if normalize:
        lo = lax.broadcasted_iota(jnp.int32, out.shape, 1) < HEAD_DIM
        out = _pair_rms(out, g_ref[...], lo)
    rows = lax.broadcasted_iota(jnp.int32, out.shape, 0)
    out = jnp.where(rows < n_valid, out, 0.0)
    o_ref[0] = out.T if transpose_out else out


def _cmp_fin(a, b, cw, n_valid, normalize, transpose_out=False):
    nb, n_chunks, wid = a.shape
    blk = pl.BlockSpec((1, n_chunks, wid), lambda i: (i, 0, 0))
    oshape = (LANES, n_chunks) if transpose_out else (n_chunks, LANES)
    return pl.pallas_call(
        functools.partial(_cmp_fin_kernel, n_valid=n_valid, normalize=normalize, transpose_out=transpose_out),
        grid=(nb,),
        in_specs=[blk, blk, _resident(cw["pe"].shape), _resident(cw["w1f"].shape),
                  _resident(cw["w2bd"].shape), _resident(cw["gain"].shape)],
        out_specs=pl.BlockSpec((1,) + oshape, lambda i: (i, 0, 0)),
        out_shape=jax.ShapeDtypeStruct((nb,) + oshape, F32),
        compiler_params=_params(1),
        name="cmp_fin",
    )(a, b, cw["pe"], cw["w1f"], cw["w2bd"], cw["gain"])


def _compress_rows(rows, cw, normalize, transpose_out):
    nb, t, _ = rows.shape
    n_chunks = t // CMP_STRIDE
    wid = cw["wab"].shape[1] // 2
    out_blk = pl.BlockSpec((1, n_chunks, wid), lambda i: (i, 0, 0))
    a, b = pl.pallas_call(
        _cmp_ab_kernel,
        grid=(nb,),
        in_specs=[pl.BlockSpec((1, t, LANES), lambda i: (i, 0, 0)), _resident(cw["wab"].shape)],
        out_specs=[out_blk, out_blk],
        out_shape=[jax.ShapeDtypeStruct((nb, n_chunks, wid), F32)] * 2,
        compiler_params=_params(1),
        name="cmp_ab_rows",
    )(rows, cw["wab"])
    return _cmp_fin(a, b, cw, (t - CMP_LEN) // CMP_STRIDE + 1, normalize, transpose_out)


def _page_spec(i, base, pages_per_step):
    return pl.BlockSpec((1, PAGE_SIZE, LANES),
                        lambda b, s, pt: (base + pt[b, s * pages_per_step + i], 0, 0))


def _compress_paged(cache, page_table, base, cw, normalize):
    nb, n_pages = page_table.shape
    pps = CMP_PAGES_PER_STEP
    per_page = PAGE_SIZE // CMP_STRIDE
    ch = pps * per_page
    wid = cw["wab"].shape[1] // 2
    n_chunks = n_pages * per_page
    out_blk = pl.BlockSpec((1, ch, wid), lambda b, s, pt: (b, s, 0))
    r = np.arange(PAGE_SIZE)
    perm = jnp.asarray(r[None, :] == (r % per_page)[:, None] * CMP_STRIDE + (r // per_page)[:, None], BF16)
    cst = lambda a: pl.BlockSpec(a.shape, lambda b, s, pt: (0, 0), pipeline_mode=pl.Buffered(1))
    a, b = pl.pallas_call(
        functools.partial(_cmp_ab_paged_kernel, n_in=pps),
        grid_spec=pltpu.PrefetchScalarGridSpec(
            num_scalar_prefetch=1,
            grid=(nb, n_pages // pps),
            in_specs=[_page_spec(i, base, pps) for i in range(pps)] + [cst(perm), cst(cw["wab"])],
            out_specs=[out_blk, out_blk]),
        out_shape=[jax.ShapeDtypeStruct((nb, n_chunks, wid), F32)] * 2,
        compiler_params=_params(2),
        name="cmp_ab_paged",
    )(page_table, *([cache] * pps), perm, cw["wab"])
    return _cmp_fin(a, b, cw, n_chunks - 1, normalize)


def _split_bf16(x):
    hi = x.astype(BF16)
    return hi, (x - hi.astype(F32)).astype(BF16)


def _online_update(state, s, mask, v_t):
    m, l, acc = state
    s = jnp.where(mask, s, NEG)
    m_new = jnp.maximum(m, jnp.max(s, axis=-1, keepdims=True))
    alpha = jnp.exp2(m - m_new)
    e = jnp.where(mask, jnp.exp2(s - m_new), 0.0)
    l = alpha * l + jnp.sum(e, axis=-1, keepdims=True)
    acc = alpha * acc + _dot_nt(e.astype(BF16), v_t)
    return m_new, l, acc


def _finish(state):
    _, l, acc = state
    return acc / jnp.where(l > 0, l, 1.0)


def _reset_t(m_ref, acc_ref):
    m_ref[...] = jnp.full(m_ref.shape, NEG, F32)
    acc_ref[...] = jnp.zeros(acc_ref.shape, F32)


def _update_t(m_ref, acc_ref, s_t, v_t):
    m = m_ref[...]
    m_new = jnp.maximum(m, jnp.max(s_t, axis=0, keepdims=True))
    alpha = jnp.exp2(m - m_new)
    e = jnp.exp2(s_t - m_new).astype(BF16)
    acc_ref[...] = alpha * acc_ref[...] + _dot(v_t, e)
    m_ref[...] = m_new


def _pattn_kernel(qt_ref, gt_ref, kcmp_ref, vcmpt_ref, ksa_ref, vst_ref, kwa_ref, vwt_ref,
                  bcmp_ref, stab_ref, wa_ref, wb_ref, diag_ref, covt_ref, o_ref, m_ref, acc_ref, mw_ref, accw_ref, qa_ref, out_ref, sn_ref, *, n_sel, topk):
    qi = pl.program_id(1)
    q0 = qi * TQ
    n_slabs = qt_ref.shape[1] // LANES
    wide = n_slabs * TQ
    sub = lax.broadcasted_iota(jnp.int32, (LANES, TQ), 0)
    lane = lax.broadcasted_iota(jnp.int32, (LANES, TQ), 1)
    sub_w = lax.broadcasted_iota(jnp.int32, (LANES, wide), 0)
    tpos_w = q0 + (lax.broadcasted_iota(jnp.int32, (LANES, wide), 1) & (TQ - 1))
    kcmp = kcmp_ref[0].astype(BF16)
    vcmp_t = vcmpt_ref[0].astype(BF16)
    covt = covt_ref[...]
    gt = gt_ref[0]
    zero_rows = jnp.zeros((LANES - n_sel, TQ), F32)
    def gate(br, g):
        return jnp.concatenate([gt[br * N_HEADS + g * GROUP + j:br * N_HEADS + g * GROUP + j + 1, :]
                                for j in range(n_slabs)], axis=1)

    def keys(k_ref, v_ref, g, kt, n):
        start = pl.multiple_of(kt * TQ, TQ)
        return k_ref[0, pl.ds(start, n * TQ), :], v_ref[0, g, :, pl.ds(start, n * TQ)]

    def add_branch(br, acc_ref_):
        for g in range(N_KV):
            acc = acc_ref_[g]
            den = acc[HEAD_DIM:HEAD_DIM + 1] if g == 0 else acc[0:1]
            out_ref[g] += (gate(br, g) * (1.0 / den)) * acc

    q_ts = []
    for g in range(N_KV):
        mine = (sub < HEAD_DIM) if g == 0 else (sub >= HEAD_DIM)
        q_ts.append(jnp.concatenate(
            [jnp.where(mine, qt_ref[0, j * LANES:(j + 1) * LANES, :], jnp.zeros((), BF16))
             for j in range(n_slabs)], axis=1))
        qa_ref[g, 0:LANES, :] = q_ts[g]

    pad_pen = jnp.where(sub_w == 0, NEG, 0.0).astype(BF16)
    win = []
    for g in range(N_KV):
        q_win = jnp.concatenate([q_ts[g], pad_pen], axis=0)
        jobs = []
        for kt, n, tab in ((qi, 2, wa_ref[...]), (qi + 2, 2, wb_ref[g]), (qi + 4, 1, diag_ref[g])):
            k_, v_ = keys(kwa_ref, vwt_ref, g, kt, n)
            jobs.append((_dot(k_, q_win) + tab, v_))
        win.append(jobs)
    n_blk = bcmp_ref.shape[1] // 2
    row0 = pl.multiple_of(n_blk - qi * (TQ // CMP_STRIDE), TQ // CMP_STRIDE)
    s_cmp = [_dot(kcmp, q_ts[g]) + bcmp_ref[g, pl.ds(row0, n_blk), :] for g in range(N_KV)]

    for g in range(N_KV):
        mask = sub_w * CMP_STRIDE + (CMP_LEN - 1) <= tpos_w
        s = jnp.where(mask, s_cmp[g], NEG)
        e = jnp.where(mask, jnp.exp2(s - jnp.max(s, axis=0, keepdims=True)), 0.0)
        l = jnp.sum(e, axis=0, keepdims=True)
        p = e * (1.0 / jnp.where(l > 0, l, 1.0))
        out_ref[g] = gate(0, g) * _dot(vcmp_t, p.astype(BF16))
        p_sum = p[:, 0:TQ]
        for j in range(1, n_slabs):
            p_sum = p_sum + p[:, j * TQ:(j + 1) * TQ]

        hi, lo_ = _split_bf16(p_sum)
        sc = _dot(covt, hi) + _dot(covt, lo_)
        tq_pos = q0 + lane
        cur = tq_pos >> 6
        forced = (sub == 0) | (sub == cur) | (sub == cur - 1)
        sc = jnp.where(forced, 1e6, jnp.where(sub * SEL_BLOCK > tq_pos, -1e6, sc))
        sc = sc[0:n_sel]
        blk = sub[0:n_sel]
        rank = jnp.zeros((n_sel, TQ), F32)
        for i in range(n_sel):
            si = sc[i:i + 1, :]
            beats = (si > sc) | ((si == sc) & (blk > i))
            rank = rank + jnp.where(beats, 1.0, 0.0)
        pen = jnp.concatenate([jnp.where(rank < topk, 0.0, NEG), zero_rows], axis=0)
        pen = jnp.where(sub == LANES - 1, NEG, pen).astype(BF16)
        qa_ref[g, LANES:2 * LANES, :] = jnp.concatenate([pen] * n_slabs, axis=1)

    _reset_t(mw_ref, accw_ref)
    for job in range(len(win[0])):
        for g in range(N_KV):
            _update_t(mw_ref.at[g], accw_ref.at[g], *win[g][job])

    _reset_t(m_ref, acc_ref)
    n_far = jnp.maximum(qi - 1, 0)

    n_pairs = n_far // 2
    for g in range(N_KV):
        sn_ref[g] = _dot(keys(ksa_ref, vst_ref, g, 1, 2)[0], qa_ref[g])

    @pl.loop(0, n_pairs)
    def _(i):
        nxt = 1 + 2 * jnp.minimum(i + 1, n_pairs - 1)
        s_next = [_dot(keys(ksa_ref, vst_ref, g, nxt, 2)[0], qa_ref[g]) for g in range(N_KV)]
        for g in range(N_KV):
            _update_t(m_ref.at[g], acc_ref.at[g], sn_ref[g], keys(ksa_ref, vst_ref, g, 1 + 2 * i, 2)[1])
        for g in range(N_KV):
            sn_ref[g] = s_next[g]

    odd = jnp.where(n_far % 2 == 1, n_far, 0)
    last = []
    for g in range(N_KV):
        k0, v0 = keys(ksa_ref, vst_ref, g, odd, 1)
        k1, v1 = keys(ksa_ref, vst_ref, g, qi, 2)
        last.append((_dot(jnp.concatenate([k0, k1], axis=0), qa_ref[g]) + stab_ref[g],
                     jnp.concatenate([v0, v1], axis=1)))
    for g in range(N_KV):
        _update_t(m_ref.at[g], acc_ref.at[g], *last[g])
    add_branch(1, acc_ref)
    add_branch(2, accw_ref)

    out = jnp.where(sub_w < HEAD_DIM, out_ref[0], out_ref[1])
    for j in range(n_slabs):
        o_ref[0, :, j * LANES:(j + 1) * LANES] = out[:, j * TQ:(j + 1) * TQ].T


def _pattn(q_t, gates_t, kcmp, vcmp_t, ks_aug, vs_t, kw_aug, vw_t, tabs):
    nb, d_att, t = q_t.shape
    n_q = t // TQ
    n_sel = -(-t // SEL_BLOCK)
    n_slabs = d_att // LANES
    whole = lambda a: pl.BlockSpec((1,) + a.shape[1:], lambda b, i: (b,) + (0,) * (a.ndim - 1))
    return pl.pallas_call(
        functools.partial(_pattn_kernel, n_sel=n_sel, topk=min(SEL_TOPK, n_sel)),
        grid=(nb, n_q),
        in_specs=[pl.BlockSpec((1, d_att, TQ), lambda b, i: (b, 0, i)),
                  pl.BlockSpec((1, GATE_ROWS, TQ), lambda b, i: (b, 0, i)),
                  whole(kcmp), whole(vcmp_t), whole(ks_aug), whole(vs_t), whole(kw_aug), whole(vw_t),
                  _resident(tabs["bcmp"].shape),
                  _resident(tabs["sel"].shape), _resident(tabs["win_a"].shape), _resident(tabs["win_b"].shape),
                  _resident(tabs["diag"].shape), _resident(tabs["covt"].shape)],
        out_specs=pl.BlockSpec((1, TQ, d_att), lambda b, i: (b, i, 0)),
        out_shape=jax.ShapeDtypeStruct((nb, t, d_att), F32),
        scratch_shapes=[pltpu.VMEM((N_KV, 1, n_slabs * TQ), F32), pltpu.VMEM((N_KV, LANES, n_slabs * TQ), F32)] * 2
        + [pltpu.VMEM((N_KV, 2 * LANES, n_slabs * TQ), BF16), pltpu.VMEM((N_KV, LANES, n_slabs * TQ), F32),
           pltpu.VMEM((N_KV, 2 * TQ, n_slabs * TQ), F32)],
        compiler_params=_params(2),
        name="pattn",
    )(q_t, gates_t, kcmp, vcmp_t, ks_aug, vs_t, kw_aug, vw_t, tabs["bcmp"], tabs["sel"], tabs["win_a"],
      tabs["win_b"], tabs["diag"], tabs["covt"])


def _sattn1_kernel(qm_ref, kcmp_ref, vcmp_ref, kwc_ref, vwc_ref, kwn_ref, vwn_ref,
                   bcmp_ref, bwc_ref, bwn_ref, cov_ref, ocmp_ref, owin_ref, sel_ref,
                   *, n_cmp, n_sel, topk, t_new, pos0, win_start):
    qm = qm_ref[0]
    n_rows = qm.shape[0]
    n_gt = N_KV * t_new
    s = _dot_nt(qm, kcmp_ref[0].astype(BF16)) + bcmp_ref[...]
    ncol = lax.broadcasted_iota(jnp.int32, s.shape, 1)
    trow = pos0 + (lax.broadcasted_iota(jnp.int32, s.shape, 0) & (t_new - 1))
    mask = (ncol < n_cmp) & (ncol * CMP_STRIDE + (CMP_LEN - 1) <= trow)
    s = jnp.where(mask, s, NEG)
    e = jnp.where(mask, jnp.exp2(s - jnp.max(s, axis=-1, keepdims=True)), 0.0)
    l = jnp.sum(e, axis=-1, keepdims=True)
    p = e / jnp.where(l > 0, l, 1.0)
    ocmp_ref[0] = _dot(p.astype(BF16), vcmp_ref[0].astype(BF16))
    p_sum = p[0:n_gt]
    for j in range(1, n_rows // n_gt):
        p_sum = p_sum + p[j * n_gt:(j + 1) * n_gt]
    hi, lo = _split_bf16(p_sum)
    sc = _dot(hi, cov_ref[...]) + _dot(lo, cov_ref[...])
    blk = lax.broadcasted_iota(jnp.int32, sc.shape, 1)
    qpos = pos0 + (lax.broadcasted_iota(jnp.int32, sc.shape, 0) & (t_new - 1))
    cur = qpos >> 6
    forced = (blk == 0) | (blk == cur) | (blk == cur - 1)
    sc = jnp.where(forced, 1e6, jnp.where(blk * SEL_BLOCK > qpos, -1e6, sc))
    sc = jnp.where(blk < n_sel, sc, -3e38)
    rank = jnp.zeros(sc.shape, F32)
    for i in range(n_sel):
        si = sc[:, i:i + 1]
        beats = (si > sc) | ((si == sc) & (blk > i))
        rank = rank + jnp.where(beats, 1.0, 0.0)
    sel_ref[0] = jnp.where((rank < topk) & (blk < n_sel), 1.0, 0.0)

    init = (jnp.full((n_rows, 1), NEG, F32), jnp.zeros((n_rows, 1), F32), jnp.zeros((n_rows, LANES), F32))
    s = _dot(qm, kwc_ref[0, 0].astype(BF16)) + bwc_ref[...]
    trow = pos0 + (lax.broadcasted_iota(jnp.int32, s.shape, 0) & (t_new - 1))
    dist = trow - (win_start + lax.broadcasted_iota(jnp.int32, s.shape, 1))
    st = _online_update(init, s, (dist >= 0) & (dist < WINDOW), vwc_ref[0, 0].astype(BF16))
    s = _dot(qm, kwn_ref[0].astype(BF16)) + bwn_ref[...]
    trow = pos0 + (lax.broadcasted_iota(jnp.int32, s.shape, 0) & (t_new - 1))
    col = lax.broadcasted_iota(jnp.int32, s.shape, 1)
    dist = trow - (pos0 + col)
    st = _online_update(st, s, (dist >= 0) & (dist < WINDOW) & (col < t_new), vwn_ref[0].astype(BF16))
    owin_ref[0] = _finish(st)


def _sattn2_kernel(pt_ref, *refs, pps, t_new):
    k_refs = refs[:pps]
    v_refs = refs[pps:2 * pps]
    (qm_ref, selc_ref, seln_ref, ksn_ref, vsn_ref, bsel_ref, bseln_ref, exp_ref,
     ocmp_ref, owin_ref, gate_ref, o_ref, m_sc, l_sc, acc_sc) = refs[2 * pps:]
    step = pl.program_id(1)
    qm = qm_ref[0]
    n_rows = qm.shape[0]
    reps = n_rows // selc_ref.shape[2]

    @pl.when(step == 0)
    def _():
        s = _dot(qm, ksn_ref[0].astype(BF16)) + bseln_ref[...]
        trow = lax.broadcasted_iota(jnp.int32, s.shape, 0) & (t_new - 1)
        col = lax.broadcasted_iota(jnp.int32, s.shape, 1)
        hit = jnp.concatenate([seln_ref[0]] * reps, axis=0) > 0.5
        init = (jnp.full((n_rows, 1), NEG, F32), jnp.zeros((n_rows, 1), F32),
                jnp.zeros((n_rows, LANES), F32))
        m, l, acc = _online_update(init, s, hit & (col <= trow) & (col < t_new), vsn_ref[0].astype(BF16))
        m_sc[...] = m
        l_sc[...] = l
        acc_sc[...] = acc

    k_t = jnp.concatenate([r[0] for r in k_refs], axis=1).astype(BF16)
    v_t = jnp.concatenate([r[0] for r in v_refs], axis=1).astype(BF16)
    s = _dot(qm, k_t) + bsel_ref[...]
    hit = _dot(selc_ref[0, 0].astype(BF16), exp_ref[...])
    mask = jnp.concatenate([hit] * reps, axis=0) > 0.5
    m, l, acc = _online_update((m_sc[...], l_sc[...], acc_sc[...]), s, mask, v_t)
    m_sc[...] = m
    l_sc[...] = l
    acc_sc[...] = acc

    @pl.when(step == pl.num_programs(1) - 1)
    def _():
        o_sel = _finish((m, l, acc))
        o_ref[0] = gate_ref[0, 0] * ocmp_ref[0] + gate_ref[0, 1] * o_sel + gate_ref[0, 2] * owin_ref[0]


def _sattn(qm, kcmp, vcmp, layer, kwc_t, vwc_t, kwn_t, vwn_t, ksn_t, vsn_t, cache_k, cache_v, page_table, base,
           gates, tabs, t_new, pos0):
    nb, n_rows, _ = qm.shape
    n_pages = page_table.shape[1]
    n_chunks = kcmp.shape[1]
    n_gt = N_KV * t_new
    n_sel = -(-(pos0 + t_new) // SEL_BLOCK)
    selw = tabs["cov"].shape[1]
    per_b = lambda shape: pl.BlockSpec((1,) + shape, lambda b: (b,) + (0,) * len(shape))
    win_c = pl.BlockSpec((1, 1) + kwc_t.shape[2:], lambda b: (layer, b, 0, 0))
    ocmp, owin, sel = pl.pallas_call(
        functools.partial(_sattn1_kernel, n_cmp=n_chunks - 1, n_sel=n_sel, topk=min(SEL_TOPK, n_sel),
                          t_new=t_new, pos0=pos0, win_start=pos0 - kwc_t.shape[3]),
        grid=(nb,),
        in_specs=[per_b((n_rows, LANES)), per_b((n_chunks, LANES)), per_b((n_chunks, LANES)),
                  win_c, win_c, per_b(kwn_t.shape[1:]), per_b(vwn_t.shape[1:]),
                  _resident(tabs["bcmp"].shape), _resident(tabs["bwc"].shape), _resident(tabs["bwn"].shape),
                  _resident(tabs["cov"].shape)],
        out_specs=[per_b((n_rows, LANES)), per_b((n_rows, LANES)), per_b((n_gt, selw))],
        out_shape=[jax.ShapeDtypeStruct((nb, n_rows, LANES), F32), jax.ShapeDtypeStruct((nb, n_rows, LANES), F32),
                   jax.ShapeDtypeStruct((nb, n_gt, selw), F32)],
        compiler_params=_params(1),
        name="sattn1",
    )(qm, kcmp, vcmp, kwc_t, vwc_t, kwn_t, vwn_t, tabs["bcmp"], tabs["bwc"], tabs["bwn"], tabs["cov"])

    pps = PAGES_PER_STEP
    n_steps = n_pages // pps
    bps = pps * PAGE_SIZE // SEL_BLOCK
    n_cblk = n_pages * PAGE_SIZE // SEL_BLOCK
    selc = sel[:, :, :n_cblk].reshape(nb, n_gt, n_steps, bps).transpose(0, 2, 1, 3)
    selc = jnp.pad(selc, ((0, 0), (0, 0), (0, 0), (0, LANES - bps)))
    seln = jnp.broadcast_to(sel[:, :, n_cblk:n_cblk + 1], (nb, n_gt, LANES))

    keys = pps * PAGE_SIZE
    cst = lambda shape: pl.BlockSpec(shape, lambda b, s, pt: (0,) * len(shape), pipeline_mode=pl.Buffered(1))
    pb = lambda shape: pl.BlockSpec((1,) + shape, lambda b, s, pt: (b,) + (0,) * len(shape))
    return pl.pallas_call(
        functools.partial(_sattn2_kernel, pps=pps, t_new=t_new),
        grid_spec=pltpu.PrefetchScalarGridSpec(
            num_scalar_prefetch=1,
            grid=(nb, n_steps),
            in_specs=[_page_spec(i, base, pps) for i in range(pps)] * 2
            + [pb((n_rows, LANES)),
               pl.BlockSpec((1, 1, n_gt, LANES), lambda b, s, pt: (b, s, 0, 0)),
               pb((n_gt, LANES)), pb(ksn_t.shape[1:]), pb(vsn_t.shape[1:]),
               pl.BlockSpec((n_rows, keys), lambda b, s, pt: (0, s)),
               cst(tabs["bseln"].shape), cst(tabs["expand"].shape),
               pb((n_rows, LANES)), pb((n_rows, LANES)), pb((3, n_rows, LANES))],
            out_specs=pb((n_rows, LANES)),
            scratch_shapes=[pltpu.VMEM((n_rows, 1), F32), pltpu.VMEM((n_rows, 1), F32),
                            pltpu.VMEM((n_rows, LANES), F32)]),
        out_shape=jax.ShapeDtypeStruct((nb, n_rows, LANES), F32),
        compiler_params=_params(2),
        name="sattn2",
    )(page_table, *([cache_k] * pps), *([cache_v] * pps), qm, selc, seln, ksn_t, vsn_t,
      tabs["bsel"], tabs["bseln"], tabs["expand"], ocmp, owin, gates)


def _mix_ffn_kernel(x_ref, gb_ref, u_ref, u1_ref, u2_ref, oa_ref, cw_ref, nc_ref, na_ref, wo_ref,
                    nf_ref, wg_ref, wu_ref, wd_ref, y_ref, acc_ref, *, tiles_per_seq):
    d = gb_ref.shape[1]
    u = u_ref[...]
    if tiles_per_seq is None:
        u1, u2 = u1_ref[...], u2_ref[...]
    else:
        prev = u1_ref[...] * jnp.where(pl.program_id(0) % tiles_per_seq == 0, 0.0, 1.0)
        row = lax.broadcasted_iota(jnp.int32, u.shape, 0)
        u1 = jnp.where(row == 0, prev[7:8], pltpu.roll(u, 1, 0))
        u2 = jnp.where(row == 0, prev[6:7], jnp.where(row == 1, prev[7:8], pltpu.roll(u, 2, 0)))
    conv = u2 * cw_ref[0:1, :] + u1 * cw_ref[1:2, :] + u * cw_ref[2:3, :]
    yc = _rms(gb_ref[...] * conv, nc_ref[...]).astype(BF16)
    oa = _rms(oa_ref[...], na_ref[...]).astype(BF16)
    x1 = x_ref[...] + (_dot(yc, wo_ref[0:d, :]) + _dot(oa, wo_ref[d:, :]))
    h2 = _rms(x1, nf_ref[...]).astype(BF16)
    acc_ref[...] = jnp.zeros_like(acc_ref)
    for c in range(0, wg_ref.shape[1], FF_CHUNK):
        gate = _dot(h2, wg_ref[:, c:c + FF_CHUNK])
        act = (gate * jax.nn.sigmoid(gate)) * _dot(h2, wu_ref[:, c:c + FF_CHUNK])
        acc_ref[...] += _dot(act.astype(BF16), wd_ref[c:c + FF_CHUNK, :])
    y_ref[...] = x1 + acc_ref[...]


def _mix_ffn(x, gb, u, u1, u2, oa, lw, tm, tiles_per_seq=None):
    n, dm = x.shape
    d = gb.shape[1]
    row = lambda w_: pl.BlockSpec((tm, w_), lambda i: (i, 0))
    hist = row(d) if tiles_per_seq is None else pl.BlockSpec((8, d), lambda i: (jnp.maximum(i * (tm // 8) - 1, 0), 0))
    consts = [lw["conv_w"], lw["norm_conv"], lw["norm_att"], lw["w_out"], lw["norm_ffn"],
              lw["w_gate"], lw["w_up"], lw["w_down"]]
    return pl.pallas_call(
        functools.partial(_mix_ffn_kernel, tiles_per_seq=tiles_per_seq),
        grid=(n // tm,),
        in_specs=[row(dm), row(d), row(d), hist, hist, row(d)] + [_resident(c.shape) for c in consts],
        out_specs=row(dm),
        out_shape=jax.ShapeDtypeStruct((n, dm), F32),
        scratch_shapes=[pltpu.VMEM((tm, dm), F32)],
        compiler_params=_params(1),
        name="mix_ffn",
    )(x, gb, u, u1, u2, oa, *consts)


def _t5_bucket(dist):
    n = jnp.maximum(dist, 0)
    max_exact = NUM_BUCKETS // 2
    nf = jnp.maximum(n, 1).astype(F32)
    large = max_exact + (jnp.log(nf / max_exact) / math.log(MAX_DISTANCE / max_exact)
                         * (NUM_BUCKETS - max_exact)).astype(jnp.int32)
    return jnp.where(n < max_exact, n, jnp.minimum(large, NUM_BUCKETS - 1))


def _bias_lut(tbl):
    return tbl[_t5_bucket(jnp.arange(MAX_DISTANCE + 1, dtype=jnp.int32))]


def _bias(dist, lut):
    idx = jnp.clip(dist, 0, MAX_DISTANCE)
    onehot = (idx[..., None] == jnp.arange(MAX_DISTANCE + 1, dtype=jnp.int32)).astype(F32)
    return jnp.einsum("...k,kh->h...", onehot, lut, precision=lax.Precision.HIGHEST)


def _cover(n_cmp, n_sel):
    c0 = np.arange(n_cmp)[:, None] * CMP_STRIDE
    s0 = np.arange(n_sel)[None, :] * SEL_BLOCK
    return np.clip(np.minimum(c0 + CMP_LEN, s0 + SEL_BLOCK) - np.maximum(c0, s0), 0, None) / CMP_STRIDE


def _prompt_tables(tbl, t):
    n_q = t // TQ
    n_chunks = t // CMP_STRIDE
    ar = jnp.arange(TQ, dtype=jnp.int32)
    lut = _bias_lut(tbl)
    far = lut[MAX_DISTANCE]
    assert MAX_DISTANCE <= TQ

    def key_major(b):
        k, q = b.shape[1:]
        return b.reshape(N_KV, GROUP, k, q).transpose(0, 2, 1, 3).reshape(N_KV, k, GROUP * q)

    near = []
    for c in range(2):
        dist = c * TQ + ar[None, :] - ar[:, None]
        b = (_bias(dist, lut) - far[:, None, None]) * LOG2E
        near.append(key_major(jnp.where(dist[None] >= 0, b, NEG)))
    oldest = jnp.where(ar[:, None] > ar[None, :], 0.0, NEG)
    near.append(key_major(jnp.broadcast_to(oldest[None], (N_HEADS, TQ, TQ))))
    rel = np.arange(n_chunks)
    rel = np.where(rel < n_chunks // 2, rel, rel - n_chunks)
    dist = np.arange(TQ)[None, :] - (rel[:, None] * CMP_STRIDE + (CMP_LEN - 1))
    dist = np.where(rel[:, None] >= TQ // CMP_STRIDE, MAX_DISTANCE, dist)
    assert (TQ // CMP_STRIDE) * (n_q - 1) < n_chunks and (n_chunks // 2) * CMP_STRIDE >= MAX_DISTANCE + CMP_LEN
    b = key_major(_bias(jnp.asarray(dist, jnp.int32), lut) * LOG2E)
    b = jnp.concatenate([b, b], axis=1)
    n_cmp = (t - CMP_LEN) // CMP_STRIDE + 1
    n_sel = -(-t // SEL_BLOCK)
    covt = np.zeros((LANES, n_chunks), np.float32)
    covt[:n_sel, :n_cmp] = _cover(n_cmp, n_sel).T
    keys = np.arange(-TQ, t)
    onehot = (keys[:, None] // SEL_BLOCK == np.arange(LANES)[None, :]) & (keys[:, None] >= 0)
    onehot[:, LANES - 1] = keys < 0
    wkeys = np.arange(-WINDOW, t)
    wflag = np.zeros((WINDOW + t, LANES), np.float32)
    wflag[:, 0] = wkeys < 0
    diag, prev, oldest = near
    none = jnp.zeros_like(prev)
    tabs = dict(sel=jnp.concatenate([none, prev, diag], axis=1), win_a=jnp.concatenate([oldest, none], axis=1)[0],
                win_b=jnp.concatenate([none, prev], axis=1), diag=diag)
    return dict(**tabs, bcmp=b, covt=jnp.asarray(covt, BF16),
                sel_onehot=jnp.asarray(onehot.astype(np.float32), BF16), win_flag=jnp.asarray(wflag, BF16))


def _sample_tables(tbl, t_new, pos0, n_win, n_pad):
    n_chunks = pos0 // CMP_STRIDE
    n_cmp = n_chunks - 1
    n_sel = -(-(pos0 + t_new) // SEL_BLOCK)
    selw = -(-n_sel // LANES) * LANES
    qpos = pos0 + jnp.arange(t_new, dtype=jnp.int32)
    lut = _bias_lut(tbl)

    def rows(dist):
        b = _bias(dist, lut) * LOG2E
        return b.reshape(N_KV, GROUP, t_new, -1).transpose(1, 0, 2, 3).reshape(N_HEADS * t_new, -1)

    blk_end = jnp.arange(n_chunks, dtype=jnp.int32) * CMP_STRIDE + (CMP_LEN - 1)
    new_pos = pos0 + jnp.arange(n_pad, dtype=jnp.int32)
    cov = np.zeros((n_chunks, selw), np.float32)
    cov[:n_cmp, :n_sel] = _cover(n_cmp, n_sel)
    keys = PAGES_PER_STEP * PAGE_SIZE
    expand = (np.arange(LANES)[:, None] == np.arange(keys)[None, :] // SEL_BLOCK).astype(np.float32)
    bnew = rows(qpos[:, None] - new_pos[None, :])
    return dict(
        bcmp=rows(qpos[:, None] - blk_end[None, :]),
        bwc=rows(qpos[:, None] - (pos0 - n_win + jnp.arange(n_win, dtype=jnp.int32))[None, :]),
        bwn=bnew, bseln=bnew,
        bsel=rows(qpos[:, None] - jnp.arange(pos0, dtype=jnp.int32)[None, :]),
        cov=jnp.asarray(cov, BF16), expand=jnp.asarray(expand, BF16))


def _layer_weights(l, norm_mix, w_in, conv_w, q_norm, k_norm, cmp_pe, cmp_w1, cmp_w2, out_norm_conv,
                   out_norm_att, w_out, norm_ffn, w_gate, w_up, w_down, d_conv, d_att, tm_p):
    perm = np.arange(d_att).reshape(N_KV, GROUP, HEAD_DIM).transpose(1, 0, 2).reshape(-1)
    q0 = 3 * d_conv
    w = w_in[l]
    k0 = q0 + d_att
    n_gates = w.shape[1] - (k0 + 6 * D_KV)
    seg = lambda i: w[:, k0 + i * D_KV:k0 + (i + 1) * D_KV]
    gates = jnp.concatenate([w[:, k0 + 6 * D_KV:], jnp.zeros((w.shape[0], GATE_ROWS - n_gates), w.dtype)], axis=1)
    w_tok = jnp.concatenate([w[:, :q0], seg(0), seg(1), seg(2), seg(4)], axis=1).astype(BF16)
    w_chan = jnp.concatenate([w[:, q0:k0][:, perm], seg(3), seg(5), gates], axis=1).T.astype(BF16)
    wo = w_out[l]
    wo = jnp.concatenate([wo[:d_conv], wo[d_conv:][perm]], axis=0).astype(BF16)
    pair = lambda v: jnp.tile(v, 2)[None, :]
    q_gain = q_norm[l] * (SCALE * LOG2E)
    eye2 = jnp.eye(N_KV, dtype=F32)

    def cmp_weights(i, gain):
        w1 = cmp_w1[l, i]
        w2 = cmp_w2[l, i]
        wab = jnp.einsum("hrde,gk->rgdhke", w1.reshape(2, CMP_STRIDE, HEAD_DIM, CMP_HID), eye2)
        return dict(
            wab=wab.reshape(CMP_STRIDE * D_KV, 2 * N_KV * CMP_HID).astype(BF16),
            pe=jnp.broadcast_to(cmp_pe[l, i].reshape(1, -1), (8, CMP_LEN * HEAD_DIM)),
            w1f=w1.reshape(CMP_LEN * HEAD_DIM, CMP_HID),
            w2bd=jnp.einsum("ed,gk->gekd", w2, eye2).reshape(N_KV * CMP_HID, D_KV).astype(BF16),
            gain=gain)

    return dict(
        norm_mix=norm_mix[l][None, :], w_tok=w_tok, w_chan=w_chan,
        q_gain_t=jnp.broadcast_to(q_gain[:, None], (HEAD_DIM, tm_p)),
        k_gain=jnp.stack([jnp.tile(k_norm[l, 1], 2), jnp.tile(k_norm[l, 2], 2)]),
        cmp_k=cmp_weights(0, pair(k_norm[l, 0])), cmp_v=cmp_weights(1, jnp.ones((1, LANES), F32)),
        conv_w=jnp.pad(conv_w[l], ((0, 8 - CONV_WIDTH), (0, 0))),
        norm_conv=out_norm_conv[l][None, :], norm_att=out_norm_att[l][perm][None, :], w_out=wo,
        norm_ffn=norm_ffn[l][None, :],
        w_gate=w_gate[l].astype(BF16), w_up=w_up[l].astype(BF16), w_down=w_down[l].astype(BF16))


def _shifted(u, hist):
    up = jnp.concatenate([hist, u], axis=1)
    t = u.shape[1]
    return up[:, 1:1 + t], up[:, 0:t]


def _chan_major(a):
    a = jnp.moveaxis(a, -3, -1)
    return a.reshape(a.shape[:-3] + (D_KV, a.shape[-1]))


def _token_major(a_t):
    a = a_t.reshape(a_t.shape[:-2] + (N_KV, HEAD_DIM, a_t.shape[-1]))
    return jnp.moveaxis(a, -1, -3)


def kernel(x_prompt, x_sample, state_conv, cache_cmp_k, cache_cmp_v, cache_slc_k, cache_slc_v,
           cache_win_k, cache_win_v, page_table, rel_bias, norm_mix, w_in, conv_w, q_norm, k_norm,
           cmp_pe, cmp_w1, cmp_w2, out_norm_conv, out_norm_att, w_out, norm_ffn, w_gate, w_up, w_down):
    nb, t, dm = x_prompt.shape
    ns, t_new, _ = x_sample.shape
    depth = w_in.shape[0]
    d_conv = conv_w.shape[2]
    d_att = dm - d_conv
    n_pool = cache_cmp_k.shape[1]
    n_pages = page_table.shape[1]
    pos0 = n_pages * PAGE_SIZE
    n_win = cache_win_k.shape[2]
    assert t % TQ == 0 and d_att == N_HEADS * HEAD_DIM
    assert n_pages % PAGES_PER_STEP == 0 and n_pages % CMP_PAGES_PER_STEP == 0
    assert n_win == WINDOW and WINDOW == 4 * TQ and pos0 >= WINDOW and t_new <= LANES and t_new & (t_new - 1) == 0
    assert t // CMP_STRIDE == LANES and SEL_BLOCK == 64 and t >= WINDOW and t // SEL_BLOCK < LANES - 1

    tm_p = 512
    kv = lambda a, b_, t_: a.reshape(b_, t_, N_KV, HEAD_DIM)
    ptab = _prompt_tables(rel_bias, t)
    stab = _sample_tables(rel_bias, t_new, pos0, n_win, LANES)
    chan_pool = lambda c: _chan_major(c).reshape(depth * n_pool, D_KV, PAGE_SIZE)
    cmp_pools = [chan_pool(cache_cmp_k), chan_pool(cache_cmp_v)]
    slc_pools = [chan_pool(cache_slc_k), chan_pool(cache_slc_v)]
    win_t = [_chan_major(cache_win_k), _chan_major(cache_win_v)]
    new_t = lambda a: jnp.pad(a.reshape(ns, t_new, LANES).transpose(0, 2, 1), ((0, 0), (0, 0), (0, LANES - t_new)))
    eye = jnp.eye(N_KV, dtype=BF16)[None, None, :, None, :, None]
    other = (jnp.arange(LANES)[None, :] // HEAD_DIM != jnp.arange(N_KV)[:, None])[None, :, :, None]

    xp = x_prompt.reshape(nb * t, dm)
    xs = x_sample.reshape(ns * t_new, dm)
    st_p, st_s = [], []
    for l in range(depth):
        lw = _layer_weights(l, norm_mix, w_in, conv_w, q_norm, k_norm, cmp_pe, cmp_w1, cmp_w2, out_norm_conv,
                            out_norm_att, w_out, norm_ffn, w_gate, w_up, w_down, d_conv, d_att, tm_p)

        gb, u, kc, vc, ks, kw, q_t, vs_t, vw_t, gates_t = _in_proj_t(
            xp, lw["norm_mix"], lw["w_tok"], lw["w_chan"], lw["q_gain_t"], lw["k_gain"], nb, d_conv, d_att, tm_p)
        seq = lambda a: a.reshape(nb, t, LANES)
        kcmp = _compress_rows(seq(kc), lw["cmp_k"], True, False)
        vcmp_t = _compress_rows(seq(vc), lw["cmp_v"], False, True)
        ks_aug = jnp.concatenate(
            [jnp.pad(seq(ks).astype(BF16), ((0, 0), (TQ, 0), (0, 0))),
             jnp.broadcast_to(ptab["sel_onehot"][None], (nb, TQ + t, LANES))], axis=2)
        kw_aug = jnp.concatenate(
            [jnp.pad(seq(kw).astype(BF16), ((0, 0), (WINDOW, 0), (0, 0))),
             jnp.broadcast_to(ptab["win_flag"][None], (nb, WINDOW + t, LANES))], axis=2)
        ones_rows = lambda v_t, front: jnp.where(
            other, 1.0, jnp.pad(v_t, ((0, 0), (0, 0), (front, 0)))[:, None]).astype(BF16)
        o_t = _pattn(q_t, gates_t, kcmp, vcmp_t, ks_aug, ones_rows(vs_t, TQ), kw_aug, ones_rows(vw_t, WINDOW), ptab)
        o_att = o_t.reshape(nb * t, d_att)
        u3 = u.reshape(nb, t, d_conv)
        xp = _mix_ffn(xp, gb, u, u, u, o_att, lw, tm_p, tiles_per_seq=t // tm_p)
        n_keep = min(WINDOW, t)
        st_p.append((u3[:, -(CONV_WIDTH - 1):], kv(kc, nb, t), kv(vc, nb, t), kv(ks, nb, t), _token_major(vs_t),
                     kv(kw, nb, t)[:, -n_keep:], _token_major(vw_t[:, :, -n_keep:])))

        n_s = ns * t_new
        gb, u, kc, vc, ks, kw, q_t, vs_t, vw_t, gates_t = _in_proj_t(
            xs, lw["norm_mix"], lw["w_tok"], lw["w_chan"], lw["q_gain_t"][:, :n_s], lw["k_gain"], 1, d_conv, d_att, n_s)
        q, vs, vw, gates = q_t[0].T, vs_t[0].T, vw_t[0].T, gates_t[0].T
        base = l * n_pool
        kcmp = _compress_paged(cmp_pools[0], page_table, base, lw["cmp_k"], True)
        vcmp = _compress_paged(cmp_pools[1], page_table, base, lw["cmp_v"], False)
        qr = q.reshape(ns, t_new, GROUP, N_KV, HEAD_DIM).transpose(0, 2, 3, 1, 4)
        qm = (qr[:, :, :, :, None, :] * eye).reshape(ns, N_HEADS * t_new, LANES)
        gr = gates[:, :3 * N_HEADS].reshape(ns, t_new, 3, N_KV, GROUP).transpose(0, 2, 4, 3, 1)
        gr = jnp.broadcast_to(gr.reshape(ns, 3, N_HEADS * t_new, 1), (ns, 3, N_HEADS * t_new, LANES))
        kw_t, vw_t = new_t(kw), new_t(vw)
        o = _sattn(qm, kcmp, vcmp, l, win_t[0], win_t[1], kw_t, vw_t, new_t(ks), new_t(vs),
                   slc_pools[0], slc_pools[1], page_table, base, gr, stab, t_new, pos0)
        o6 = o.reshape(ns, GROUP, N_KV, t_new, N_KV, HEAD_DIM)
        o_att = jnp.stack([o6[:, :, g, :, g] for g in range(N_KV)], axis=2)
        o_att = o_att.transpose(0, 3, 1, 2, 4).reshape(ns * t_new, d_att)
        u3 = u.reshape(ns, t_new, d_conv)
        u1, u2 = _shifted(u3, state_conv[l])
        xs = _mix_ffn(xs, gb, u, u1.reshape(-1, d_conv), u2.reshape(-1, d_conv), o_att, lw, ns * t_new)
        new_conv = jnp.concatenate([state_conv[l], u3], axis=1)[:, -(CONV_WIDTH - 1):]
        n_keep = min(WINDOW, n_win + t_new)
        keep = lambda c_t, n_t: _token_major(jnp.concatenate([c_t, n_t[:, :, :t_new]], axis=2)[:, :, -n_keep:])
        st_s.append((new_conv, kv(kc, ns, t_new), kv(vc, ns, t_new), kv(ks, ns, t_new), kv(vs, ns, t_new),
                     keep(win_t[0][l], kw_t), keep(win_t[1][l], vw_t)))

    outs_p = [jnp.stack(a) for a in zip(*st_p)]
    outs_s = [jnp.stack(a) for a in zip(*st_s)]
    return (xp.reshape(nb, t, dm), xs.reshape(ns, t_new, dm), *outs_p, *outs_s)
```

```python
import functools
import math

import jax
import jax.numpy as jnp
import numpy as np
from jax import lax
from jax.experimental import pallas as pl
from jax.experimental.pallas import tpu as pltpu

F32 = jnp.float32
BF16 = jnp.bfloat16

HEAD_DIM = 64
N_KV = 2
GROUP = 4
N_HEADS = N_KV * GROUP
D_KV = N_KV * HEAD_DIM
CMP_LEN = 32
CMP_STRIDE = 16
CMP_HID = 2 * HEAD_DIM
SEL_BLOCK = 64
SEL_TOPK = 16
WINDOW = 512
NUM_BUCKETS = 32
MAX_DISTANCE = 128
PAGE_SIZE = 128
CONV_WIDTH = 3
EPS = 1e-6
SCALE = HEAD_DIM ** -0.5
LOG2E = math.log2(math.e)
NEG = -1e30

LANES = 128
TQ = 128
GATE_ROWS = 32
FF_CHUNK = 256
CMP_PAGES_PER_STEP = 64
PAGES_PER_STEP = 32
VMEM_LIMIT = 56 * 1024 * 1024

_NT = (((1,), (1,)), ((), ()))


def _dot(a, b):
    return jnp.dot(a, b, preferred_element_type=F32)


def _dot_nt(a, b):
    return lax.dot_general(a, b, _NT, preferred_element_type=F32)


def _params(n_axes):
    return pltpu.CompilerParams(dimension_semantics=("arbitrary",) * n_axes,
                                vmem_limit_bytes=VMEM_LIMIT)


def _resident(shape):
    nd = len(shape)
    return pl.BlockSpec(shape, lambda *_: (0,) * nd, pipeline_mode=pl.Buffered(1))


def _rms(x, gain):
    return x * lax.rsqrt(jnp.mean(x * x, axis=-1, keepdims=True) + EPS) * gain


def _pair_rms(y, gain, lo):
    y2 = y * y
    s_lo = jnp.sum(jnp.where(lo, y2, 0.0), axis=-1, keepdims=True)
    s_hi = jnp.sum(jnp.where(lo, 0.0, y2), axis=-1, keepdims=True)
    inv = jnp.where(lo, lax.rsqrt(s_lo / HEAD_DIM + EPS), lax.rsqrt(s_hi / HEAD_DIM + EPS))
    return y * inv * gain


def _in_proj_t_kernel(x_ref, nm_ref, w_ref, wt_ref, qgt_ref, kg_ref,
                      gb_ref, u_ref, kc_ref, vc_ref, ks_ref, kw_ref, qt_ref, vst_ref, vwt_ref, gtt_ref):
    x = x_ref[...]
    hb = _rms(x, nm_ref[...]).astype(BF16)
    d = gb_ref.shape[1]

    def seg(a, b):
        return _dot(hb, w_ref[:, a:b])

    gb_ref[...] = seg(0, d)
    u_ref[...] = seg(d, 2 * d) * seg(2 * d, 3 * d)
    lo = lax.broadcasted_iota(jnp.int32, (x.shape[0], LANES), 1) < HEAD_DIM
    c = 3 * d
    kc_ref[...] = seg(c, c + LANES)
    vc_ref[...] = seg(c + LANES, c + 2 * LANES)
    ks_ref[...] = _pair_rms(seg(c + 2 * LANES, c + 3 * LANES), kg_ref[0:1, :], lo)
    kw_ref[...] = _pair_rms(seg(c + 3 * LANES, c + 4 * LANES), kg_ref[1:2, :], lo)

    yt = _dot_nt(wt_ref[...], hb)
    d_att = qt_ref.shape[1]
    for h in range(d_att // HEAD_DIM):
        y = yt[h * HEAD_DIM:(h + 1) * HEAD_DIM]
        inv = lax.rsqrt(jnp.mean(y * y, axis=0, keepdims=True) + EPS)
        qt_ref[0, h * HEAD_DIM:(h + 1) * HEAD_DIM, :] = (y * inv * qgt_ref[...]).astype(BF16)
    vst_ref[0] = yt[d_att:d_att + LANES]
    vwt_ref[0] = yt[d_att + LANES:d_att + 2 * LANES]
    gtt_ref[0] = jax.nn.sigmoid(yt[d_att + 2 * LANES:d_att + 2 * LANES + GATE_ROWS])


def _in_proj_t(x, nm, w, wt, qgt, kg, nb, d_conv, d_att, tm):
    n, dm = x.shape
    t = n // nb
    per_b = t // tm
    row = lambda w_: pl.BlockSpec((tm, w_), lambda i: (i, 0))
    col = lambda r: pl.BlockSpec((1, r, tm), lambda i: (i // per_b, 0, i % per_b))
    widths = [d_conv, d_conv] + [LANES] * 4
    rows_t = [d_att, LANES, LANES, GATE_ROWS]
    dt_t = [BF16, F32, F32, F32]
    return pl.pallas_call(
        _in_proj_t_kernel,
        grid=(n // tm,),
        in_specs=[row(dm), _resident(nm.shape), _resident(w.shape), _resident(wt.shape),
                  _resident(qgt.shape), _resident(kg.shape)],
        out_specs=[row(w_) for w_ in widths] + [col(r) for r in rows_t],
        out_shape=[jax.ShapeDtypeStruct((n, w_), F32) for w_ in widths]
        + [jax.ShapeDtypeStruct((nb, r, t), dt) for r, dt in zip(rows_t, dt_t)],
        compiler_params=_params(1),
        name="in_proj_t",
    )(x, nm, w, wt, qgt, kg)


def _cmp_ab_kernel(x_ref, w_ref, a_ref, b_ref):
    n_chunks = x_ref.shape[1] // CMP_STRIDE
    cols = [x_ref[0, pl.ds(l, n_chunks, stride=CMP_STRIDE), :] for l in range(CMP_STRIDE)]
    xr = jnp.concatenate(cols, axis=1).astype(BF16)
    ab = _dot(xr, w_ref[...])
    half = ab.shape[1] // 2
    a_ref[0] = ab[:, :half]
    b_ref[0] = ab[:, half:]


def _cmp_ab_paged_kernel(pt_ref, *refs, n_in):
    x_refs = refs[:n_in]
    perm_ref, w_ref, a_ref, b_ref = refs[n_in:]
    per_page = PAGE_SIZE // CMP_STRIDE
    rows = []
    for xr in x_refs:
        xp = _dot_nt(perm_ref[...], xr[0].astype(BF16))
        rows.append(jnp.concatenate([xp[l * per_page:(l + 1) * per_page] for l in range(CMP_STRIDE)], axis=1))
    xr = jnp.concatenate(rows, axis=0).astype(BF16)
    ab = _dot(xr, w_ref[...])
    half = ab.shape[1] // 2
    a_ref[0] = ab[:, :half]
    b_ref[0] = ab[:, half:]


def _cmp_fin_kernel(a_ref, b_ref, pe_ref, w1_ref, w2_ref, g_ref, o_ref, *, n_valid, normalize, transpose_out):
    a = a_ref[0]
    bm = b_ref[0]
    n_chunks = a.shape[0]
    b_next = pltpu.roll(bm, n_chunks - 1, 0)
    cpe = jnp.dot(pe_ref[...], w1_ref[...], preferred_element_type=F32,
                  precision=lax.Precision.HIGHEST)[0:1]
    x = a + b_next + jnp.concatenate([cpe, cpe], axis=1)
    hid = x * (0.5 * (1.0 + jnp.tanh(math.sqrt(2.0 / math.pi) * (x + 0.044715 * (x * x * x)))))
    out = _dot(hid.astype(BF16), w2_ref[...])
    if normalize:
        lo = lax.broadcasted_iota(jnp.int32, out.shape, 1) < HEAD_DIM
        out = _pair_rms(out, g_ref[...], lo)
    rows = lax.broadcasted_iota(jnp.int32, out.shape, 0)
    out = jnp.where(rows < n_valid, out, 0.0)
    o_ref[0] = out.T if transpose_out else out


def _cmp_fin(a, b, cw, n_valid, normalize, transpose_out=False):
    nb, n_chunks, wid = a.shape
    blk = pl.BlockSpec((1, n_chunks, wid), lambda i: (i, 0, 0))
    oshape = (LANES, n_chunks) if transpose_out else (n_chunks, LANES)
    return pl.pallas_call(
        functools.partial(_cmp_fin_kernel, n_valid=n_valid, normalize=normalize, transpose_out=transpose_out),
        grid=(nb,),
        in_specs=[blk, blk, _resident(cw["pe"].shape), _resident(cw["w1f"].shape),
                  _resident(cw["w2bd"].shape), _resident(cw["gain"].shape)],
        out_specs=pl.BlockSpec((1,) + oshape, lambda i: (i, 0, 0)),
        out_shape=jax.ShapeDtypeStruct((nb,) + oshape, F32),
        compiler_params=_params(1),
        name="cmp_fin",
    )(a, b, cw["pe"], cw["w1f"], cw["w2bd"], cw["gain"])


def _compress_rows(rows, cw, normalize, transpose_out):
    nb, t, _ = rows.shape
    n_chunks = t // CMP_STRIDE
    wid = cw["wab"].shape[1] // 2
    out_blk = pl.BlockSpec((1, n_chunks, wid), lambda i: (i, 0, 0))
    a, b = pl.pallas_call(
        _cmp_ab_kernel,
        grid=(nb,),
        in_specs=[pl.BlockSpec((1, t, LANES), lambda i: (i, 0, 0)), _resident(cw["wab"].shape)],
        out_specs=[out_blk, out_blk],
        out_shape=[jax.ShapeDtypeStruct((nb, n_chunks, wid), F32)] * 2,
        compiler_params=_params(1),
        name="cmp_ab_rows",
    )(rows, cw["wab"])
    return _cmp_fin(a, b, cw, (t - CMP_LEN) // CMP_STRIDE + 1, normalize, transpose_out)


def _page_spec(i, base, pages_per_step):
    return pl.BlockSpec((1, PAGE_SIZE, LANES),
                        lambda b, s, pt: (base + pt[b, s * pages_per_step + i], 0, 0))


def _compress_paged(cache, page_table, base, cw, normalize):
    nb, n_pages = page_table.shape
    pps = CMP_PAGES_PER_STEP
    per_page = PAGE_SIZE // CMP_STRIDE
    ch = pps * per_page
    wid = cw["wab"].shape[1] // 2
    n_chunks = n_pages * per_page
    out_blk = pl.BlockSpec((1, ch, wid), lambda b, s, pt: (b, s, 0))
    r = np.arange(PAGE_SIZE)
    perm = jnp.asarray(r[None, :] == (r % per_page)[:, None] * CMP_STRIDE + (r // per_page)[:, None], BF16)
    cst = lambda a: pl.BlockSpec(a.shape, lambda b, s, pt: (0, 0), pipeline_mode=pl.Buffered(1))
    a, b = pl.pallas_call(
        functools.partial(_cmp_ab_paged_kernel, n_in=pps),
        grid_spec=pltpu.PrefetchScalarGridSpec(
            num_scalar_prefetch=1,
            grid=(nb, n_pages // pps),
            in_specs=[_page_spec(i, base, pps) for i in range(pps)] + [cst(perm), cst(cw["wab"])],
            out_specs=[out_blk, out_blk]),
        out_shape=[jax.ShapeDtypeStruct((nb, n_chunks, wid), F32)] * 2,
        compiler_params=_params(2),
        name="cmp_ab_paged",
    )(page_table, *([cache] * pps), perm, cw["wab"])
    return _cmp_fin(a, b, cw, n_chunks - 1, normalize)


def _split_bf16(x):
    hi = x.astype(BF16)
    return hi, (x - hi.astype(F32)).astype(BF16)


def _online_update(state, s, mask, v_t):
    m, l, acc = state
    s = jnp.where(mask, s, NEG)
    m_new = jnp.maximum(m, jnp.max(s, axis=-1, keepdims=True))
    alpha = jnp.exp2(m - m_new)
    e = jnp.where(mask, jnp.exp2(s - m_new), 0.0)
    l = alpha * l + jnp.sum(e, axis=-1, keepdims=True)
    acc = alpha * acc + _dot_nt(e.astype(BF16), v_t)
    return m_new, l, acc


def _finish(state):
    _, l, acc = state
    return acc / jnp.where(l > 0, l, 1.0)


def _reset_t(m_ref, acc_ref):
    m_ref[...] = jnp.full(m_ref.shape, NEG, F32)
    acc_ref[...] = jnp.zeros(acc_ref.shape, F32)


def _update_t(m_ref, acc_ref, s_t, v_t):
    m = m_ref[...]
    m_new = jnp.maximum(m, jnp.max(s_t, axis=0, keepdims=True))
    alpha = jnp.exp2(m - m_new)
    e = jnp.exp2(s_t - m_new).astype(BF16)
    acc_ref[...] = alpha * acc_ref[...] + _dot(v_t, e)
    m_ref[...] = m_new


def _pattn_kernel(qt_ref, gt_ref, kcmp_ref, vcmpt_ref, ksa_ref, vst_ref, kwa_ref, vwt_ref,
                  bcmp_ref, stab_ref, wa_ref, wb_ref, diag_ref, covt_ref, o_ref, m_ref, acc_ref, mw_ref, accw_ref, qa_ref, out_ref, sn_ref, sl_ref, *, n_sel, topk):
    qi = pl.program_id(1)
    q0 = qi * TQ
    n_slabs = qt_ref.shape[1] // LANES
    wide = n_slabs * TQ
    sub = lax.broadcasted_iota(jnp.int32, (LANES, TQ), 0)
    lane = lax.broadcasted_iota(jnp.int32, (LANES, TQ), 1)
    sub_w = lax.broadcasted_iota(jnp.int32, (LANES, wide), 0)
    tpos_w = q0 + (lax.broadcasted_iota(jnp.int32, (LANES, wide), 1) & (TQ - 1))
    kcmp = kcmp_ref[0].astype(BF16)
    vcmp_t = vcmpt_ref[0].astype(BF16)
    covt = covt_ref[...]
    gt = gt_ref[0]
    zero_rows = jnp.zeros((LANES - n_sel, TQ), F32)
    def gate(br, g):
        return jnp.concatenate([gt[br * N_HEADS + g * GROUP + j:br * N_HEADS + g * GROUP + j + 1, :]
                                for j in range(n_slabs)], axis=1)

    def keys(k_ref, v_ref, g, kt, n):
        start = pl.multiple_of(kt * TQ, TQ)
        return k_ref[0, pl.ds(start, n * TQ), :], v_ref[0, g, :, pl.ds(start, n * TQ)]

    def add_branch(br, acc_ref_):
        for g in range(N_KV):
            acc = acc_ref_[g]
            den = acc[HEAD_DIM:HEAD_DIM + 1] if g == 0 else acc[0:1]
            out_ref[g] += (gate(br, g) * (1.0 / den)) * acc

    q_ts = []
    for g in range(N_KV):
        mine = (sub < HEAD_DIM) if g == 0 else (sub >= HEAD_DIM)
        q_ts.append(jnp.concatenate(
            [jnp.where(mine, qt_ref[0, j * LANES:(j + 1) * LANES, :], jnp.zeros((), BF16))
             for j in range(n_slabs)], axis=1))
        qa_ref[g, 0:LANES, :] = q_ts[g]

    pad_pen = jnp.where(sub_w == 0, NEG, 0.0).astype(BF16)
    win = []
    for g in range(N_KV):
        q_win = jnp.concatenate([q_ts[g], pad_pen], axis=0)
        jobs = []
        for kt, n, tab in ((qi, 2, wa_ref[...]), (qi + 2, 2, wb_ref[g]), (qi + 4, 1, diag_ref[g])):
            k_, v_ = keys(kwa_ref, vwt_ref, g, kt, n)
            jobs.append((_dot(k_, q_win) + tab, v_))
        win.append(jobs)
    n_blk = bcmp_ref.shape[1] // 2
    row0 = pl.multiple_of(n_blk - qi * (TQ // CMP_STRIDE), TQ // CMP_STRIDE)
    s_cmp = [_dot(kcmp, q_ts[g]) + bcmp_ref[g, pl.ds(row0, n_blk), :] for g in range(N_KV)]

    for g in range(N_KV):
        mask = sub_w * CMP_STRIDE + (CMP_LEN - 1) <= tpos_w
        s = jnp.where(mask, s_cmp[g], NEG)
        e = jnp.where(mask, jnp.exp2(s - jnp.max(s, axis=0, keepdims=True)), 0.0)
        l = jnp.sum(e, axis=0, keepdims=True)
        p = e * (1.0 / jnp.where(l > 0, l, 1.0))
        out_ref[g] = gate(0, g) * _dot(vcmp_t, p.astype(BF16))
        p_sum = p[:, 0:TQ]
        for j in range(1, n_slabs):
            p_sum = p_sum + p[:, j * TQ:(j + 1) * TQ]

        hi, lo_ = _split_bf16(p_sum)
        sc = _dot(covt, hi) + _dot(covt, lo_)
        tq_pos = q0 + lane
        cur = tq_pos >> 6
        forced = (sub == 0) | (sub == cur) | (sub == cur - 1)
        sc = jnp.where(forced, 1e6, jnp.where(sub * SEL_BLOCK > tq_pos, -1e6, sc))
        sc = sc[0:n_sel]
        blk = sub[0:n_sel]
        rank = jnp.zeros((n_sel, TQ), F32)
        for i in range(n_sel):
            si = sc[i:i + 1, :]
            beats = (si > sc) | ((si == sc) & (blk > i))
            rank = rank + jnp.where(beats, 1.0, 0.0)
        pen = jnp.concatenate([jnp.where(rank < topk, 0.0, NEG), zero_rows], axis=0)
        pen = jnp.where(sub == LANES - 1, NEG, pen).astype(BF16)
        qa_ref[g, LANES:2 * LANES, :] = jnp.concatenate([pen] * n_slabs, axis=1)

    _reset_t(mw_ref, accw_ref)
    for job in range(len(win[0])):
        for g in range(N_KV):
            _update_t(mw_ref.at[g], accw_ref.at[g], *win[g][job])

    _reset_t(m_ref, acc_ref)
    n_far = jnp.maximum(qi - 1, 0)

    n_pairs = n_far // 2
    odd = jnp.where(n_far % 2 == 1, n_far, 0)
    for g in range(N_KV):
        sn_ref[g] = _dot(keys(ksa_ref, vst_ref, g, 1, 2)[0], qa_ref[g])
        k_last = jnp.concatenate([keys(ksa_ref, vst_ref, g, odd, 1)[0], keys(ksa_ref, vst_ref, g, qi, 2)[0]], axis=0)
        sl_ref[g] = _dot(k_last, qa_ref[g]) + stab_ref[g]
    add_branch(2, accw_ref)

    @pl.loop(0, n_pairs)
    def _(i):
        nxt = 1 + 2 * jnp.minimum(i + 1, n_pairs - 1)
        s_next = [_dot(keys(ksa_ref, vst_ref, g, nxt, 2)[0], qa_ref[g]) for g in range(N_KV)]
        for g in range(N_KV):
            _update_t(m_ref.at[g], acc_ref.at[g], sn_ref[g], keys(ksa_ref, vst_ref, g, 1 + 2 * i, 2)[1])
        for g in range(N_KV):
            sn_ref[g] = s_next[g]

    for g in range(N_KV):
        v_last = jnp.concatenate([keys(ksa_ref, vst_ref, g, odd, 1)[1], keys(ksa_ref, vst_ref, g, qi, 2)[1]], axis=1)
        _update_t(m_ref.at[g], acc_ref.at[g], sl_ref[g], v_last)
    add_branch(1, acc_ref)

    out = jnp.where(sub_w < HEAD_DIM, out_ref[0], out_ref[1])
    for j in range(n_slabs):
        o_ref[0, :, j * LANES:(j + 1) * LANES] = out[:, j * TQ:(j + 1) * TQ].T


def _pattn(q_t, gates_t, kcmp, vcmp_t, ks_aug, vs_t, kw_aug, vw_t, tabs):
    nb, d_att, t = q_t.shape
    n_q = t // TQ
    n_sel = -(-t // SEL_BLOCK)
    n_slabs = d_att // LANES
    whole = lambda a: pl.BlockSpec((1,) + a.shape[1:], lambda b, i: (b,) + (0,) * (a.ndim - 1))
    return pl.pallas_call(
        functools.partial(_pattn_kernel, n_sel=n_sel, topk=min(SEL_TOPK, n_sel)),
        grid=(nb, n_q),
        in_specs=[pl.BlockSpec((1, d_att, TQ), lambda b, i: (b, 0, i)),
                  pl.BlockSpec((1, GATE_ROWS, TQ), lambda b, i: (b, 0, i)),
                  whole(kcmp), whole(vcmp_t), whole(ks_aug), whole(vs_t), whole(kw_aug), whole(vw_t),
                  _resident(tabs["bcmp"].shape),
                  _resident(tabs["sel"].shape), _resident(tabs["win_a"].shape), _resident(tabs["win_b"].shape),
                  _resident(tabs["diag"].shape), _resident(tabs["covt"].shape)],
        out_specs=pl.BlockSpec((1, TQ, d_att), lambda b, i: (b, i, 0)),
        out_shape=jax.ShapeDtypeStruct((nb, t, d_att), F32),
        scratch_shapes=[pltpu.VMEM((N_KV, 1, n_slabs * TQ), F32), pltpu.VMEM((N_KV, LANES, n_slabs * TQ), F32)] * 2
        + [pltpu.VMEM((N_KV, 2 * LANES, n_slabs * TQ), BF16), pltpu.VMEM((N_KV, LANES, n_slabs * TQ), F32),
           pltpu.VMEM((N_KV, 2 * TQ, n_slabs * TQ), F32), pltpu.VMEM((N_KV, 3 * TQ, n_slabs * TQ), F32)],
        compiler_params=_params(2),
        name="pattn",
    )(q_t, gates_t, kcmp, vcmp_t, ks_aug, vs_t, kw_aug, vw_t, tabs["bcmp"], tabs["sel"], tabs["win_a"],
      tabs["win_b"], tabs["diag"], tabs["covt"])


def _sattn1_kernel(qm_ref, kcmp_ref, vcmp_ref, kwc_ref, vwc_ref, kwn_ref, vwn_ref,
                   bcmp_ref, bwc_ref, bwn_ref, cov_ref, ocmp_ref, owin_ref, sel_ref,
                   *, n_cmp, n_sel, topk, t_new, pos0, win_start):
    qm = qm_ref[0]
    n_rows = qm.shape[0]
    n_gt = N_KV * t_new
    s = _dot_nt(qm, kcmp_ref[0].astype(BF16)) + bcmp_ref[...]
    ncol = lax.broadcasted_iota(jnp.int32, s.shape, 1)
    trow = pos0 + (lax.broadcasted_iota(jnp.int32, s.shape, 0) & (t_new - 1))
    mask = (ncol < n_cmp) & (ncol * CMP_STRIDE + (CMP_LEN - 1) <= trow)
    s = jnp.where(mask, s, NEG)
    e = jnp.where(mask, jnp.exp2(s - jnp.max(s, axis=-1, keepdims=True)), 0.0)
    l = jnp.sum(e, axis=-1, keepdims=True)
    p = e / jnp.where(l > 0, l, 1.0)
    ocmp_ref[0] = _dot(p.astype(BF16), vcmp_ref[0].astype(BF16))
    p_sum = p[0:n_gt]
    for j in range(1, n_rows // n_gt):
        p_sum = p_sum + p[j * n_gt:(j + 1) * n_gt]
    hi, lo = _split_bf16(p_sum)
    sc = _dot(hi, cov_ref[...]) + _dot(lo, cov_ref[...])
    blk = lax.broadcasted_iota(jnp.int32, sc.shape, 1)
    qpos = pos0 + (lax.broadcasted_iota(jnp.int32, sc.shape, 0) & (t_new - 1))
    cur = qpos >> 6
    forced = (blk == 0) | (blk == cur) | (blk == cur - 1)
    sc = jnp.where(forced, 1e6, jnp.where(blk * SEL_BLOCK > qpos, -1e6, sc))
    sc = jnp.where(blk < n_sel, sc, -3e38)
    rank = jnp.zeros(sc.shape, F32)
    for i in range(n_sel):
        si = sc[:, i:i + 1]
        beats = (si > sc) | ((si == sc) & (blk > i))
        rank = rank + jnp.where(beats, 1.0, 0.0)
    sel_ref[0] = jnp.where((rank < topk) & (blk < n_sel), 1.0, 0.0)

    init = (jnp.full((n_rows, 1), NEG, F32), jnp.zeros((n_rows, 1), F32), jnp.zeros((n_rows, LANES), F32))
    s = _dot(qm, kwc_ref[0, 0].astype(BF16)) + bwc_ref[...]
    trow = pos0 + (lax.broadcasted_iota(jnp.int32, s.shape, 0) & (t_new - 1))
    dist = trow - (win_start + lax.broadcasted_iota(jnp.int32, s.shape, 1))
    st = _online_update(init, s, (dist >= 0) & (dist < WINDOW), vwc_ref[0, 0].astype(BF16))
    s = _dot(qm, kwn_ref[0].astype(BF16)) + bwn_ref[...]
    trow = pos0 + (lax.broadcasted_iota(jnp.int32, s.shape, 0) & (t_new - 1))
    col = lax.broadcasted_iota(jnp.int32, s.shape, 1)
    dist = trow - (pos0 + col)
    st = _online_update(st, s, (dist >= 0) & (dist < WINDOW) & (col < t_new), vwn_ref[0].astype(BF16))
    owin_ref[0] = _finish(st)


def _sattn2_kernel(pt_ref, *refs, pps, t_new):
    k_refs = refs[:pps]
    v_refs = refs[pps:2 * pps]
    (qm_ref, selc_ref, seln_ref, ksn_ref, vsn_ref, bsel_ref, bseln_ref, exp_ref,
     ocmp_ref, owin_ref, gate_ref, o_ref, m_sc, l_sc, acc_sc) = refs[2 * pps:]
    step = pl.program_id(1)
    qm = qm_ref[0]
    n_rows = qm.shape[0]
    reps = n_rows // selc_ref.shape[2]

    @pl.when(step == 0)
    def _():
        s = _dot(qm, ksn_ref[0].astype(BF16)) + bseln_ref[...]
        trow = lax.broadcasted_iota(jnp.int32, s.shape, 0) & (t_new - 1)
        col = lax.broadcasted_iota(jnp.int32, s.shape, 1)
        hit = jnp.concatenate([seln_ref[0]] * reps, axis=0) > 0.5
        init = (jnp.full((n_rows, 1), NEG, F32), jnp.zeros((n_rows, 1), F32),
                jnp.zeros((n_rows, LANES), F32))
        m, l, acc = _online_update(init, s, hit & (col <= trow) & (col < t_new), vsn_ref[0].astype(BF16))
        m_sc[...] = m
        l_sc[...] = l
        acc_sc[...] = acc

    k_t = jnp.concatenate([r[0] for r in k_refs], axis=1).astype(BF16)
    v_t = jnp.concatenate([r[0] for r in v_refs], axis=1).astype(BF16)
    s = _dot(qm, k_t) + bsel_ref[...]
    hit = _dot(selc_ref[0, 0].astype(BF16), exp_ref[...])
    mask = jnp.concatenate([hit] * reps, axis=0) > 0.5
    m, l, acc = _online_update((m_sc[...], l_sc[...], acc_sc[...]), s, mask, v_t)
    m_sc[...] = m
    l_sc[...] = l
    acc_sc[...] = acc

    @pl.when(step == pl.num_programs(1) - 1)
    def _():
        o_sel = _finish((m, l, acc))
        o_ref[0] = gate_ref[0, 0] * ocmp_ref[0] + gate_ref[0, 1] * o_sel + gate_ref[0, 2] * owin_ref[0]


def _sattn(qm, kcmp, vcmp, layer, kwc_t, vwc_t, kwn_t, vwn_t, ksn_t, vsn_t, cache_k, cache_v, page_table, base,
           gates, tabs, t_new, pos0):
    nb, n_rows, _ = qm.shape
    n_pages = page_table.shape[1]
    n_chunks = kcmp.shape[1]
    n_gt = N_KV * t_new
    n_sel = -(-(pos0 + t_new) // SEL_BLOCK)
    selw = tabs["cov"].shape[1]
    per_b = lambda shape: pl.BlockSpec((1,) + shape, lambda b: (b,) + (0,) * len(shape))
    win_c = pl.BlockSpec((1, 1) + kwc_t.shape[2:], lambda b: (layer, b, 0, 0))
    ocmp, owin, sel = pl.pallas_call(
        functools.partial(_sattn1_kernel, n_cmp=n_chunks - 1, n_sel=n_sel, topk=min(SEL_TOPK, n_sel),
                          t_new=t_new, pos0=pos0, win_start=pos0 - kwc_t.shape[3]),
        grid=(nb,),
        in_specs=[per_b((n_rows, LANES)), per_b((n_chunks, LANES)), per_b((n_chunks, LANES)),
                  win_c, win_c, per_b(kwn_t.shape[1:]), per_b(vwn_t.shape[1:]),
                  _resident(tabs["bcmp"].shape), _resident(tabs["bwc"].shape), _resident(tabs["bwn"].shape),
                  _resident(tabs["cov"].shape)],
        out_specs=[per_b((n_rows, LANES)), per_b((n_rows, LANES)), per_b((n_gt, selw))],
        out_shape=[jax.ShapeDtypeStruct((nb, n_rows, LANES), F32), jax.ShapeDtypeStruct((nb, n_rows, LANES), F32),
                   jax.ShapeDtypeStruct((nb, n_gt, selw), F32)],
        compiler_params=_params(1),
        name="sattn1",
    )(qm, kcmp, vcmp, kwc_t, vwc_t, kwn_t, vwn_t, tabs["bcmp"], tabs["bwc"], tabs["bwn"], tabs["cov"])

    pps = PAGES_PER_STEP
    n_steps = n_pages // pps
    bps = pps * PAGE_SIZE // SEL_BLOCK
    n_cblk = n_pages * PAGE_SIZE // SEL_BLOCK
    selc = sel[:, :, :n_cblk].reshape(nb, n_gt, n_steps, bps).transpose(0, 2, 1, 3)
    selc = jnp.pad(selc, ((0, 0), (0, 0), (0, 0), (0, LANES - bps)))
    seln = jnp.broadcast_to(sel[:, :, n_cblk:n_cblk + 1], (nb, n_gt, LANES))

    keys = pps * PAGE_SIZE
    cst = lambda shape: pl.BlockSpec(shape, lambda b, s, pt: (0,) * len(shape), pipeline_mode=pl.Buffered(1))
    pb = lambda shape: pl.BlockSpec((1,) + shape, lambda b, s, pt: (b,) + (0,) * len(shape))
    return pl.pallas_call(
        functools.partial(_sattn2_kernel, pps=pps, t_new=t_new),
        grid_spec=pltpu.PrefetchScalarGridSpec(
            num_scalar_prefetch=1,
            grid=(nb, n_steps),
            in_specs=[_page_spec(i, base, pps) for i in range(pps)] * 2
            + [pb((n_rows, LANES)),
               pl.BlockSpec((1, 1, n_gt, LANES), lambda b, s, pt: (b, s, 0, 0)),
               pb((n_gt, LANES)), pb(ksn_t.shape[1:]), pb(vsn_t.shape[1:]),
               pl.BlockSpec((n_rows, keys), lambda b, s, pt: (0, s)),
               cst(tabs["bseln"].shape), cst(tabs["expand"].shape),
               pb((n_rows, LANES)), pb((n_rows, LANES)), pb((3, n_rows, LANES))],
            out_specs=pb((n_rows, LANES)),
            scratch_shapes=[pltpu.VMEM((n_rows, 1), F32), pltpu.VMEM((n_rows, 1), F32),
                            pltpu.VMEM((n_rows, LANES), F32)]),
        out_shape=jax.ShapeDtypeStruct((nb, n_rows, LANES), F32),
        compiler_params=_params(2),
        name="sattn2",
    )(page_table, *([cache_k] * pps), *([cache_v] * pps), qm, selc, seln, ksn_t, vsn_t,
      tabs["bsel"], tabs["bseln"], tabs["expand"], ocmp, owin, gates)


def _mix_ffn_kernel(x_ref, gb_ref, u_ref, u1_ref, u2_ref, oa_ref, cw_ref, nc_ref, na_ref, wo_ref,
                    nf_ref, wg_ref, wu_ref, wd_ref, y_ref, acc_ref, *, tiles_per_seq):
    d = gb_ref.shape[1]
    u = u_ref[...]
    if tiles_per_seq is None:
        u1, u2 = u1_ref[...], u2_ref[...]
    else:
        prev = u1_ref[...] * jnp.where(pl.program_id(0) % tiles_per_seq == 0, 0.0, 1.0)
        row = lax.broadcasted_iota(jnp.int32, u.shape, 0)
        u1 = jnp.where(row == 0, prev[7:8], pltpu.roll(u, 1, 0))
        u2 = jnp.where(row == 0, prev[6:7], jnp.where(row == 1, prev[7:8], pltpu.roll(u, 2, 0)))
    conv = u2 * cw_ref[0:1, :] + u1 * cw_ref[1:2, :] + u * cw_ref[2:3, :]
    yc = _rms(gb_ref[...] * conv, nc_ref[...]).astype(BF16)
    oa = _rms(oa_ref[...], na_ref[...]).astype(BF16)
    x1 = x_ref[...] + (_dot(yc, wo_ref[0:d, :]) + _dot(oa, wo_ref[d:, :]))
    h2 = _rms(x1, nf_ref[...]).astype(BF16)
    acc_ref[...] = jnp.zeros_like(acc_ref)
    for c in range(0, wg_ref.shape[1], FF_CHUNK):
        gate = _dot(h2, wg_ref[:, c:c + FF_CHUNK])
        act = (gate * jax.nn.sigmoid(gate)) * _dot(h2, wu_ref[:, c:c + FF_CHUNK])
        acc_ref[...] += _dot(act.astype(BF16), wd_ref[c:c + FF_CHUNK, :])
    y_ref[...] = x1 + acc_ref[...]


def _mix_ffn(x, gb, u, u1, u2, oa, lw, tm, tiles_per_seq=None):
    n, dm = x.shape
    d = gb.shape[1]
    row = lambda w_: pl.BlockSpec((tm, w_), lambda i: (i, 0))
    hist = row(d) if tiles_per_seq is None else pl.BlockSpec((8, d), lambda i: (jnp.maximum(i * (tm // 8) - 1, 0), 0))
    consts = [lw["conv_w"], lw["norm_conv"], lw["norm_att"], lw["w_out"], lw["norm_ffn"],
              lw["w_gate"], lw["w_up"], lw["w_down"]]
    return pl.pallas_call(
        functools.partial(_mix_ffn_kernel, tiles_per_seq=tiles_per_seq),
        grid=(n // tm,),
        in_specs=[row(dm), row(d), row(d), hist, hist, row(d)] + [_resident(c.shape) for c in consts],
        out_specs=row(dm),
        out_shape=jax.ShapeDtypeStruct((n, dm), F32),
        scratch_shapes=[pltpu.VMEM((tm, dm), F32)],
        compiler_params=_params(1),
        name="mix_ffn",
    )(x, gb, u, u1, u2, oa, *consts)


def _t5_bucket(dist):
    n = jnp.maximum(dist, 0)
    max_exact = NUM_BUCKETS // 2
    nf = jnp.maximum(n, 1).astype(F32)
    large = max_exact + (jnp.log(nf / max_exact) / math.log(MAX_DISTANCE / max_exact)
                         * (NUM_BUCKETS - max_exact)).astype(jnp.int32)
    return jnp.where(n < max_exact, n, jnp.minimum(large, NUM_BUCKETS - 1))


def _bias_lut(tbl):
    return tbl[_t5_bucket(jnp.arange(MAX_DISTANCE + 1, dtype=jnp.int32))]


def _bias(dist, lut):
    idx = jnp.clip(dist, 0, MAX_DISTANCE)
    onehot = (idx[..., None] == jnp.arange(MAX_DISTANCE + 1, dtype=jnp.int32)).astype(F32)
    return jnp.einsum("...k,kh->h...", onehot, lut, precision=lax.Precision.HIGHEST)


def _cover(n_cmp, n_sel):
    c0 = np.arange(n_cmp)[:, None] * CMP_STRIDE
    s0 = np.arange(n_sel)[None, :] * SEL_BLOCK
    return np.clip(np.minimum(c0 + CMP_LEN, s0 + SEL_BLOCK) - np.maximum(c0, s0), 0, None) / CMP_STRIDE


def _prompt_tables(tbl, t):
    n_q = t // TQ
    n_chunks = t // CMP_STRIDE
    ar = jnp.arange(TQ, dtype=jnp.int32)
    lut = _bias_lut(tbl)
    far = lut[MAX_DISTANCE]
    assert MAX_DISTANCE <= TQ

    def key_major(b):
        k, q = b.shape[1:]
        return b.reshape(N_KV, GROUP, k, q).transpose(0, 2, 1, 3).reshape(N_KV, k, GROUP * q)

    near = []
    for c in range(2):
        dist = c * TQ + ar[None, :] - ar[:, None]
        b = (_bias(dist, lut) - far[:, None, None]) * LOG2E
        near.append(key_major(jnp.where(dist[None] >= 0, b, NEG)))
    oldest = jnp.where(ar[:, None] > ar[None, :], 0.0, NEG)
    near.append(key_major(jnp.broadcast_to(oldest[None], (N_HEADS, TQ, TQ))))
    rel = np.arange(n_chunks)
    rel = np.where(rel < n_chunks // 2, rel, rel - n_chunks)
    dist = np.arange(TQ)[None, :] - (rel[:, None] * CMP_STRIDE + (CMP_LEN - 1))
    dist = np.where(rel[:, None] >= TQ // CMP_STRIDE, MAX_DISTANCE, dist)
    assert (TQ // CMP_STRIDE) * (n_q - 1) < n_chunks and (n_chunks // 2) * CMP_STRIDE >= MAX_DISTANCE + CMP_LEN
    b = key_major(_bias(jnp.asarray(dist, jnp.int32), lut) * LOG2E)
    b = jnp.concatenate([b, b], axis=1)
    n_cmp = (t - CMP_LEN) // CMP_STRIDE + 1
    n_sel = -(-t // SEL_BLOCK)
    covt = np.zeros((LANES, n_chunks), np.float32)
    covt[:n_sel, :n_cmp] = _cover(n_cmp, n_sel).T
    keys = np.arange(-TQ, t)
    onehot = (keys[:, None] // SEL_BLOCK == np.arange(LANES)[None, :]) & (keys[:, None] >= 0)
    onehot[:, LANES - 1] = keys < 0
    wkeys = np.arange(-WINDOW, t)
    wflag = np.zeros((WINDOW + t, LANES), np.float32)
    wflag[:, 0] = wkeys < 0
    diag, prev, oldest = near
    none = jnp.zeros_like(prev)
    tabs = dict(sel=jnp.concatenate([none, prev, diag], axis=1), win_a=jnp.concatenate([oldest, none], axis=1)[0],
                win_b=jnp.concatenate([none, prev], axis=1), diag=diag)
    return dict(**tabs, bcmp=b, covt=jnp.asarray(covt, BF16),
                sel_onehot=jnp.asarray(onehot.astype(np.float32), BF16), win_flag=jnp.asarray(wflag, BF16))


def _sample_tables(tbl, t_new, pos0, n_win, n_pad):
    n_chunks = pos0 // CMP_STRIDE
    n_cmp = n_chunks - 1
    n_sel = -(-(pos0 + t_new) // SEL_BLOCK)
    selw = -(-n_sel // LANES) * LANES
    qpos = pos0 + jnp.arange(t_new, dtype=jnp.int32)
    lut = _bias_lut(tbl)

    def rows(dist):
        b = _bias(dist, lut) * LOG2E
        return b.reshape(N_KV, GROUP, t_new, -1).transpose(1, 0, 2, 3).reshape(N_HEADS * t_new, -1)

    blk_end = jnp.arange(n_chunks, dtype=jnp.int32) * CMP_STRIDE + (CMP_LEN - 1)
    new_pos = pos0 + jnp.arange(n_pad, dtype=jnp.int32)
    cov = np.zeros((n_chunks, selw), np.float32)
    cov[:n_cmp, :n_sel] = _cover(n_cmp, n_sel)
    keys = PAGES_PER_STEP * PAGE_SIZE
    expand = (np.arange(LANES)[:, None] == np.arange(keys)[None, :] // SEL_BLOCK).astype(np.float32)
    bnew = rows(qpos[:, None] - new_pos[None, :])
    return dict(
        bcmp=rows(qpos[:, None] - blk_end[None, :]),
        bwc=rows(qpos[:, None] - (pos0 - n_win + jnp.arange(n_win, dtype=jnp.int32))[None, :]),
        bwn=bnew, bseln=bnew,
        bsel=rows(qpos[:, None] - jnp.arange(pos0, dtype=jnp.int32)[None, :]),
        cov=jnp.asarray(cov, BF16), expand=jnp.asarray(expand, BF16))


def _layer_weights(l, norm_mix, w_in, conv_w, q_norm, k_norm, cmp_pe, cmp_w1, cmp_w2, out_norm_conv,
                   out_norm_att, w_out, norm_ffn, w_gate, w_up, w_down, d_conv, d_att, tm_p):
    perm = np.arange(d_att).reshape(N_KV, GROUP, HEAD_DIM).transpose(1, 0, 2).reshape(-1)
    q0 = 3 * d_conv
    w = w_in[l]
    k0 = q0 + d_att
    n_gates = w.shape[1] - (k0 + 6 * D_KV)
    seg = lambda i: w[:, k0 + i * D_KV:k0 + (i + 1) * D_KV]
    gates = jnp.concatenate([w[:, k0 + 6 * D_KV:], jnp.zeros((w.shape[0], GATE_ROWS - n_gates), w.dtype)], axis=1)
    w_tok = jnp.concatenate([w[:, :q0], seg(0), seg(1), seg(2), seg(4)], axis=1).astype(BF16)
    w_chan = jnp.concatenate([w[:, q0:k0][:, perm], seg(3), seg(5), gates], axis=1).T.astype(BF16)
    wo = w_out[l]
    wo = jnp.concatenate([wo[:d_conv], wo[d_conv:][perm]], axis=0).astype(BF16)
    pair = lambda v: jnp.tile(v, 2)[None, :]
    q_gain = q_norm[l] * (SCALE * LOG2E)
    eye2 = jnp.eye(N_KV, dtype=F32)

    def cmp_weights(i, gain):
        w1 = cmp_w1[l, i]
        w2 = cmp_w2[l, i]
        wab = jnp.einsum("hrde,gk->rgdhke", w1.reshape(2, CMP_STRIDE, HEAD_DIM, CMP_HID), eye2)
        return dict(
            wab=wab.reshape(CMP_STRIDE * D_KV, 2 * N_KV * CMP_HID).astype(BF16),
            pe=jnp.broadcast_to(cmp_pe[l, i].reshape(1, -1), (8, CMP_LEN * HEAD_DIM)),
            w1f=w1.reshape(CMP_LEN * HEAD_DIM, CMP_HID),
            w2bd=jnp.einsum("ed,gk->gekd", w2, eye2).reshape(N_KV * CMP_HID, D_KV).astype(BF16),
            gain=gain)

    return dict(
        norm_mix=norm_mix[l][None, :], w_tok=w_tok, w_chan=w_chan,
        q_gain_t=jnp.broadcast_to(q_gain[:, None], (HEAD_DIM, tm_p)),
        k_gain=jnp.stack([jnp.tile(k_norm[l, 1], 2), jnp.tile(k_norm[l, 2], 2)]),
        cmp_k=cmp_weights(0, pair(k_norm[l, 0])), cmp_v=cmp_weights(1, jnp.ones((1, LANES), F32)),
        conv_w=jnp.pad(conv_w[l], ((0, 8 - CONV_WIDTH), (0, 0))),
        norm_conv=out_norm_conv[l][None, :], norm_att=out_norm_att[l][perm][None, :], w_out=wo,
        norm_ffn=norm_ffn[l][None, :],
        w_gate=w_gate[l].astype(BF16), w_up=w_up[l].astype(BF16), w_down=w_down[l].astype(BF16))


def _shifted(u, hist):
    up = jnp.concatenate([hist, u], axis=1)
    t = u.shape[1]
    return up[:, 1:1 + t], up[:, 0:t]


def _chan_major(a):
    a = jnp.moveaxis(a, -3, -1)
    return a.reshape(a.shape[:-3] + (D_KV, a.shape[-1]))


def _token_major(a_t):
    a = a_t.reshape(a_t.shape[:-2] + (N_KV, HEAD_DIM, a_t.shape[-1]))
    return jnp.moveaxis(a, -1, -3)


def kernel(x_prompt, x_sample, state_conv, cache_cmp_k, cache_cmp_v, cache_slc_k, cache_slc_v,
           cache_win_k, cache_win_v, page_table, rel_bias, norm_mix, w_in, conv_w, q_norm, k_norm,
           cmp_pe, cmp_w1, cmp_w2, out_norm_conv, out_norm_att, w_out, norm_ffn, w_gate, w_up, w_down):
    nb, t, dm = x_prompt.shape
    ns, t_new, _ = x_sample.shape
    depth = w_in.shape[0]
    d_conv = conv_w.shape[2]
    d_att = dm - d_conv
    n_pool = cache_cmp_k.shape[1]
    n_pages = page_table.shape[1]
    pos0 = n_pages * PAGE_SIZE
    n_win = cache_win_k.shape[2]
    assert t % TQ == 0 and d_att == N_HEADS * HEAD_DIM
    assert n_pages % PAGES_PER_STEP == 0 and n_pages % CMP_PAGES_PER_STEP == 0
    assert n_win == WINDOW and WINDOW == 4 * TQ and pos0 >= WINDOW and t_new <= LANES and t_new & (t_new - 1) == 0
    assert t // CMP_STRIDE == LANES and SEL_BLOCK == 64 and t >= WINDOW and t // SEL_BLOCK < LANES - 1

    tm_p = 512
    kv = lambda a, b_, t_: a.reshape(b_, t_, N_KV, HEAD_DIM)
    ptab = _prompt_tables(rel_bias, t)
    stab = _sample_tables(rel_bias, t_new, pos0, n_win, LANES)
    chan_pool = lambda c: _chan_major(c).reshape(depth * n_pool, D_KV, PAGE_SIZE)
    cmp_pools = [chan_pool(cache_cmp_k), chan_pool(cache_cmp_v)]
    slc_pools = [chan_pool(cache_slc_k), chan_pool(cache_slc_v)]
    win_t = [_chan_major(cache_win_k), _chan_major(cache_win_v)]
    new_t = lambda a: jnp.pad(a.reshape(ns, t_new, LANES).transpose(0, 2, 1), ((0, 0), (0, 0), (0, LANES - t_new)))
    eye = jnp.eye(N_KV, dtype=BF16)[None, None, :, None, :, None]
    other = (jnp.arange(LANES)[None, :] // HEAD_DIM != jnp.arange(N_KV)[:, None])[None, :, :, None]

    xp = x_prompt.reshape(nb * t, dm)
    xs = x_sample.reshape(ns * t_new, dm)
    st_p, st_s = [], []
    for l in range(depth):
        lw = _layer_weights(l, norm_mix, w_in, conv_w, q_norm, k_norm, cmp_pe, cmp_w1, cmp_w2, out_norm_conv,
                            out_norm_att, w_out, norm_ffn, w_gate, w_up, w_down, d_conv, d_att, tm_p)

        gb, u, kc, vc, ks, kw, q_t, vs_t, vw_t, gates_t = _in_proj_t(
            xp, lw["norm_mix"], lw["w_tok"], lw["w_chan"], lw["q_gain_t"], lw["k_gain"], nb, d_conv, d_att, tm_p)
        seq = lambda a: a.reshape(nb, t, LANES)
        kcmp = _compress_rows(seq(kc), lw["cmp_k"], True, False)
        vcmp_t = _compress_rows(seq(vc), lw["cmp_v"], False, True)
        ks_aug = jnp.concatenate(
            [jnp.pad(seq(ks).astype(BF16), ((0, 0), (TQ, 0), (0, 0))),
             jnp.broadcast_to(ptab["sel_onehot"][None], (nb, TQ + t, LANES))], axis=2)
        kw_aug = jnp.concatenate(
            [jnp.pad(seq(kw).astype(BF16), ((0, 0), (WINDOW, 0), (0, 0))),
             jnp.broadcast_to(ptab["win_flag"][None], (nb, WINDOW + t, LANES))], axis=2)
        ones_rows = lambda v_t, front: jnp.where(
            other, 1.0, jnp.pad(v_t, ((0, 0), (0, 0), (front, 0)))[:, None]).astype(BF16)
        o_t = _pattn(q_t, gates_t, kcmp, vcmp_t, ks_aug, ones_rows(vs_t, TQ), kw_aug, ones_rows(vw_t, WINDOW), ptab)
        o_att = o_t.reshape(nb * t, d_att)
        u3 = u.reshape(nb, t, d_conv)
        xp = _mix_ffn(xp, gb, u, u, u, o_att, lw, tm_p, tiles_per_seq=t // tm_p)
        n_keep = min(WINDOW, t)
        st_p.append((u3[:, -(CONV_WIDTH - 1):], kv(kc, nb, t), kv(vc, nb, t), kv(ks, nb, t), _token_major(vs_t),
                     kv(kw, nb, t)[:, -n_keep:], _token_major(vw_t[:, :, -n_keep:])))

        n_s = ns * t_new
        gb, u, kc, vc, ks, kw, q_t, vs_t, vw_t, gates_t = _in_proj_t(
            xs, lw["norm_mix"], lw["w_tok"], lw["w_chan"], lw["q_gain_t"][:, :n_s], lw["k_gain"], 1, d_conv, d_att, n_s)
        q, vs, vw, gates = q_t[0].T, vs_t[0].T, vw_t[0].T, gates_t[0].T
        base = l * n_pool
        kcmp = _compress_paged(cmp_pools[0], page_table, base, lw["cmp_k"], True)
        vcmp = _compress_paged(cmp_pools[1], page_table, base, lw["cmp_v"], False)
        qr = q.reshape(ns, t_new, GROUP, N_KV, HEAD_DIM).transpose(0, 2, 3, 1, 4)
        qm = (qr[:, :, :, :, None, :] * eye).reshape(ns, N_HEADS * t_new, LANES)
        gr = gates[:, :3 * N_HEADS].reshape(ns, t_new, 3, N_KV, GROUP).transpose(0, 2, 4, 3, 1)
        gr = jnp.broadcast_to(gr.reshape(ns, 3, N_HEADS * t_new, 1), (ns, 3, N_HEADS * t_new, LANES))
        kw_t, vw_t = new_t(kw), new_t(vw)
        o = _sattn(qm, kcmp, vcmp, l, win_t[0], win_t[1], kw_t, vw_t, new_t(ks), new_t(vs),
                   slc_pools[0], slc_pools[1], page_table, base, gr, stab, t_new, pos0)
        o6 = o.reshape(ns, GROUP, N_KV, t_new, N_KV, HEAD_DIM)
        o_att = jnp.stack([o6[:, :, g, :, g] for g in range(N_KV)], axis=2)
        o_att = o_att.transpose(0, 3, 1, 2, 4).reshape(ns * t_new, d_att)
        u3 = u.reshape(ns, t_new, d_conv)
        u1, u2 = _shifted(u3, state_conv[l])
        xs = _mix_ffn(xs, gb, u, u1.reshape(-1, d_conv), u2.reshape(-1, d_conv), o_att, lw, ns * t_new)
        new_conv = jnp.concatenate([state_conv[l], u3], axis=1)[:, -(CONV_WIDTH - 1):]
        n_keep = min(WINDOW, n_win + t_new)
        keep = lambda c_t, n_t: _token_major(jnp.concatenate([c_t, n_t[:, :, :t_new]], axis=2)[:, :, -n_keep:])
        st_s.append((new_conv, kv(kc, ns, t_new), kv(vc, ns, t_new), kv(ks, ns, t_new), kv(vs, ns, t_new),
                     keep(win_t[0][l], kw_t), keep(win_t[1][l], vw_t)))

    outs_p = [jnp.stack(a) for a in zip(*st_p)]
    outs_s = [jnp.stack(a) for a in zip(*st_s)]
    return (xp.reshape(nb, t, dm), xs.reshape(ns, t_new, dm), *outs_p, *outs_s)
```
